```python
import jax, jax.numpy as jnp
from jax import lax
import numpy as np

D_MODEL = 1024
BATCH = 8
SEQ = 4096
DEPTH = 4

CHUNK = 64
N_MIXERS = 2
EPS = 1e-6

SSD_EXPAND = 2
SSD_D_INNER = SSD_EXPAND * D_MODEL
SSD_HEAD_DIM = 64
SSD_N_HEADS = SSD_D_INNER // SSD_HEAD_DIM
SSD_N_GROUPS = 8
SSD_HEADS_PER_GROUP = SSD_N_HEADS // SSD_N_GROUPS
SSD_D_STATE = 128
SSD_CONV_W = 4
SSD_BC_DIM = SSD_N_GROUPS * SSD_D_STATE
SSD_CONV_DIM = SSD_D_INNER + 2 * SSD_BC_DIM
SSD_IN_DIM = SSD_D_INNER + SSD_CONV_DIM + SSD_N_HEADS

SC_WIDTH = D_MODEL
SC_CONV_W = 3

FFN_HIDDEN = 2816
FFN_CONV_W = 3

N_SSD_LAYERS = (DEPTH + 1) // 2
N_SC_LAYERS = DEPTH // 2

kernel_name = "hybrid_ssd_shortconv_convffn_sandwich"


def rms_norm(x, g):
    xf = x.astype(jnp.float32)
    y = xf * lax.rsqrt(jnp.mean(xf * xf, axis=-1, keepdims=True) + EPS)
    return (y * g.astype(jnp.float32)).astype(x.dtype)


def causal_dwconv(x, w, b=None):
    k = w.shape[0]
    length = x.shape[1]
    xp = jnp.pad(x, ((0, 0), (k - 1, 0), (0, 0)))
    out = xp[:, 0:length] * w[0]
    for t in range(1, k):
        out = out + xp[:, t:t + length] * w[t]
    if b is not None:
        out = out + b
    return out


def ssd_scan(x, dt, a_head, b_in, c_in):
    bsz, length, h, p = x.shape
    g, n = b_in.shape[2], b_in.shape[3]
    r = h // g
    nc = length // CHUNK
    xf = (x.astype(jnp.float32) * dt[..., None]).reshape(bsz, nc, CHUNK, g, r, p)
    a = (dt * a_head).reshape(bsz, nc, CHUNK, g, r)
    a = jnp.moveaxis(a, 2, -1)
    a_cs = jnp.cumsum(a, axis=-1)
    bc = b_in.astype(jnp.float32).reshape(bsz, nc, CHUNK, g, n)
    cc = c_in.astype(jnp.float32).reshape(bsz, nc, CHUNK, g, n)
    seg = a_cs[..., :, None] - a_cs[..., None, :]
    tri = jnp.tril(jnp.ones((CHUNK, CHUNK), dtype=bool))
    lmat = jnp.exp(jnp.where(tri, seg, -jnp.inf))
    cb = jnp.einsum("bclgn,bcsgn->bcgls", cc, bc)
    y_diag = jnp.einsum("bcgls,bcgrls,bcsgrp->bclgrp", cb, lmat, xf)
    decay_states = jnp.exp(a_cs[..., -1:] - a_cs)
    states = jnp.einsum("bclgn,bcgrl,bclgrp->bcgrpn", bc, decay_states, xf)
    chunk_decay = jnp.exp(a_cs[..., -1])

    def step(s, inp):
        st, dec = inp
        return s * dec[..., None, None] + st, s

    init = jnp.zeros((bsz, g, r, p, n), jnp.float32)
    _, prev = lax.scan(step, init, (jnp.moveaxis(states, 1, 0), jnp.moveaxis(chunk_decay, 1, 0)))
    prev = jnp.moveaxis(prev, 0, 1)
    y_off = jnp.einsum("bclgn,bcgrpn,bcgrl->bclgrp", cc, prev, jnp.exp(a_cs))
    return (y_diag + y_off).reshape(bsz, length, h, p)


def ssd_mixer(h, w_in, conv_w, conv_b, dt_bias, a_log, d_skip, norm_w, w_out):
    bsz, length, _ = h.shape
    zxbcdt = h @ w_in
    z = zxbcdt[..., :SSD_D_INNER]
    xbc = zxbcdt[..., SSD_D_INNER:SSD_D_INNER + SSD_CONV_DIM]
    dt_raw = zxbcdt[..., SSD_D_INNER + SSD_CONV_DIM:]
    xbc = jax.nn.silu(causal_dwconv(xbc, conv_w, conv_b))
    xs = xbc[..., :SSD_D_INNER].reshape(bsz, length, SSD_N_HEADS, SSD_HEAD_DIM)
    bs = xbc[..., SSD_D_INNER:SSD_D_INNER + SSD_BC_DIM].reshape(bsz, length, SSD_N_GROUPS, SSD_D_STATE)
    cs = xbc[..., SSD_D_INNER + SSD_BC_DIM:].reshape(bsz, length, SSD_N_GROUPS, SSD_D_STATE)
    dt = jax.nn.softplus(dt_raw.astype(jnp.float32) + dt_bias.astype(jnp.float32))
    a_head = -jnp.exp(a_log.astype(jnp.float32))
    y = ssd_scan(xs, dt, a_head, bs, cs)
    y = y + xs.astype(jnp.float32) * d_skip.astype(jnp.float32)[:, None]
    y = y.reshape(bsz, length, SSD_D_INNER).astype(h.dtype)
    y = rms_norm(y * jax.nn.silu(z), norm_w)
    return y @ w_out


def shortconv_mixer(h, w_in, conv_w, w_out):
    bcv = h @ w_in
    gb = bcv[..., :SC_WIDTH]
    gc = bcv[..., SC_WIDTH:2 * SC_WIDTH]
    v = bcv[..., 2 * SC_WIDTH:]
    u = causal_dwconv(gc * v, conv_w)
    return (gb * u) @ w_out


def conv_ffn(h, w_up, conv_w, conv_b, w_down):
    up = h @ w_up
    gate = causal_dwconv(up[..., :FFN_HIDDEN], conv_w, conv_b)
    val = up[..., FFN_HIDDEN:]
    return (jax.nn.silu(gate) * val) @ w_down


def _fwd_setup_inputs(seed: int = 0) -> dict:
    key = jax.random.key(seed)
    ks = jax.random.split(key, 20)
    f32 = jnp.float32
    nrm = lambda k, shape, s: jax.random.normal(k, shape, f32) * s
    x = jax.random.normal(ks[0], (BATCH, SEQ, D_MODEL), f32)
    gains = lambda k: 1.0 + nrm(k, (DEPTH, D_MODEL), 0.02)
    dt0 = jnp.exp(jax.random.uniform(ks[8], (N_SSD_LAYERS, SSD_N_HEADS), f32,
                                     np.float32(np.log(1e-3)), np.float32(np.log(1e-1))))
    return {
        "x": x,
        "mix_pre_g": gains(ks[1]),
        "mix_post_g": gains(ks[2]),
        "ffn_pre_g": gains(ks[3]),
        "ffn_post_g": gains(ks[4]),
        "ssd_w_in": nrm(ks[5], (N_SSD_LAYERS, D_MODEL, SSD_IN_DIM), D_MODEL ** -0.5),
        "ssd_conv_w": nrm(ks[6], (N_SSD_LAYERS, SSD_CONV_W, SSD_CONV_DIM), SSD_CONV_W ** -0.5),
        "ssd_conv_b": nrm(ks[7], (N_SSD_LAYERS, SSD_CONV_DIM), 0.01),
        "ssd_dt_bias": dt0 + jnp.log(-jnp.expm1(-dt0)),
        "ssd_A_log": jnp.log(jax.random.uniform(ks[9], (N_SSD_LAYERS, SSD_N_HEADS), f32, 1.0, 16.0)),
        "ssd_D": 1.0 + nrm(ks[10], (N_SSD_LAYERS, SSD_N_HEADS), 0.1),
        "ssd_norm_w": 1.0 + nrm(ks[11], (N_SSD_LAYERS, SSD_D_INNER), 0.02),
        "ssd_w_out": nrm(ks[12], (N_SSD_LAYERS, SSD_D_INNER, D_MODEL), SSD_D_INNER ** -0.5),
        "sc_w_in": nrm(ks[13], (N_SC_LAYERS, D_MODEL, 3 * SC_WIDTH), D_MODEL ** -0.5),
        "sc_conv_w": nrm(ks[14], (N_SC_LAYERS, SC_CONV_W, SC_WIDTH), SC_CONV_W ** -0.5),
        "sc_w_out": nrm(ks[15], (N_SC_LAYERS, SC_WIDTH, D_MODEL), SC_WIDTH ** -0.5),
        "ffn_w_up": nrm(ks[16], (DEPTH, D_MODEL, 2 * FFN_HIDDEN), D_MODEL ** -0.5),
        "ffn_conv_w": nrm(ks[17], (DEPTH, FFN_CONV_W, FFN_HIDDEN), FFN_CONV_W ** -0.5),
        "ffn_conv_b": nrm(ks[18], (DEPTH, FFN_HIDDEN), 0.01),
        "ffn_w_down": nrm(ks[19], (DEPTH, FFN_HIDDEN, D_MODEL), FFN_HIDDEN ** -0.5),
    }


def _fwd_reference(x, mix_pre_g, mix_post_g, ffn_pre_g, ffn_post_g,
              ssd_w_in, ssd_conv_w, ssd_conv_b, ssd_dt_bias, ssd_A_log, ssd_D,
              ssd_norm_w, ssd_w_out, sc_w_in, sc_conv_w, sc_w_out,
              ffn_w_up, ffn_conv_w, ffn_conv_b, ffn_w_down):
    for i in range(DEPTH):
        j = i // N_MIXERS
        h = rms_norm(x, mix_pre_g[i])
        if i % N_MIXERS == 0:
            m = ssd_mixer(h, ssd_w_in[j], ssd_conv_w[j], ssd_conv_b[j], ssd_dt_bias[j],
                          ssd_A_log[j], ssd_D[j], ssd_norm_w[j], ssd_w_out[j])
        else:
            m = shortconv_mixer(h, sc_w_in[j], sc_conv_w[j], sc_w_out[j])
        x = x + rms_norm(m, mix_post_g[i])
        f = conv_ffn(rms_norm(x, ffn_pre_g[i]), ffn_w_up[i], ffn_conv_w[i], ffn_conv_b[i], ffn_w_down[i])
        x = x + rms_norm(f, ffn_post_g[i])
    return x


import jax as _jax
import jax.numpy as _jnp

TWIN_FORMAT = 'train_step'
FWD_PARAMS = ['x', 'mix_pre_g', 'mix_post_g', 'ffn_pre_g', 'ffn_post_g', 'ssd_w_in', 'ssd_conv_w', 'ssd_conv_b', 'ssd_dt_bias', 'ssd_A_log', 'ssd_D', 'ssd_norm_w', 'ssd_w_out', 'sc_w_in', 'sc_conv_w', 'sc_w_out', 'ffn_w_up', 'ffn_conv_w', 'ffn_conv_b', 'ffn_w_down']
TWIN_WEIGHTS = ['mix_pre_g', 'mix_post_g', 'ffn_pre_g', 'ffn_post_g', 'ssd_w_in', 'ssd_conv_w', 'ssd_conv_b', 'ssd_dt_bias', 'ssd_A_log', 'ssd_D', 'ssd_norm_w', 'ssd_w_out', 'sc_w_in', 'sc_conv_w', 'sc_w_out', 'ffn_w_up', 'ffn_conv_w', 'ffn_conv_b', 'ffn_w_down']
TWIN_DIFF_INPUT = 'x'
TWIN_INPUTS = ['x', 'mix_pre_g', 'mix_post_g', 'ffn_pre_g', 'ffn_post_g', 'ssd_w_in', 'ssd_conv_w', 'ssd_conv_b', 'ssd_dt_bias', 'ssd_A_log', 'ssd_D', 'ssd_norm_w', 'ssd_w_out', 'sc_w_in', 'sc_conv_w', 'sc_w_out', 'ffn_w_up', 'ffn_conv_w', 'ffn_conv_b', 'ffn_w_down', 'loss_target', 'm_mix_pre_g', 'm_mix_post_g', 'm_ffn_pre_g', 'm_ffn_post_g', 'm_ssd_w_in', 'm_ssd_conv_w', 'm_ssd_conv_b', 'm_ssd_dt_bias', 'm_ssd_A_log', 'm_ssd_D', 'm_ssd_norm_w', 'm_ssd_w_out', 'm_sc_w_in', 'm_sc_conv_w', 'm_sc_w_out', 'm_ffn_w_up', 'm_ffn_conv_w', 'm_ffn_conv_b', 'm_ffn_w_down', 'v_mix_pre_g', 'v_mix_post_g', 'v_ffn_pre_g', 'v_ffn_post_g', 'v_ssd_w_in', 'v_ssd_conv_w', 'v_ssd_conv_b', 'v_ssd_dt_bias', 'v_ssd_A_log', 'v_ssd_D', 'v_ssd_norm_w', 'v_ssd_w_out', 'v_sc_w_in', 'v_sc_conv_w', 'v_sc_w_out', 'v_ffn_w_up', 'v_ffn_conv_w', 'v_ffn_conv_b', 'v_ffn_w_down']
TWIN_OUTPUTS = ['loss', 'grad_x', 'grad_mix_pre_g', 'grad_mix_post_g', 'grad_ffn_pre_g', 'grad_ffn_post_g', 'grad_ssd_w_in', 'grad_ssd_conv_w', 'grad_ssd_conv_b', 'grad_ssd_dt_bias', 'grad_ssd_A_log', 'grad_ssd_D', 'grad_ssd_norm_w', 'grad_ssd_w_out', 'grad_sc_w_in', 'grad_sc_conv_w', 'grad_sc_w_out', 'grad_ffn_w_up', 'grad_ffn_conv_w', 'grad_ffn_conv_b', 'grad_ffn_w_down', 'delta_mix_pre_g', 'delta_mix_post_g', 'delta_ffn_pre_g', 'delta_ffn_post_g', 'delta_ssd_w_in', 'delta_ssd_conv_w', 'delta_ssd_conv_b', 'delta_ssd_dt_bias', 'delta_ssd_A_log', 'delta_ssd_D', 'delta_ssd_norm_w', 'delta_ssd_w_out', 'delta_sc_w_in', 'delta_sc_conv_w', 'delta_sc_w_out', 'delta_ffn_w_up', 'delta_ffn_conv_w', 'delta_ffn_conv_b', 'delta_ffn_w_down', 'new_m_mix_pre_g', 'new_m_mix_post_g', 'new_m_ffn_pre_g', 'new_m_ffn_post_g', 'new_m_ssd_w_in', 'new_m_ssd_conv_w', 'new_m_ssd_conv_b', 'new_m_ssd_dt_bias', 'new_m_ssd_A_log', 'new_m_ssd_D', 'new_m_ssd_norm_w', 'new_m_ssd_w_out', 'new_m_sc_w_in', 'new_m_sc_conv_w', 'new_m_sc_w_out', 'new_m_ffn_w_up', 'new_m_ffn_conv_w', 'new_m_ffn_conv_b', 'new_m_ffn_w_down', 'new_v_mix_pre_g', 'new_v_mix_post_g', 'new_v_ffn_pre_g', 'new_v_ffn_post_g', 'new_v_ssd_w_in', 'new_v_ssd_conv_w', 'new_v_ssd_conv_b', 'new_v_ssd_dt_bias', 'new_v_ssd_A_log', 'new_v_ssd_D', 'new_v_ssd_norm_w', 'new_v_ssd_w_out', 'new_v_sc_w_in', 'new_v_sc_conv_w', 'new_v_sc_w_out', 'new_v_ffn_w_up', 'new_v_ffn_conv_w', 'new_v_ffn_conv_b', 'new_v_ffn_w_down']
TWIN_LEAF_KINDS = {'loss': 'loss', 'grad_x': 'grad_x', 'grad_mix_pre_g': 'grad_w', 'grad_mix_post_g': 'grad_w', 'grad_ffn_pre_g': 'grad_w', 'grad_ffn_post_g': 'grad_w', 'grad_ssd_w_in': 'grad_w', 'grad_ssd_conv_w': 'grad_w', 'grad_ssd_conv_b': 'grad_w', 'grad_ssd_dt_bias': 'grad_w', 'grad_ssd_A_log': 'grad_w', 'grad_ssd_D': 'grad_w', 'grad_ssd_norm_w': 'grad_w', 'grad_ssd_w_out': 'grad_w', 'grad_sc_w_in': 'grad_w', 'grad_sc_conv_w': 'grad_w', 'grad_sc_w_out': 'grad_w', 'grad_ffn_w_up': 'grad_w', 'grad_ffn_conv_w': 'grad_w', 'grad_ffn_conv_b': 'grad_w', 'grad_ffn_w_down': 'grad_w', 'delta_mix_pre_g': 'delta_w', 'delta_mix_post_g': 'delta_w', 'delta_ffn_pre_g': 'delta_w', 'delta_ffn_post_g': 'delta_w', 'delta_ssd_w_in': 'delta_w', 'delta_ssd_conv_w': 'delta_w', 'delta_ssd_conv_b': 'delta_w', 'delta_ssd_dt_bias': 'delta_w', 'delta_ssd_A_log': 'delta_w', 'delta_ssd_D': 'delta_w', 'delta_ssd_norm_w': 'delta_w', 'delta_ssd_w_out': 'delta_w', 'delta_sc_w_in': 'delta_w', 'delta_sc_conv_w': 'delta_w', 'delta_sc_w_out': 'delta_w', 'delta_ffn_w_up': 'delta_w', 'delta_ffn_conv_w': 'delta_w', 'delta_ffn_conv_b': 'delta_w', 'delta_ffn_w_down': 'delta_w', 'new_m_mix_pre_g': 'new_m', 'new_m_mix_post_g': 'new_m', 'new_m_ffn_pre_g': 'new_m', 'new_m_ffn_post_g': 'new_m', 'new_m_ssd_w_in': 'new_m', 'new_m_ssd_conv_w': 'new_m', 'new_m_ssd_conv_b': 'new_m', 'new_m_ssd_dt_bias': 'new_m', 'new_m_ssd_A_log': 'new_m', 'new_m_ssd_D': 'new_m', 'new_m_ssd_norm_w': 'new_m', 'new_m_ssd_w_out': 'new_m', 'new_m_sc_w_in': 'new_m', 'new_m_sc_conv_w': 'new_m', 'new_m_sc_w_out': 'new_m', 'new_m_ffn_w_up': 'new_m', 'new_m_ffn_conv_w': 'new_m', 'new_m_ffn_conv_b': 'new_m', 'new_m_ffn_w_down': 'new_m', 'new_v_mix_pre_g': 'new_v', 'new_v_mix_post_g': 'new_v', 'new_v_ffn_pre_g': 'new_v', 'new_v_ffn_post_g': 'new_v', 'new_v_ssd_w_in': 'new_v', 'new_v_ssd_conv_w': 'new_v', 'new_v_ssd_conv_b': 'new_v', 'new_v_ssd_dt_bias': 'new_v', 'new_v_ssd_A_log': 'new_v', 'new_v_ssd_D': 'new_v', 'new_v_ssd_norm_w': 'new_v', 'new_v_ssd_w_out': 'new_v', 'new_v_sc_w_in': 'new_v', 'new_v_sc_conv_w': 'new_v', 'new_v_sc_w_out': 'new_v', 'new_v_ffn_w_up': 'new_v', 'new_v_ffn_conv_w': 'new_v', 'new_v_ffn_conv_b': 'new_v', 'new_v_ffn_w_down': 'new_v'}


def _forward(args):
    return _fwd_reference(*[args[k] for k in FWD_PARAMS])


def _output_shape():
    def fwd():
        inp = _fwd_setup_inputs(0)
        return _fwd_reference(*[inp[k] for k in FWD_PARAMS])
    out = _jax.eval_shape(fwd)
    return out.shape, out.dtype

N_MICROBATCH = 1
ADAM_LR = 0.001
ADAM_B1 = 0.9
ADAM_B2 = 0.999
ADAM_EPS = 1e-08
ADAM_WD = 0.01
ADAM_STEP = 10
PER_EXAMPLE_BATCH_AXIS = {'x': 0, 'loss_target': 0}
SHARED_INPUTS = []
_WEIGHT_DTYPES = {'mix_pre_g': _jnp.float32, 'mix_post_g': _jnp.float32, 'ffn_pre_g': _jnp.float32, 'ffn_post_g': _jnp.float32, 'ssd_w_in': _jnp.float32, 'ssd_conv_w': _jnp.float32, 'ssd_conv_b': _jnp.float32, 'ssd_dt_bias': _jnp.float32, 'ssd_A_log': _jnp.float32, 'ssd_D': _jnp.float32, 'ssd_norm_w': _jnp.float32, 'ssd_w_out': _jnp.float32, 'sc_w_in': _jnp.float32, 'sc_conv_w': _jnp.float32, 'sc_w_out': _jnp.float32, 'ffn_w_up': _jnp.float32, 'ffn_conv_w': _jnp.float32, 'ffn_conv_b': _jnp.float32, 'ffn_w_down': _jnp.float32}
MOMENT_SCALE = {'mix_pre_g': 2.345038e+00, 'mix_post_g': 3.185711e+01, 'ffn_pre_g': 1.811805e+00, 'ffn_post_g': 3.196657e+01, 'ssd_w_in': 1.133151e+00, 'ssd_conv_w': 1.052422e+00, 'ssd_conv_b': 2.067868e+00, 'ssd_dt_bias': 3.012793e+00, 'ssd_A_log': 7.019970e+00, 'ssd_D': 8.354920e+00, 'ssd_norm_w': 1.501351e+00, 'ssd_w_out': 2.366016e+00, 'sc_w_in': 9.735301e-01, 'sc_conv_w': 9.905453e-01, 'sc_w_out': 1.029199e+00, 'ffn_w_up': 7.584282e-01, 'ffn_conv_w': 8.075247e-01, 'ffn_conv_b': 1.328256e+00, 'ffn_w_down': 1.375765e+00}


def _to_microbatches(a, axis):
    t = _jnp.moveaxis(a, axis, 0)
    t = t.reshape((N_MICROBATCH, t.shape[0] // N_MICROBATCH) + t.shape[1:])
    return _jnp.moveaxis(t, 1, axis + 1)


def setup_inputs(seed: int = 0) -> dict:
    inp = _fwd_setup_inputs(seed)
    key = _jax.random.fold_in(_jax.random.key(seed), 7919)
    shape, _ = _output_shape()
    out = dict(inp)
    out["loss_target"] = _jax.random.normal(_jax.random.fold_in(key, 0), shape, _jnp.float32)
    for i, name in enumerate(TWIN_WEIGHTS):
        w = inp[name].astype(_jnp.float32)
        if MOMENT_SCALE is None:
            s = _jnp.sqrt(_jnp.mean(_jnp.square(w)) + 1e-30)
        else:
            s = MOMENT_SCALE[name]
        km, kv = _jax.random.split(_jax.random.fold_in(key, i + 1))
        out[name] = w
        out["m_" + name] = s * _jax.random.normal(km, w.shape, _jnp.float32)
        out["v_" + name] = (s * s) * _jax.random.uniform(kv, w.shape, _jnp.float32, 0.5, 1.5)
    if N_MICROBATCH > 1:
        for name, axis in PER_EXAMPLE_BATCH_AXIS.items():
            out[name] = _to_microbatches(out[name], axis)
    return {'x': out['x'], 'mix_pre_g': out['mix_pre_g'], 'mix_post_g': out['mix_post_g'], 'ffn_pre_g': out['ffn_pre_g'], 'ffn_post_g': out['ffn_post_g'], 'ssd_w_in': out['ssd_w_in'], 'ssd_conv_w': out['ssd_conv_w'], 'ssd_conv_b': out['ssd_conv_b'], 'ssd_dt_bias': out['ssd_dt_bias'], 'ssd_A_log': out['ssd_A_log'], 'ssd_D': out['ssd_D'], 'ssd_norm_w': out['ssd_norm_w'], 'ssd_w_out': out['ssd_w_out'], 'sc_w_in': out['sc_w_in'], 'sc_conv_w': out['sc_conv_w'], 'sc_w_out': out['sc_w_out'], 'ffn_w_up': out['ffn_w_up'], 'ffn_conv_w': out['ffn_conv_w'], 'ffn_conv_b': out['ffn_conv_b'], 'ffn_w_down': out['ffn_w_down'], 'loss_target': out['loss_target'], 'm_mix_pre_g': out['m_mix_pre_g'], 'm_mix_post_g': out['m_mix_post_g'], 'm_ffn_pre_g': out['m_ffn_pre_g'], 'm_ffn_post_g': out['m_ffn_post_g'], 'm_ssd_w_in': out['m_ssd_w_in'], 'm_ssd_conv_w': out['m_ssd_conv_w'], 'm_ssd_conv_b': out['m_ssd_conv_b'], 'm_ssd_dt_bias': out['m_ssd_dt_bias'], 'm_ssd_A_log': out['m_ssd_A_log'], 'm_ssd_D': out['m_ssd_D'], 'm_ssd_norm_w': out['m_ssd_norm_w'], 'm_ssd_w_out': out['m_ssd_w_out'], 'm_sc_w_in': out['m_sc_w_in'], 'm_sc_conv_w': out['m_sc_conv_w'], 'm_sc_w_out': out['m_sc_w_out'], 'm_ffn_w_up': out['m_ffn_w_up'], 'm_ffn_conv_w': out['m_ffn_conv_w'], 'm_ffn_conv_b': out['m_ffn_conv_b'], 'm_ffn_w_down': out['m_ffn_w_down'], 'v_mix_pre_g': out['v_mix_pre_g'], 'v_mix_post_g': out['v_mix_post_g'], 'v_ffn_pre_g': out['v_ffn_pre_g'], 'v_ffn_post_g': out['v_ffn_post_g'], 'v_ssd_w_in': out['v_ssd_w_in'], 'v_ssd_conv_w': out['v_ssd_conv_w'], 'v_ssd_conv_b': out['v_ssd_conv_b'], 'v_ssd_dt_bias': out['v_ssd_dt_bias'], 'v_ssd_A_log': out['v_ssd_A_log'], 'v_ssd_D': out['v_ssd_D'], 'v_ssd_norm_w': out['v_ssd_norm_w'], 'v_ssd_w_out': out['v_ssd_w_out'], 'v_sc_w_in': out['v_sc_w_in'], 'v_sc_conv_w': out['v_sc_conv_w'], 'v_sc_w_out': out['v_sc_w_out'], 'v_ffn_w_up': out['v_ffn_w_up'], 'v_ffn_conv_w': out['v_ffn_conv_w'], 'v_ffn_conv_b': out['v_ffn_conv_b'], 'v_ffn_w_down': out['v_ffn_w_down']}


def _loss(weights, diff, rest, loss_target):
    with _jax.named_scope("forward"):
        args = {**rest, TWIN_DIFF_INPUT: diff, **{k: w.astype(_WEIGHT_DTYPES[k]) for k, w in weights.items()}}
        y = _forward(args)
    with _jax.named_scope("loss_head"):
        err = _jnp.square(y.astype(_jnp.float32) - loss_target)
        return 0.5 * _jnp.sum(_jnp.mean(err, axis=-1)) if err.ndim else 0.5 * err


def _adamw(w, g, m, v):
    m = ADAM_B1 * m + (1.0 - ADAM_B1) * g
    v = ADAM_B2 * v + (1.0 - ADAM_B2) * _jnp.square(g)
    m_hat = m / (1.0 - ADAM_B1 ** ADAM_STEP)
    v_hat = v / (1.0 - ADAM_B2 ** ADAM_STEP)
    delta = -ADAM_LR * (m_hat / (_jnp.sqrt(v_hat) + ADAM_EPS) + ADAM_WD * w)
    return delta, m, v


def reference(x, mix_pre_g, mix_post_g, ffn_pre_g, ffn_post_g, ssd_w_in, ssd_conv_w, ssd_conv_b, ssd_dt_bias, ssd_A_log, ssd_D, ssd_norm_w, ssd_w_out, sc_w_in, sc_conv_w, sc_w_out, ffn_w_up, ffn_conv_w, ffn_conv_b, ffn_w_down, loss_target, m_mix_pre_g, m_mix_post_g, m_ffn_pre_g, m_ffn_post_g, m_ssd_w_in, m_ssd_conv_w, m_ssd_conv_b, m_ssd_dt_bias, m_ssd_A_log, m_ssd_D, m_ssd_norm_w, m_ssd_w_out, m_sc_w_in, m_sc_conv_w, m_sc_w_out, m_ffn_w_up, m_ffn_conv_w, m_ffn_conv_b, m_ffn_w_down, v_mix_pre_g, v_mix_post_g, v_ffn_pre_g, v_ffn_post_g, v_ssd_w_in, v_ssd_conv_w, v_ssd_conv_b, v_ssd_dt_bias, v_ssd_A_log, v_ssd_D, v_ssd_norm_w, v_ssd_w_out, v_sc_w_in, v_sc_conv_w, v_sc_w_out, v_ffn_w_up, v_ffn_conv_w, v_ffn_conv_b, v_ffn_w_down):
    given = dict(x=x, mix_pre_g=mix_pre_g, mix_post_g=mix_post_g, ffn_pre_g=ffn_pre_g, ffn_post_g=ffn_post_g, ssd_w_in=ssd_w_in, ssd_conv_w=ssd_conv_w, ssd_conv_b=ssd_conv_b, ssd_dt_bias=ssd_dt_bias, ssd_A_log=ssd_A_log, ssd_D=ssd_D, ssd_norm_w=ssd_norm_w, ssd_w_out=ssd_w_out, sc_w_in=sc_w_in, sc_conv_w=sc_conv_w, sc_w_out=sc_w_out, ffn_w_up=ffn_w_up, ffn_conv_w=ffn_conv_w, ffn_conv_b=ffn_conv_b, ffn_w_down=ffn_w_down, loss_target=loss_target, m_mix_pre_g=m_mix_pre_g, m_mix_post_g=m_mix_post_g, m_ffn_pre_g=m_ffn_pre_g, m_ffn_post_g=m_ffn_post_g, m_ssd_w_in=m_ssd_w_in, m_ssd_conv_w=m_ssd_conv_w, m_ssd_conv_b=m_ssd_conv_b, m_ssd_dt_bias=m_ssd_dt_bias, m_ssd_A_log=m_ssd_A_log, m_ssd_D=m_ssd_D, m_ssd_norm_w=m_ssd_norm_w, m_ssd_w_out=m_ssd_w_out, m_sc_w_in=m_sc_w_in, m_sc_conv_w=m_sc_conv_w, m_sc_w_out=m_sc_w_out, m_ffn_w_up=m_ffn_w_up, m_ffn_conv_w=m_ffn_conv_w, m_ffn_conv_b=m_ffn_conv_b, m_ffn_w_down=m_ffn_w_down, v_mix_pre_g=v_mix_pre_g, v_mix_post_g=v_mix_post_g, v_ffn_pre_g=v_ffn_pre_g, v_ffn_post_g=v_ffn_post_g, v_ssd_w_in=v_ssd_w_in, v_ssd_conv_w=v_ssd_conv_w, v_ssd_conv_b=v_ssd_conv_b, v_ssd_dt_bias=v_ssd_dt_bias, v_ssd_A_log=v_ssd_A_log, v_ssd_D=v_ssd_D, v_ssd_norm_w=v_ssd_norm_w, v_ssd_w_out=v_ssd_w_out, v_sc_w_in=v_sc_w_in, v_sc_conv_w=v_sc_conv_w, v_sc_w_out=v_sc_w_out, v_ffn_w_up=v_ffn_w_up, v_ffn_conv_w=v_ffn_conv_w, v_ffn_conv_b=v_ffn_conv_b, v_ffn_w_down=v_ffn_w_down)
    weights = {n: given[n] for n in TWIN_WEIGHTS}
    shared = {n: given[n] for n in SHARED_INPUTS}
    per_example = {n: given[n] for n in ['x']}
    grad_fn = _jax.value_and_grad(_loss, argnums=(0, 1))

    def one_microbatch(ex, loss_target):
        ex = dict(ex)
        diff = ex.pop(TWIN_DIFF_INPUT)
        return grad_fn(weights, diff, {**shared, **ex}, loss_target)

    if N_MICROBATCH == 1:
        loss, (grad_w, grad_x) = one_microbatch(per_example, given["loss_target"])
    else:
        def body(carry, xs):
            loss_sum, grad_sum = carry
            l_k, (gw_k, gx_k) = one_microbatch(xs[0], xs[1])
            with _jax.named_scope("update"):
                return (loss_sum + l_k, _jax.tree.map(_jnp.add, grad_sum, gw_k)), gx_k

        init = (_jnp.zeros((), _jnp.float32), _jax.tree.map(_jnp.zeros_like, weights))
        (loss, grad_w), grad_x = _jax.lax.scan(body, init, (per_example, given["loss_target"]))
    with _jax.named_scope("update"):
        delta_w, new_m, new_v = {}, {}, {}
        for n in TWIN_WEIGHTS:
            delta_w[n], new_m[n], new_v[n] = _adamw(weights[n], grad_w[n], given["m_" + n], given["v_" + n])
    return (loss, grad_x, *[grad_w[n] for n in TWIN_WEIGHTS], *[delta_w[n] for n in TWIN_WEIGHTS],
            *[new_m[n] for n in TWIN_WEIGHTS], *[new_v[n] for n in TWIN_WEIGHTS])
```

```python
import functools

import jax
import jax.numpy as jnp
from jax import lax
from jax.experimental import pallas as pl
from jax.experimental.pallas import tpu as pltpu

F32 = jnp.float32
BF16 = jnp.bfloat16

EPS = 1e-6
D_MODEL = 1024
DEPTH = 4
N_DEV = 8
CHUNK = 64
SSD_DI = 2048
SSD_H = 32
SSD_P = 64
SSD_G = 8
SSD_R = SSD_H // SSD_G
SSD_N = 128
SSD_CONV = SSD_DI + 2 * SSD_G * SSD_N
SSD_IN = SSD_DI + SSD_CONV + SSD_H
LANES = 128
SSD_IN_PAD = -(-SSD_IN // LANES) * LANES
SSD_KW = 4
SC_KW = 3
FFN_F = 2816
FFN_KW = 3
TL = 256
HALO = 8
VMEM_LIMIT = 60 * 1024 * 1024

ADAM_LR = 0.001
ADAM_B1 = 0.9
ADAM_B2 = 0.999
ADAM_EPS = 1e-08
ADAM_WD = 0.01
ADAM_STEP = 10

MESH = pl.DeviceIdType.MESH


def _rms(x, g):
    r = lax.rsqrt(jnp.mean(x * x, axis=-1, keepdims=True) + EPS)
    return x * r * g


def _rms_bwd(x, g, dy):
    r = lax.rsqrt(jnp.mean(x * x, axis=-1, keepdims=True) + EPS)
    xh = x * r
    dg = jnp.sum(dy * xh, axis=0, keepdims=True)
    dxh = dy * g
    dx = r * (dxh - xh * jnp.mean(dxh * xh, axis=-1, keepdims=True))
    return dx, dg


def _mm(a, b):
    return jnp.dot(a, b, preferred_element_type=F32)


def _mm_nt(a, b):
    return lax.dot_general(a, b, (((1,), (1,)), ((), ())), preferred_element_type=F32)


def _mm_tn(a, b):
    return lax.dot_general(a, b, (((0,), (0,)), ((), ())), preferred_element_type=F32)


def _silu_parts(x):
    sg = jax.nn.sigmoid(x)
    return x * sg, sg * (1.0 + x * (1.0 - sg))


def _conv_fwd(ext, w, kw, tl):
    base = HALO - (kw - 1)
    out = ext[base:base + tl] * w[0:1]
    for j in range(1, kw):
        out = out + ext[base + j:base + j + tl] * w[j:j + 1]
    return out


def _conv_bwd_in(extd, w, kw, tl):
    out = extd[kw - 1:kw - 1 + tl] * w[0:1]
    for j in range(1, kw):
        out = out + extd[kw - 1 - j:kw - 1 - j + tl] * w[j:j + 1]
    return out


def _conv_bwd_w(ext, dy, kw, tl):
    base = HALO - (kw - 1)
    return jnp.concatenate(
        [jnp.sum(dy * ext[base + j:base + j + tl], axis=0, keepdims=True) for j in range(kw)], axis=0)


def _emit_row_shards(acc_ref, out_ref, stage_ref):
    rows = out_ref.shape[1]
    for k in range(N_DEV):
        stage_ref[...] = acc_ref[k * rows:(k + 1) * rows, :].astype(BF16)
        pltpu.sync_copy(stage_ref, out_ref.at[k])


def _emit_col_shards(acc_ref, out_ref, stage_ref):
    cols = out_ref.shape[2]
    for k in range(N_DEV):
        stage_ref[...] = acc_ref[:, k * cols:(k + 1) * cols].astype(BF16)
        pltpu.sync_copy(stage_ref, out_ref.at[k])


def _res(shape):
    nd = len(shape)
    return pl.BlockSpec(shape, lambda i: (0,) * nd, pipeline_mode=pl.Buffered(1))


def _small(shape):
    nd = len(shape)
    return pl.BlockSpec(shape, lambda i: (0,) * nd)


def _tile(n):
    return pl.BlockSpec((TL, n), lambda i: (i, 0))


def _rtile(n, nt):
    return pl.BlockSpec((TL, n), lambda i: (nt - 1 - i, 0))


def _halo_before(n, nt, reverse):
    per = TL // HALO
    if reverse:
        return pl.BlockSpec((HALO, n), lambda i: (jnp.maximum((nt - 1 - i) * per - 1, 0), 0))
    return pl.BlockSpec((HALO, n), lambda i: (jnp.maximum(i * per - 1, 0), 0))


_ANY = pl.BlockSpec(memory_space=pl.ANY)


def _seq_params():
    return pltpu.CompilerParams(dimension_semantics=("arbitrary",), vmem_limit_bytes=VMEM_LIMIT)


def _sds(shape, dtype=F32):
    return jax.ShapeDtypeStruct(shape, dtype)


def ffn_fwd(x, g_pre, w_up, conv_w, conv_b, w_down, g_post, name):
    L = x.shape[0]
    nt = L // TL
    F = FFN_F

    def body(x_ref, gpre_ref, wup_ref, cw_ref, cb_ref, wdn_ref, gpost_ref, up_ref, f_ref, xn_ref, carry_ref):
        i = pl.program_id(0)

        @pl.when(i == 0)
        def _():
            carry_ref[...] = jnp.zeros_like(carry_ref)

        x = x_ref[...]
        h = _rms(x, gpre_ref[...]).astype(BF16)
        up = _mm(h, wup_ref[...])
        up_ref[...] = up
        ug = up[:, :F]
        val = up[:, F:]
        ext = jnp.concatenate([carry_ref[...], ug], axis=0)
        gate = _conv_fwd(ext, cw_ref[...], FFN_KW, TL) + cb_ref[...]
        carry_ref[...] = ug[TL - HALO:, :]
        a = (gate * jax.nn.sigmoid(gate) * val).astype(BF16)
        f = _mm(a, wdn_ref[...])
        f_ref[...] = f
        xn_ref[...] = x + _rms(f, gpost_ref[...])

    return pl.pallas_call(
        body, name=name, grid=(nt,),
        in_specs=[_tile(D_MODEL), _small((1, D_MODEL)), _res((D_MODEL, 2 * F)), _small((FFN_KW, F)), _small((1, F)),
                  _res((F, D_MODEL)), _small((1, D_MODEL))],
        out_specs=[_tile(2 * F), _tile(D_MODEL), _tile(D_MODEL)],
        out_shape=[_sds((L, 2 * F)), _sds((L, D_MODEL)), _sds((L, D_MODEL))],
        scratch_shapes=[pltpu.VMEM((HALO, F), F32)],
        compiler_params=_seq_params(),
    )(x, g_pre, w_up, conv_w, conv_b, w_down, g_post)


def ffn_bwd1(dxo, f, up, g_post, w_down, conv_w, conv_b, name):
    L = dxo.shape[0]
    nt = L // TL
    F = FFN_F
    rows = F // N_DEV

    def body(dxo_ref, f_ref, up_ref, halo_ref, gpost_ref, wdn_ref, cw_ref, cb_ref,
             dup_ref, dwdn_ref, dgp_ref, dcw_ref, dcb_ref, acc_ref, carry_ref, stage_ref):
        i = pl.program_id(0)
        t = nt - 1 - i

        @pl.when(i == 0)
        def _():
            acc_ref[...] = jnp.zeros_like(acc_ref)
            carry_ref[...] = jnp.zeros_like(carry_ref)
            dgp_ref[...] = jnp.zeros_like(dgp_ref)
            dcw_ref[...] = jnp.zeros_like(dcw_ref)
            dcb_ref[...] = jnp.zeros_like(dcb_ref)

        df, dgp = _rms_bwd(f_ref[...], gpost_ref[...], dxo_ref[...])
        dgp_ref[...] += dgp
        dfb = df.astype(BF16)
        da = _mm_nt(dfb, wdn_ref[...])
        up = up_ref[...]
        ug = up[:, :F]
        val = up[:, F:]
        halo = jnp.where(t == 0, 0.0, halo_ref[...])
        ext = jnp.concatenate([halo, ug], axis=0)
        w = cw_ref[...]
        gate = _conv_fwd(ext, w, FFN_KW, TL) + cb_ref[...]
        s, ds = _silu_parts(gate)
        acc_ref[...] += _mm_tn((s * val).astype(BF16), dfb)
        dval = da * s
        dgate = da * val * ds
        dcb_ref[...] += jnp.sum(dgate, axis=0, keepdims=True)
        dcw_ref[...] += _conv_bwd_w(ext, dgate, FFN_KW, TL)
        extd = jnp.concatenate([dgate, carry_ref[...]], axis=0)
        dug = _conv_bwd_in(extd, w, FFN_KW, TL)
        carry_ref[...] = dgate[:HALO, :]
        dup_ref[...] = jnp.concatenate([dug, dval], axis=1).astype(BF16)

        @pl.when(i == nt - 1)
        def _():
            _emit_row_shards(acc_ref, dwdn_ref, stage_ref)

    return pl.pallas_call(
        body, name=name, grid=(nt,),
        in_specs=[_rtile(D_MODEL, nt), _rtile(D_MODEL, nt), _rtile(2 * F, nt), _halo_before(F, nt, True),
                  _small((1, D_MODEL)), _res((F, D_MODEL)), _small((FFN_KW, F)), _small((1, F))],
        out_specs=[_rtile(2 * F, nt), _ANY, _small((1, D_MODEL)), _small((FFN_KW, F)), _small((1, F))],
        out_shape=[_sds((L, 2 * F), BF16), _sds((N_DEV, rows, D_MODEL), BF16), _sds((1, D_MODEL)),
                   _sds((FFN_KW, F)), _sds((1, F))],
        scratch_shapes=[pltpu.VMEM((F, D_MODEL), F32), pltpu.VMEM((HALO, F), F32), pltpu.VMEM((rows, D_MODEL), BF16)],
        compiler_params=_seq_params(),
    )(dxo, f, up, up, g_post, w_down, conv_w, conv_b)


def inproj_bwd(x, g_pre, d, w, dxo, cols, name):
    L = x.shape[0]
    nt = L // TL
    N = w.shape[1]

    def body(x_ref, g_ref, d_ref, w_ref, dxo_ref, dx_ref, dw_ref, dg_ref, acc_ref, stage_ref):
        i = pl.program_id(0)

        @pl.when(i == 0)
        def _():
            acc_ref[...] = jnp.zeros_like(acc_ref)
            dg_ref[...] = jnp.zeros_like(dg_ref)

        x = x_ref[...]
        g = g_ref[...]
        d = d_ref[...]
        h = _rms(x, g).astype(BF16)
        dh = _mm_nt(d, w_ref[...])
        acc_ref[...] += _mm_tn(h, d)
        dxn, dg = _rms_bwd(x, g, dh)
        dx_ref[...] = dxo_ref[...] + dxn
        dg_ref[...] += dg

        @pl.when(i == nt - 1)
        def _():
            _emit_col_shards(acc_ref, dw_ref, stage_ref)

    return pl.pallas_call(
        body, name=name, grid=(nt,),
        in_specs=[_tile(D_MODEL), _small((1, D_MODEL)), _tile(N), _res((D_MODEL, N)), _tile(D_MODEL)],
        out_specs=[_tile(D_MODEL), _ANY, _small((1, D_MODEL))],
        out_shape=[_sds((L, D_MODEL)), _sds((N_DEV, D_MODEL, cols), BF16), _sds((1, D_MODEL))],
        scratch_shapes=[pltpu.VMEM((D_MODEL, N), F32), pltpu.VMEM((D_MODEL, cols), BF16)],
        compiler_params=_seq_params(),
    )(x, g_pre, d, w, dxo)


def sc_fwd(x, g_pre, w_in, conv_w, w_out, g_post, name):
    L = x.shape[0]
    nt = L // TL
    W = D_MODEL

    def body(x_ref, gpre_ref, win_ref, cw_ref, wout_ref, gpost_ref, bcv_ref, m_ref, xn_ref, carry_ref):
        i = pl.program_id(0)

        @pl.when(i == 0)
        def _():
            carry_ref[...] = jnp.zeros_like(carry_ref)

        x = x_ref[...]
        h = _rms(x, gpre_ref[...]).astype(BF16)
        bcv = _mm(h, win_ref[...])
        bcv_ref[...] = bcv
        gb = bcv[:, :W]
        p = bcv[:, W:2 * W] * bcv[:, 2 * W:]
        ext = jnp.concatenate([carry_ref[...], p], axis=0)
        u = _conv_fwd(ext, cw_ref[...], SC_KW, TL)
        carry_ref[...] = p[TL - HALO:, :]
        m = _mm((gb * u).astype(BF16), wout_ref[...])
        m_ref[...] = m
        xn_ref[...] = x + _rms(m, gpost_ref[...])

    return pl.pallas_call(
        body, name=name, grid=(nt,),
        in_specs=[_tile(W), _small((1, W)), _res((W, 3 * W)), _small((SC_KW, W)), _res((W, W)), _small((1, W))],
        out_specs=[_tile(3 * W), _tile(W), _tile(W)],
        out_shape=[_sds((L, 3 * W)), _sds((L, W)), _sds((L, W))],
        scratch_shapes=[pltpu.VMEM((HALO, W), F32)],
        compiler_params=_seq_params(),
    )(x, g_pre, w_in, conv_w, w_out, g_post)


def sc_bwd1(dxo, m, bcv, g_post, w_out, conv_w, name):
    L = dxo.shape[0]
    nt = L // TL
    W = D_MODEL
    rows = W // N_DEV

    def body(dxo_ref, m_ref, bcv_ref, halo_ref, gpost_ref, wout_ref, cw_ref,
             dbcv_ref, dwout_ref, dgp_ref, dcw_ref, acc_ref, carry_ref, stage_ref):
        i = pl.program_id(0)
        t = nt - 1 - i

        @pl.when(i == 0)
        def _():
            acc_ref[...] = jnp.zeros_like(acc_ref)
            carry_ref[...] = jnp.zeros_like(carry_ref)
            dgp_ref[...] = jnp.zeros_like(dgp_ref)
            dcw_ref[...] = jnp.zeros_like(dcw_ref)

        dm, dgp = _rms_bwd(m_ref[...], gpost_ref[...], dxo_ref[...])
        dgp_ref[...] += dgp
        dmb = dm.astype(BF16)
        dq = _mm_nt(dmb, wout_ref[...])
        bcv = bcv_ref[...]
        gb = bcv[:, :W]
        gc = bcv[:, W:2 * W]
        v = bcv[:, 2 * W:]
        hb = halo_ref[...]
        halo = jnp.where(t == 0, 0.0, hb[:, W:2 * W] * hb[:, 2 * W:])
        ext = jnp.concatenate([halo, gc * v], axis=0)
        w = cw_ref[...]
        u = _conv_fwd(ext, w, SC_KW, TL)
        acc_ref[...] += _mm_tn((gb * u).astype(BF16), dmb)
        dgb = dq * u
        du = dq * gb
        dcw_ref[...] += _conv_bwd_w(ext, du, SC_KW, TL)
        extd = jnp.concatenate([du, carry_ref[...]], axis=0)
        dp = _conv_bwd_in(extd, w, SC_KW, TL)
        carry_ref[...] = du[:HALO, :]
        dbcv_ref[...] = jnp.concatenate([dgb, dp * v, dp * gc], axis=1).astype(BF16)

        @pl.when(i == nt - 1)
        def _():
            _emit_row_shards(acc_ref, dwout_ref, stage_ref)

    return pl.pallas_call(
        body, name=name, grid=(nt,),
        in_specs=[_rtile(W, nt), _rtile(W, nt), _rtile(3 * W, nt), _halo_before(3 * W, nt, True),
                  _small((1, W)), _res((W, W)), _small((SC_KW, W))],
        out_specs=[_rtile(3 * W, nt), _ANY, _small((1, W)), _small((SC_KW, W))],
        out_shape=[_sds((L, 3 * W), BF16), _sds((N_DEV, rows, W), BF16), _sds((1, W)), _sds((SC_KW, W))],
        scratch_shapes=[pltpu.VMEM((W, W), F32), pltpu.VMEM((HALO, W), F32), pltpu.VMEM((rows, W), BF16)],
        compiler_params=_seq_params(),
    )(dxo, m, bcv, bcv, g_post, w_out, conv_w)


def ssd_inproj(x, g_pre, w_in, conv_w, conv_b, name):
    L = x.shape[0]
    nt = L // TL

    def body(x_ref, gpre_ref, win_ref, cw_ref, cb_ref, z_ref, raw_ref, dt_ref, xh_ref, bm_ref, cm_ref, carry_ref):
        i = pl.program_id(0)

        @pl.when(i == 0)
        def _():
            carry_ref[...] = jnp.zeros_like(carry_ref)

        h = _rms(x_ref[...], gpre_ref[...]).astype(BF16)
        zx = _mm(h, win_ref[...])
        z_ref[...] = zx[:, :SSD_DI]
        raw = zx[:, SSD_DI:SSD_DI + SSD_CONV]
        raw_ref[...] = raw
        dt_ref[...] = zx[:, SSD_DI + SSD_CONV:SSD_IN]
        ext = jnp.concatenate([carry_ref[...], raw], axis=0)
        pre = _conv_fwd(ext, cw_ref[...], SSD_KW, TL) + cb_ref[...]
        carry_ref[...] = raw[TL - HALO:, :]
        act = pre * jax.nn.sigmoid(pre)
        for hh in range(SSD_H):
            xh_ref[hh] = act[:, hh * SSD_P:(hh + 1) * SSD_P]
        for g in range(SSD_G):
            bm_ref[g] = act[:, SSD_DI + g * SSD_N:SSD_DI + (g + 1) * SSD_N]
            cm_ref[g] = act[:, SSD_DI + (SSD_G + g) * SSD_N:SSD_DI + (SSD_G + g + 1) * SSD_N]

    return pl.pallas_call(
        body, name=name, grid=(nt,),
        in_specs=[_tile(D_MODEL), _small((1, D_MODEL)), _res((D_MODEL, SSD_IN_PAD)), _small((SSD_KW, SSD_CONV)),
                  _small((1, SSD_CONV))],
        out_specs=[_tile(SSD_DI), _tile(SSD_CONV), _tile(SSD_H),
                   pl.BlockSpec((SSD_H, TL, SSD_P), lambda i: (0, i, 0)),
                   pl.BlockSpec((SSD_G, TL, SSD_N), lambda i: (0, i, 0)),
                   pl.BlockSpec((SSD_G, TL, SSD_N), lambda i: (0, i, 0))],
        out_shape=[_sds((L, SSD_DI)), _sds((L, SSD_CONV)), _sds((L, SSD_H)), _sds((SSD_H, L, SSD_P)),
                   _sds((SSD_G, L, SSD_N)), _sds((SSD_G, L, SSD_N))],
        scratch_shapes=[pltpu.VMEM((HALO, SSD_CONV), F32)],
        compiler_params=_seq_params(),
    )(x, g_pre, w_in, conv_w, conv_b)


def _per_head(v, rows):
    return jnp.stack([v[:, h:h + 1] for h in range(SSD_H)], axis=0)


def _heads_to_lanes(v):
    return jnp.concatenate([v[h] for h in range(SSD_H)], axis=1)


def _rep_heads(v):
    g, a, b = v.shape
    return jnp.broadcast_to(v[:, None], (g, SSD_R, a, b)).reshape(g * SSD_R, a, b)


def _sum_heads(v):
    h, a, b = v.shape
    return v.reshape(SSD_G, SSD_R, a, b).sum(axis=1)


def _ssd_decays(dtr, bias, a_log):
    T = CHUNK
    dt = jax.nn.softplus(dtr + bias)
    a_head = -jnp.exp(a_log)
    a = dt * a_head
    ii = lax.broadcasted_iota(jnp.int32, (T, T), 0)
    jj = lax.broadcasted_iota(jnp.int32, (T, T), 1)
    tri = ii >= jj
    cs = jnp.dot(tri.astype(F32), a, precision=lax.Precision.HIGHEST, preferred_element_type=F32)
    cs_t = cs.T
    csc = _per_head(cs, T)
    csr = jnp.stack([cs_t[h:h + 1, :] for h in range(SSD_H)], axis=0)
    dtc = _per_head(dt, T)
    cl = _per_head(cs[T - 1:T, :], 1)
    lmat = jnp.exp(jnp.where(tri[None], csc - csr, -jnp.inf))
    return dict(dt=dt, a_head=a_head, tri=tri, csc=csc, dtc=dtc, lmat=lmat,
                ecs=jnp.exp(csc), dsc=jnp.exp(cl - csc), cdc=jnp.exp(cl))


def ssd_scan_fwd(xh, bm, cm, dt_raw, dt_bias, a_log, d_skip, name):
    L = xh.shape[1]
    nc = L // CHUNK
    T = CHUNK

    def body(xh_ref, bm_ref, cm_ref, dt_ref, bias_ref, alog_ref, dsk_ref, y_ref, sp_ref, st_ref):
        c = pl.program_id(0)

        @pl.when(c == 0)
        def _():
            st_ref[...] = jnp.zeros_like(st_ref)

        dec = _ssd_decays(dt_ref[...], bias_ref[...], alog_ref[...])
        x = xh_ref[...]
        bgb = bm_ref[...].astype(BF16)
        cgb = cm_ref[...].astype(BF16)
        bh = _rep_heads(bgb)
        ch = _rep_heads(cgb)
        dh = _per_head(dsk_ref[...], 1)
        xt = x * dec["dtc"]
        cb = jnp.einsum("gln,gsn->gls", cgb, bgb, preferred_element_type=F32)
        mb = (_rep_heads(cb) * dec["lmat"]).astype(BF16)
        yd = jnp.einsum("hls,hsp->hlp", mb, xt.astype(BF16), preferred_element_type=F32)
        s = st_ref[...]
        sb = s.astype(BF16)
        yo = jnp.einsum("hln,hpn->hlp", ch, sb, preferred_element_type=F32) * dec["ecs"]
        y_ref[...] = yd + yo + x * dh
        sp_ref[0] = sb
        xd = (xt * dec["dsc"]).astype(BF16)
        st_ref[...] = s * dec["cdc"] + jnp.einsum("htp,htn->hpn", xd, bh, preferred_element_type=F32)

    hd = pl.BlockSpec((SSD_H, T, SSD_P), lambda c: (0, c, 0))
    gr = pl.BlockSpec((SSD_G, T, SSD_N), lambda c: (0, c, 0))
    return pl.pallas_call(
        body, name=name, grid=(nc,),
        in_specs=[hd, gr, gr, pl.BlockSpec((T, SSD_H), lambda c: (c, 0)),
                  _small((1, SSD_H)), _small((1, SSD_H)), _small((1, SSD_H))],
        out_specs=[hd, pl.BlockSpec((1, SSD_H, SSD_P, SSD_N), lambda c: (c, 0, 0, 0))],
        out_shape=[_sds((SSD_H, L, SSD_P)), _sds((nc, SSD_H, SSD_P, SSD_N), BF16)],
        scratch_shapes=[pltpu.VMEM((SSD_H, SSD_P, SSD_N), F32)],
        compiler_params=_seq_params(),
    )(xh, bm, cm, dt_raw, dt_bias, a_log, d_skip)


def ssd_scan_bwd(dy, xh, bm, cm, dt_raw, sprev, dt_bias, a_log, d_skip, name):
    L = xh.shape[1]
    nc = L // CHUNK
    T = CHUNK

    def body(dy_ref, xh_ref, bm_ref, cm_ref, dt_ref, sp_ref, bias_ref, alog_ref, dsk_ref,
             dxh_ref, dbm_ref, dcm_ref, ddt_ref, dbias_ref, dalog_ref, ddsk_ref, g_ref):
        i = pl.program_id(0)

        @pl.when(i == 0)
        def _():
            g_ref[...] = jnp.zeros_like(g_ref)
            dbias_ref[...] = jnp.zeros_like(dbias_ref)
            dalog_ref[...] = jnp.zeros_like(dalog_ref)
            ddsk_ref[...] = jnp.zeros_like(ddsk_ref)

        dtr = dt_ref[...]
        bias = bias_ref[...]
        dec = _ssd_decays(dtr, bias, alog_ref[...])
        dt, a_head, tri = dec["dt"], dec["a_head"], dec["tri"]
        dtc, lmat, ecs, dsc, cdc = dec["dtc"], dec["lmat"], dec["ecs"], dec["dsc"], dec["cdc"]
        x = xh_ref[...]
        dyv = dy_ref[...]
        dyb = dyv.astype(BF16)
        bgb = bm_ref[...].astype(BF16)
        cgb = cm_ref[...].astype(BF16)
        bh = _rep_heads(bgb)
        ch = _rep_heads(cgb)
        sb = sp_ref[0]
        g = g_ref[...]
        gb = g.astype(BF16)
        dh = _per_head(dsk_ref[...], 1)
        xt = x * dtc
        xtb = xt.astype(BF16)
        cb = jnp.einsum("gln,gsn->gls", cgb, bgb, preferred_element_type=F32)
        mf = _rep_heads(cb) * lmat
        mb = mf.astype(BF16)
        ddsk = jnp.sum(dyv * x, axis=(1, 2), keepdims=True)
        dx = dyv * dh
        yo_raw = jnp.einsum("hln,hpn->hlp", ch, sb, preferred_element_type=F32)
        w1 = dyv * ecs
        w1b = w1.astype(BF16)
        ds_off = jnp.einsum("hlp,hln->hpn", w1b, ch, preferred_element_type=F32)
        dch = jnp.einsum("hlp,hpn->hln", w1b, sb, preferred_element_type=F32)
        dcs_c = jnp.sum(w1 * yo_raw, axis=2, keepdims=True)
        dm = jnp.einsum("hlp,hsp->hls", dyb, xtb, preferred_element_type=F32)
        dxt = jnp.einsum("hls,hlp->hsp", mb, dyb, preferred_element_type=F32)
        dcbb = _sum_heads(dm * lmat).astype(BF16)
        dseg = dm * mf
        dcs_c = dcs_c + jnp.sum(dseg, axis=2, keepdims=True)
        dcs_r = -jnp.sum(dseg, axis=1, keepdims=True)
        dc = jnp.einsum("gls,gsn->gln", dcbb, bgb, preferred_element_type=F32) + _sum_heads(dch)
        db = jnp.einsum("gls,gln->gsn", dcbb, cgb, preferred_element_type=F32)
        xd = xt * dsc
        dxd = jnp.einsum("htn,hpn->htp", bh, gb, preferred_element_type=F32)
        db = db + _sum_heads(jnp.einsum("htp,hpn->htn", xd.astype(BF16), gb, preferred_element_type=F32))
        dxt = dxt + dxd * dsc
        d_ds = jnp.sum(dxd * xt, axis=2, keepdims=True)
        d_cd = jnp.sum(g * sb.astype(F32), axis=(1, 2), keepdims=True)
        g_ref[...] = g * cdc + ds_off
        t1 = d_ds * dsc
        dcs_c = dcs_c - t1
        dcl = jnp.sum(t1, axis=1, keepdims=True) + d_cd * cdc
        ddt_c = jnp.sum(dxt * x, axis=2, keepdims=True)
        dxh_ref[...] = dx + dxt * dtc
        dbm_ref[...] = db
        dcm_ref[...] = dc
        rows_t = jnp.concatenate([dcs_r[h] for h in range(SSD_H)], axis=0).T
        last = (lax.broadcasted_iota(jnp.int32, (T, 1), 0) == T - 1).astype(F32)
        dcs = _heads_to_lanes(dcs_c) + rows_t + last * _heads_to_lanes(dcl)
        da = lax.dot_general(tri.astype(F32), dcs, (((0,), (0,)), ((), ())),
                             precision=lax.Precision.HIGHEST, preferred_element_type=F32)
        ddt = da * a_head + _heads_to_lanes(ddt_c)
        dalog_ref[...] += jnp.sum(da * dt, axis=0, keepdims=True)
        ddtr = ddt * jax.nn.sigmoid(dtr + bias)
        ddt_ref[...] = ddtr
        dbias_ref[...] += jnp.sum(ddtr, axis=0, keepdims=True)
        ddsk_ref[...] += _heads_to_lanes(ddsk)

        @pl.when(i == nc - 1)
        def _():
            dalog_ref[...] = dalog_ref[...] * a_head

    hd = pl.BlockSpec((SSD_H, T, SSD_P), lambda i: (0, nc - 1 - i, 0))
    gr = pl.BlockSpec((SSD_G, T, SSD_N), lambda i: (0, nc - 1 - i, 0))
    tk = pl.BlockSpec((T, SSD_H), lambda i: (nc - 1 - i, 0))
    return pl.pallas_call(
        body, name=name, grid=(nc,),
        in_specs=[hd, hd, gr, gr, tk, pl.BlockSpec((1, SSD_H, SSD_P, SSD_N), lambda i: (nc - 1 - i, 0, 0, 0)),
                  _small((1, SSD_H)), _small((1, SSD_H)), _small((1, SSD_H))],
        out_specs=[hd, gr, gr, tk, _small((1, SSD_H)), _small((1, SSD_H)), _small((1, SSD_H))],
        out_shape=[_sds((SSD_H, L, SSD_P)), _sds((SSD_G, L, SSD_N)), _sds((SSD_G, L, SSD_N)), _sds((L, SSD_H)),
                   _sds((1, SSD_H)), _sds((1, SSD_H)), _sds((1, SSD_H))],
        scratch_shapes=[pltpu.VMEM((SSD_H, SSD_P, SSD_N), F32)],
        compiler_params=_seq_params(),
    )(dy, xh, bm, cm, dt_raw, sprev, dt_bias, a_log, d_skip)


def _heads_to_tokens(y_ref):
    return jnp.concatenate([y_ref[h] for h in range(SSD_H)], axis=1)


def ssd_out_fwd(x, y, z, norm_w, w_out, g_post, name):
    L = x.shape[0]
    nt = L // TL

    def body(x_ref, y_ref, z_ref, nw_ref, wout_ref, gpost_ref, m_ref, xn_ref):
        z = z_ref[...]
        yg = _heads_to_tokens(y_ref) * (z * jax.nn.sigmoid(z))
        yn = _rms(yg, nw_ref[...]).astype(BF16)
        m = _mm(yn, wout_ref[...])
        m_ref[...] = m
        xn_ref[...] = x_ref[...] + _rms(m, gpost_ref[...])

    return pl.pallas_call(
        body, name=name, grid=(nt,),
        in_specs=[_tile(D_MODEL), pl.BlockSpec((SSD_H, TL, SSD_P), lambda i: (0, i, 0)), _tile(SSD_DI),
                  _small((1, SSD_DI)), _res((SSD_DI, D_MODEL)), _small((1, D_MODEL))],
        out_specs=[_tile(D_MODEL), _tile(D_MODEL)],
        out_shape=[_sds((L, D_MODEL)), _sds((L, D_MODEL))],
        compiler_params=_seq_params(),
    )(x, y, z, norm_w, w_out, g_post)


def ssd_out_bwd(dxo, m, y, z, norm_w, w_out, g_post, name):
    L = dxo.shape[0]
    nt = L // TL
    rows = SSD_DI // N_DEV

    def body(dxo_ref, m_ref, y_ref, z_ref, nw_ref, wout_ref, gpost_ref,
             dy_ref, dz_ref, dwout_ref, dgp_ref, dnw_ref, acc_ref, stage_ref):
        i = pl.program_id(0)

        @pl.when(i == 0)
        def _():
            acc_ref[...] = jnp.zeros_like(acc_ref)
            dgp_ref[...] = jnp.zeros_like(dgp_ref)
            dnw_ref[...] = jnp.zeros_like(dnw_ref)

        dm, dgp = _rms_bwd(m_ref[...], gpost_ref[...], dxo_ref[...])
        dgp_ref[...] += dgp
        dmb = dm.astype(BF16)
        dyn = _mm_nt(dmb, wout_ref[...])
        z = z_ref[...]
        y = _heads_to_tokens(y_ref)
        sil, dsil = _silu_parts(z)
        yg = y * sil
        nw = nw_ref[...]
        acc_ref[...] += _mm_tn(_rms(yg, nw).astype(BF16), dmb)
        dyg, dnw = _rms_bwd(yg, nw, dyn)
        dnw_ref[...] += dnw
        dyv = dyg * sil
        dz_ref[...] = dyg * y * dsil
        for h in range(SSD_H):
            dy_ref[h] = dyv[:, h * SSD_P:(h + 1) * SSD_P]

        @pl.when(i == nt - 1)
        def _():
            _emit_row_shards(acc_ref, dwout_ref, stage_ref)

    hd = pl.BlockSpec((SSD_H, TL, SSD_P), lambda i: (0, i, 0))
    return pl.pallas_call(
        body, name=name, grid=(nt,),
        in_specs=[_tile(D_MODEL), _tile(D_MODEL), hd, _tile(SSD_DI), _small((1, SSD_DI)), _res((SSD_DI, D_MODEL)),
                  _small((1, D_MODEL))],
        out_specs=[hd, _tile(SSD_DI), _ANY, _small((1, D_MODEL)), _small((1, SSD_DI))],
        out_shape=[_sds((SSD_H, L, SSD_P)), _sds((L, SSD_DI)), _sds((N_DEV, rows, D_MODEL), BF16),
                   _sds((1, D_MODEL)), _sds((1, SSD_DI))],
        scratch_shapes=[pltpu.VMEM((SSD_DI, D_MODEL), F32), pltpu.VMEM((rows, D_MODEL), BF16)],
        compiler_params=_seq_params(),
    )(dxo, m, y, z, norm_w, w_out, g_post)


def ssd_conv_bwd(dxh, dbm, dcm, xbc_raw, dz, ddt_raw, conv_w, conv_b, name):
    L = xbc_raw.shape[0]
    nt = L // TL

    def body(dxh_ref, dbm_ref, dcm_ref, raw_ref, halo_ref, dz_ref, ddt_ref, cw_ref, cb_ref,
             d_ref, dcw_ref, dcb_ref, carry_ref):
        i = pl.program_id(0)
        t = nt - 1 - i

        @pl.when(i == 0)
        def _():
            carry_ref[...] = jnp.zeros_like(carry_ref)
            dcw_ref[...] = jnp.zeros_like(dcw_ref)
            dcb_ref[...] = jnp.zeros_like(dcb_ref)

        dact = jnp.concatenate([dxh_ref[h] for h in range(SSD_H)] + [dbm_ref[g] for g in range(SSD_G)]
                               + [dcm_ref[g] for g in range(SSD_G)], axis=1)
        halo = jnp.where(t == 0, 0.0, halo_ref[...])
        ext = jnp.concatenate([halo, raw_ref[...]], axis=0)
        w = cw_ref[...]
        pre = _conv_fwd(ext, w, SSD_KW, TL) + cb_ref[...]
        _, dsil = _silu_parts(pre)
        dpre = dact * dsil
        dcb_ref[...] += jnp.sum(dpre, axis=0, keepdims=True)
        dcw_ref[...] += _conv_bwd_w(ext, dpre, SSD_KW, TL)
        extd = jnp.concatenate([dpre, carry_ref[...]], axis=0)
        draw = _conv_bwd_in(extd, w, SSD_KW, TL)
        carry_ref[...] = dpre[:HALO, :]
        d_ref[:, :SSD_DI] = dz_ref[...].astype(BF16)
        d_ref[:, SSD_DI:SSD_DI + SSD_CONV] = draw.astype(BF16)
        tail = jnp.concatenate([ddt_ref[...], jnp.zeros((TL, SSD_IN_PAD - SSD_IN), F32)], axis=1)
        d_ref[:, SSD_DI + SSD_CONV:] = tail.astype(BF16)

    hd = pl.BlockSpec((SSD_H, TL, SSD_P), lambda i: (0, nt - 1 - i, 0))
    gr = pl.BlockSpec((SSD_G, TL, SSD_N), lambda i: (0, nt - 1 - i, 0))
    return pl.pallas_call(
        body, name=name, grid=(nt,),
        in_specs=[hd, gr, gr, _rtile(SSD_CONV, nt), _halo_before(SSD_CONV, nt, True), _rtile(SSD_DI, nt),
                  _rtile(SSD_H, nt), _small((SSD_KW, SSD_CONV)), _small((1, SSD_CONV))],
        out_specs=[_rtile(SSD_IN_PAD, nt), _small((SSD_KW, SSD_CONV)), _small((1, SSD_CONV))],
        out_shape=[_sds((L, SSD_IN_PAD), BF16), _sds((SSD_KW, SSD_CONV)), _sds((1, SSD_CONV))],
        scratch_shapes=[pltpu.VMEM((HALO, SSD_CONV), F32)],
        compiler_params=_seq_params(),
    )(dxh, dbm, dcm, xbc_raw, xbc_raw, dz, ddt_raw, conv_w, conv_b)


def loss_fwd_bwd(y, target, name):
    L = y.shape[0]
    nt = L // TL

    def body(y_ref, t_ref, loss_ref, dy_ref):
        i = pl.program_id(0)

        @pl.when(i == 0)
        def _():
            loss_ref[...] = jnp.zeros_like(loss_ref)

        err = y_ref[...] - t_ref[...]
        dy_ref[...] = err * (1.0 / D_MODEL)
        loss_ref[...] += 0.5 * jnp.sum(jnp.mean(err * err, axis=-1, keepdims=True), axis=0, keepdims=True)

    return pl.pallas_call(
        body, name=name, grid=(nt,),
        in_specs=[_tile(D_MODEL), _tile(D_MODEL)],
        out_specs=[_small((1, 1)), _tile(D_MODEL)],
        out_shape=[_sds((1, 1)), _sds((L, D_MODEL))],
        compiler_params=_seq_params(),
    )(y, target)


def _peer(k):
    x, y, c = lax.axis_index("x"), lax.axis_index("y"), lax.axis_index("c")
    px = x ^ (k >> 2)
    py = y ^ ((k >> 1) & 1)
    pc = c ^ (k & 1)
    return (px, py, pc), 4 * px + 2 * py + pc


def _my_index():
    return 4 * lax.axis_index("x") + 2 * lax.axis_index("y") + lax.axis_index("c")


def all_gather_layers(shard, name):
    nl, R, C = shard.shape

    def body(src_ref, out_ref, send_sems, recv_sems, local_sem):
        me = _my_index()
        local = pltpu.make_async_copy(src_ref, out_ref.at[:, me], local_sem)
        local.start()
        sends = []
        for k in range(1, N_DEV):
            dev, _ = _peer(k)
            cp = pltpu.make_async_remote_copy(src_ref=src_ref, dst_ref=out_ref.at[:, me],
                                              send_sem=send_sems.at[k - 1], recv_sem=recv_sems.at[k - 1],
                                              device_id=dev, device_id_type=MESH)
            cp.start()
            sends.append(cp)
        for k in range(1, N_DEV):
            dev, idx = _peer(k)
            pltpu.make_async_remote_copy(src_ref=src_ref, dst_ref=out_ref.at[:, idx],
                                         send_sem=send_sems.at[k - 1], recv_sem=recv_sems.at[k - 1],
                                         device_id=dev, device_id_type=MESH).wait_recv()
        for cp in sends:
            cp.wait_send()
        local.wait()

    return pl.pallas_call(
        body, name=name,
        in_specs=[_ANY], out_specs=_ANY,
        out_shape=_sds((nl, N_DEV, R, C), shard.dtype),
        scratch_shapes=[pltpu.SemaphoreType.DMA((N_DEV - 1,)), pltpu.SemaphoreType.DMA((N_DEV - 1,)),
                        pltpu.SemaphoreType.DMA],
        compiler_params=pltpu.CompilerParams(has_side_effects=True),
    )(shard)


def all_to_all_layers(parts, name):
    nl = len(parts)
    _, R, C = parts[0].shape

    def body(*refs):
        src_refs = refs[:nl]
        out_ref, send_sems, recv_sems, local_sems = refs[nl:]
        me = _my_index()
        locals_ = []
        for l in range(nl):
            cp = pltpu.make_async_copy(src_refs[l].at[me], out_ref.at[l, me], local_sems.at[l])
            cp.start()
            locals_.append(cp)
        sends = []
        for k in range(1, N_DEV):
            dev, idx = _peer(k)
            for l in range(nl):
                cp = pltpu.make_async_remote_copy(src_ref=src_refs[l].at[idx], dst_ref=out_ref.at[l, me],
                                                  send_sem=send_sems.at[k - 1, l], recv_sem=recv_sems.at[k - 1, l],
                                                  device_id=dev, device_id_type=MESH)
                cp.start()
                sends.append(cp)
        for k in range(1, N_DEV):
            dev, idx = _peer(k)
            for l in range(nl):
                pltpu.make_async_remote_copy(src_ref=src_refs[l].at[idx], dst_ref=out_ref.at[l, idx],
                                             send_sem=send_sems.at[k - 1, l], recv_sem=recv_sems.at[k - 1, l],
                                             device_id=dev, device_id_type=MESH).wait_recv()
        for cp in sends:
            cp.wait_send()
        for cp in locals_:
            cp.wait()

    return pl.pallas_call(
        body, name=name,
        in_specs=[_ANY] * nl, out_specs=_ANY,
        out_shape=_sds((nl, N_DEV, R, C), parts[0].dtype),
        scratch_shapes=[pltpu.SemaphoreType.DMA((N_DEV - 1, nl)), pltpu.SemaphoreType.DMA((N_DEV - 1, nl)),
                        pltpu.SemaphoreType.DMA((nl,))],
        compiler_params=pltpu.CompilerParams(has_side_effects=True),
    )(*parts)


def _adamw_math(w, g, m, v):
    m = ADAM_B1 * m + (1.0 - ADAM_B1) * g
    v = ADAM_B2 * v + (1.0 - ADAM_B2) * (g * g)
    m_hat = m / (1.0 - ADAM_B1 ** ADAM_STEP)
    v_hat = v / (1.0 - ADAM_B2 ** ADAM_STEP)
    delta = -ADAM_LR * (m_hat / (jnp.sqrt(v_hat) + ADAM_EPS) + ADAM_WD * w)
    return delta, m, v


def _row_tile(rows):
    for cand in (256, 176, 128, 64, 32, 16, 8):
        if rows % cand == 0:
            return cand
    return rows


def reduce_adamw(recv, w, m, v, name):
    nl, _, R, C = recv.shape
    tr = _row_tile(R)

    def body(r_ref, w_ref, m_ref, v_ref, g_out, d_out, m_out, v_out):
        g = r_ref[0, 0].astype(F32)
        for j in range(1, N_DEV):
            g = g + r_ref[0, j].astype(F32)
        delta, mn, vn = _adamw_math(w_ref[0], g, m_ref[0], v_ref[0])
        g_out[0] = g
        d_out[0] = delta
        m_out[0] = mn
        v_out[0] = vn

    blk = pl.BlockSpec((1, tr, C), lambda l, r: (l, r, 0))
    return pl.pallas_call(
        body, name=name, grid=(nl, R // tr),
        in_specs=[pl.BlockSpec((1, N_DEV, tr, C), lambda l, r: (l, 0, r, 0)), blk, blk, blk],
        out_specs=[blk] * 4,
        out_shape=[_sds((nl, R, C))] * 4,
        compiler_params=pltpu.CompilerParams(dimension_semantics=("arbitrary", "arbitrary"),
                                             vmem_limit_bytes=VMEM_LIMIT),
    )(recv, w, m, v)


def small_reduce(gathered, name):
    _, _, R, C = gathered.shape

    def body(r_ref, o_ref):
        g = r_ref[0, 0]
        for j in range(1, N_DEV):
            g = g + r_ref[0, j]
        o_ref[...] = g

    return pl.pallas_call(body, name=name, out_shape=_sds((R, C)))(gathered)


def small_adamw(g, w, m, v, name):
    def body(g_ref, w_ref, m_ref, v_ref, d_out, m_out, v_out):
        delta, mn, vn = _adamw_math(w_ref[...], g_ref[...], m_ref[...], v_ref[...])
        d_out[...] = delta
        m_out[...] = mn
        v_out[...] = vn

    return pl.pallas_call(body, name=name, out_shape=[_sds(g.shape)] * 3)(g, w, m, v)


def _pack(arrs):
    flat = jnp.concatenate([a.reshape(-1) for a in arrs])
    n = flat.shape[0]
    rows = -(-n // (8 * LANES)) * 8
    flat = jnp.pad(flat, (0, rows * LANES - n))
    return flat.reshape(rows, LANES)


def _unpack(packed, shapes):
    flat = packed.reshape(-1)
    out = []
    off = 0
    for s in shapes:
        n = 1
        for d in s:
            n *= d
        out.append(flat[off:off + n].reshape(s))
        off += n
    return out


def kernel(x, mix_pre_g, mix_post_g, ffn_pre_g, ffn_post_g, ssd_w_in, ssd_conv_w, ssd_conv_b, ssd_dt_bias, ssd_A_log, ssd_D, ssd_norm_w, ssd_w_out, sc_w_in, sc_conv_w, sc_w_out, ffn_w_up, ffn_conv_w, ffn_conv_b, ffn_w_down, loss_target, m_mix_pre_g, m_mix_post_g, m_ffn_pre_g, m_ffn_post_g, m_ssd_w_in, m_ssd_conv_w, m_ssd_conv_b, m_ssd_dt_bias, m_ssd_A_log, m_ssd_D, m_ssd_norm_w, m_ssd_w_out, m_sc_w_in, m_sc_conv_w, m_sc_w_out, m_ffn_w_up, m_ffn_conv_w, m_ffn_conv_b, m_ffn_w_down, v_mix_pre_g, v_mix_post_g, v_ffn_pre_g, v_ffn_post_g, v_ssd_w_in, v_ssd_conv_w, v_ssd_conv_b, v_ssd_dt_bias, v_ssd_A_log, v_ssd_D, v_ssd_norm_w, v_ssd_w_out, v_sc_w_in, v_sc_conv_w, v_sc_w_out, v_ffn_w_up, v_ffn_conv_w, v_ffn_conv_b, v_ffn_w_down):
    me = _my_index()
    x0 = x[0]
    target = loss_target[0]

    def gather_cols(w_shard, name, pad_to=None):
        g = all_gather_layers(w_shard.astype(BF16), name)
        nl, _, R, C = g.shape
        full = jnp.transpose(g, (0, 2, 1, 3)).reshape(nl, R, N_DEV * C)
        if pad_to is not None:
            full = jnp.pad(full, ((0, 0), (0, 0), (0, pad_to - N_DEV * C)))
        return full

    def gather_rows(w_shard, name):
        g = all_gather_layers(w_shard.astype(BF16), name)
        nl, _, R, C = g.shape
        return g.reshape(nl, N_DEV * R, C)

    W_ssd_in = gather_cols(ssd_w_in, "ag_ssd_w_in", SSD_IN_PAD)
    W_ssd_out = gather_rows(ssd_w_out, "ag_ssd_w_out")
    W_sc_in = gather_cols(sc_w_in, "ag_sc_w_in")
    W_sc_out = gather_rows(sc_w_out, "ag_sc_w_out")
    W_up = gather_cols(ffn_w_up, "ag_ffn_w_up")
    W_down = gather_rows(ffn_w_down, "ag_ffn_w_down")

    def gather_small_cols(w_shard, name):
        nl, K, C = w_shard.shape
        g = all_gather_layers(w_shard, name)
        return jnp.transpose(g, (0, 2, 1, 3)).reshape(nl, K, N_DEV * C)

    CW_ssd = gather_small_cols(ssd_conv_w, "ag_ssd_conv_w")
    CW_sc = gather_small_cols(sc_conv_w, "ag_sc_conv_w")
    CW_ffn = gather_small_cols(ffn_conv_w, "ag_ffn_conv_w")

    row = lambda a: a.reshape(1, -1)

    saved = []
    h = x0
    for i in range(DEPTH):
        j = i // 2
        blk = dict(x_mix=h)
        if i % 2 == 0:
            z, raw, dt_raw, xh, bm, cm = ssd_inproj(h, row(mix_pre_g[i]), W_ssd_in[j], CW_ssd[j], row(ssd_conv_b[j]),
                                                    f"ssd_inproj_{j}")
            y, sprev = ssd_scan_fwd(xh, bm, cm, dt_raw, row(ssd_dt_bias[j]), row(ssd_A_log[j]), row(ssd_D[j]),
                                    f"ssd_scan_fwd_{j}")
            m, h = ssd_out_fwd(h, y, z, row(ssd_norm_w[j]), W_ssd_out[j], row(mix_post_g[i]), f"ssd_out_fwd_{j}")
            blk.update(z=z, raw=raw, dt_raw=dt_raw, xh=xh, bm=bm, cm=cm, y=y, sprev=sprev, m=m)
        else:
            bcv, m, h = sc_fwd(h, row(mix_pre_g[i]), W_sc_in[j], CW_sc[j], W_sc_out[j], row(mix_post_g[i]),
                               f"sc_fwd_{j}")
            blk.update(bcv=bcv, m=m)
        blk["x_ffn"] = h
        up, f, h = ffn_fwd(h, row(ffn_pre_g[i]), W_up[i], CW_ffn[i], row(ffn_conv_b[i]), W_down[i],
                           row(ffn_post_g[i]), f"ffn_fwd_{i}")
        blk.update(up=up, f=f)
        saved.append(blk)

    loss_dev, dh = loss_fwd_bwd(h, target, "loss")
    loss = lax.psum(loss_dev[0, 0], ("x", "y", "c"))

    g_mix_pre, g_mix_post, g_ffn_pre, g_ffn_post = [None] * DEPTH, [None] * DEPTH, [None] * DEPTH, [None] * DEPTH
    g_ffn_cw, g_ffn_cb = [None] * DEPTH, [None] * DEPTH
    p_up, p_down = [None] * DEPTH, [None] * DEPTH
    g_ssd_cw, g_ssd_cb, g_ssd_dtb, g_ssd_alog, g_ssd_d, g_ssd_nw = ([None] * 2 for _ in range(6))
    p_ssd_in, p_ssd_out, p_sc_in, p_sc_out = [None] * 2, [None] * 2, [None] * 2, [None] * 2
    g_sc_cw = [None] * 2
    for i in reversed(range(DEPTH)):
        j = i // 2
        blk = saved[i]
        dup, p_down[i], g_ffn_post[i], g_ffn_cw[i], g_ffn_cb[i] = ffn_bwd1(
            dh, blk["f"], blk["up"], row(ffn_post_g[i]), W_down[i], CW_ffn[i], row(ffn_conv_b[i]), f"ffn_bwd1_{i}")
        dh, p_up[i], g_ffn_pre[i] = inproj_bwd(blk["x_ffn"], row(ffn_pre_g[i]), dup, W_up[i], dh,
                                                 2 * FFN_F // N_DEV, f"ffn_bwd2_{i}")
        if i % 2 == 0:
            dy, dz, p_ssd_out[j], g_mix_post[i], g_ssd_nw[j] = ssd_out_bwd(
                dh, blk["m"], blk["y"], blk["z"], row(ssd_norm_w[j]), W_ssd_out[j], row(mix_post_g[i]),
                f"ssd_out_bwd_{j}")
            dxh, dbm, dcm, ddt, g_ssd_dtb[j], g_ssd_alog[j], g_ssd_d[j] = ssd_scan_bwd(
                dy, blk["xh"], blk["bm"], blk["cm"], blk["dt_raw"], blk["sprev"], row(ssd_dt_bias[j]),
                row(ssd_A_log[j]), row(ssd_D[j]), f"ssd_scan_bwd_{j}")
            d_in, g_ssd_cw[j], g_ssd_cb[j] = ssd_conv_bwd(dxh, dbm, dcm, blk["raw"], dz, ddt, CW_ssd[j],
                                                         row(ssd_conv_b[j]), f"ssd_conv_bwd_{j}")
            dh, p_ssd_in[j], g_mix_pre[i] = inproj_bwd(blk["x_mix"], row(mix_pre_g[i]), d_in, W_ssd_in[j], dh,
                                                         SSD_IN // N_DEV, f"ssd_bwd2_{j}")
        else:
            dbcv, p_sc_out[j], g_mix_post[i], g_sc_cw[j] = sc_bwd1(dh, blk["m"], blk["bcv"], row(mix_post_g[i]),
                                                                   W_sc_out[j], CW_sc[j], f"sc_bwd1_{j}")
            dh, p_sc_in[j], g_mix_pre[i] = inproj_bwd(blk["x_mix"], row(mix_pre_g[i]), dbcv, W_sc_in[j], dh,
                                                        3 * D_MODEL // N_DEV, f"sc_bwd2_{j}")
    grad_x = dh[None]

    def finish(parts, w, m, v, name):
        recv = all_to_all_layers(parts, "a2a_" + name)
        return reduce_adamw(recv, w, m, v, "adamw_" + name)

    r_ssd_in = finish(p_ssd_in, ssd_w_in, m_ssd_w_in, v_ssd_w_in, "ssd_w_in")
    r_ssd_out = finish(p_ssd_out, ssd_w_out, m_ssd_w_out, v_ssd_w_out, "ssd_w_out")
    r_sc_in = finish(p_sc_in, sc_w_in, m_sc_w_in, v_sc_w_in, "sc_w_in")
    r_sc_out = finish(p_sc_out, sc_w_out, m_sc_w_out, v_sc_w_out, "sc_w_out")
    r_up = finish(p_up, ffn_w_up, m_ffn_w_up, v_ffn_w_up, "ffn_w_up")
    r_down = finish(p_down, ffn_w_down, m_ffn_w_down, v_ffn_w_down, "ffn_w_down")

    st = lambda lst: jnp.concatenate(lst, axis=0)
    small_full = [
        st(g_mix_pre), st(g_mix_post), st(g_ffn_pre), st(g_ffn_post),
        jnp.stack(g_ssd_cw), st(g_ssd_cb), st(g_ssd_dtb), st(g_ssd_alog), st(g_ssd_d), st(g_ssd_nw),
        jnp.stack(g_sc_cw), jnp.stack(g_ffn_cw), st(g_ffn_cb),
    ]
    full_shapes = [a.shape for a in small_full]
    packed = _pack(small_full)
    gathered = all_gather_layers(packed[None], "ag_small_grads")
    summed = small_reduce(gathered, "small_reduce")
    (s_mix_pre, s_mix_post, s_ffn_pre, s_ffn_post, s_ssd_cw, s_ssd_cb, s_ssd_dtb, s_ssd_alog, s_ssd_d, s_ssd_nw,
     s_sc_cw, s_ffn_cw, s_ffn_cb) = _unpack(summed, full_shapes)

    def my_cols(a, width):
        return lax.dynamic_slice_in_dim(a, me * width, width, axis=a.ndim - 1)

    s_ssd_cw = my_cols(s_ssd_cw, SSD_CONV // N_DEV)
    s_sc_cw = my_cols(s_sc_cw, D_MODEL // N_DEV)
    s_ffn_cw = my_cols(s_ffn_cw, FFN_F // N_DEV)

    small_g = [s_mix_pre, s_mix_post, s_ffn_pre, s_ffn_post, s_ssd_cw, s_ssd_cb, s_ssd_dtb, s_ssd_alog, s_ssd_d,
               s_ssd_nw, s_sc_cw, s_ffn_cw, s_ffn_cb]
    small_w = [mix_pre_g, mix_post_g, ffn_pre_g, ffn_post_g, ssd_conv_w, ssd_conv_b, ssd_dt_bias, ssd_A_log, ssd_D,
               ssd_norm_w, sc_conv_w, ffn_conv_w, ffn_conv_b]
    small_m = [m_mix_pre_g, m_mix_post_g, m_ffn_pre_g, m_ffn_post_g, m_ssd_conv_w, m_ssd_conv_b, m_ssd_dt_bias,
               m_ssd_A_log, m_ssd_D, m_ssd_norm_w, m_sc_conv_w, m_ffn_conv_w, m_ffn_conv_b]
    small_v = [v_mix_pre_g, v_mix_post_g, v_ffn_pre_g, v_ffn_post_g, v_ssd_conv_w, v_ssd_conv_b, v_ssd_dt_bias,
               v_ssd_A_log, v_ssd_D, v_ssd_norm_w, v_sc_conv_w, v_ffn_conv_w, v_ffn_conv_b]
    local_shapes = [a.shape for a in small_w]
    pd, pm, pv = small_adamw(_pack(small_g), _pack(small_w), _pack(small_m), _pack(small_v), "small_adamw")
    sd = _unpack(pd, local_shapes)
    sm = _unpack(pm, local_shapes)
    sv = _unpack(pv, local_shapes)

    def ordered(small, big):
        (mix_pre, mix_post, ffn_pre, ffn_post, ssd_cw, ssd_cb, dtb, alog, dsk, nw, sc_cw, ffn_cw, ffn_cb) = small
        (b_ssd_in, b_ssd_out, b_sc_in, b_sc_out, b_up, b_down) = big
        return [mix_pre, mix_post, ffn_pre, ffn_post, b_ssd_in, ssd_cw, ssd_cb, dtb, alog, dsk, nw, b_ssd_out,
                b_sc_in, sc_cw, b_sc_out, b_up, ffn_cw, ffn_cb, b_down]

    bigs = [r_ssd_in, r_ssd_out, r_sc_in, r_sc_out, r_up, r_down]
    grads = ordered(small_g, [r[0] for r in bigs])
    deltas = ordered(sd, [r[1] for r in bigs])
    new_m = ordered(sm, [r[2] for r in bigs])
    new_v = ordered(sv, [r[3] for r in bigs])
    return (loss, grad_x, *grads, *deltas, *new_m, *new_v)
```

```python
import functools

import jax
import jax.numpy as jnp
from jax import lax
from jax.experimental import pallas as pl
from jax.experimental.pallas import tpu as pltpu

F32 = jnp.float32
BF16 = jnp.bfloat16

EPS = 1e-6
D_MODEL = 1024
DEPTH = 4
N_DEV = 8
CHUNK = 64
SSD_DI = 2048
SSD_H = 32
SSD_P = 64
SSD_G = 8
SSD_R = SSD_H // SSD_G
SSD_N = 128
SSD_CONV = SSD_DI + 2 * SSD_G * SSD_N
SSD_IN = SSD_DI + SSD_CONV + SSD_H
LANES = 128
SSD_IN_PAD = -(-SSD_IN // LANES) * LANES
SSD_KW = 4
SC_KW = 3
FFN_F = 2816
FFN_KW = 3
TL = 256
HALO = 8
VMEM_LIMIT = 60 * 1024 * 1024

ADAM_LR = 0.001
ADAM_B1 = 0.9
ADAM_B2 = 0.999
ADAM_EPS = 1e-08
ADAM_WD = 0.01
ADAM_STEP = 10

MESH = pl.DeviceIdType.MESH


def _rms(x, g):
    r = lax.rsqrt(jnp.mean(x * x, axis=-1, keepdims=True) + EPS)
    return x * r * g


def _rms_bwd(x, g, dy):
    r = lax.rsqrt(jnp.mean(x * x, axis=-1, keepdims=True) + EPS)
    xh = x * r
    dg = jnp.sum(dy * xh, axis=0, keepdims=True)
    dxh = dy * g
    dx = r * (dxh - xh * jnp.mean(dxh * xh, axis=-1, keepdims=True))
    return dx, dg


def _mm(a, b):
    return jnp.dot(a, b, preferred_element_type=F32)


def _mm_nt(a, b):
    return lax.dot_general(a, b, (((1,), (1,)), ((), ())), preferred_element_type=F32)


def _mm_tn(a, b):
    return lax.dot_general(a, b, (((0,), (0,)), ((), ())), preferred_element_type=F32)


def _silu_parts(x):
    sg = jax.nn.sigmoid(x)
    return x * sg, sg * (1.0 + x * (1.0 - sg))


def _conv_fwd(ext, w, kw, tl):
    base = HALO - (kw - 1)
    out = ext[base:base + tl] * w[0:1]
    for j in range(1, kw):
        out = out + ext[base + j:base + j + tl] * w[j:j + 1]
    return out


def _conv_bwd_in(extd, w, kw, tl):
    out = extd[kw - 1:kw - 1 + tl] * w[0:1]
    for j in range(1, kw):
        out = out + extd[kw - 1 - j:kw - 1 - j + tl] * w[j:j + 1]
    return out


def _conv_bwd_w(ext, dy, kw, tl):
    base = HALO - (kw - 1)
    return jnp.concatenate(
        [jnp.sum(dy * ext[base + j:base + j + tl], axis=0, keepdims=True) for j in range(kw)], axis=0)


def _emit_row_shards(acc_ref, out_ref, stage_ref):
    rows = out_ref.shape[1]
    for k in range(N_DEV):
        stage_ref[...] = acc_ref[k * rows:(k + 1) * rows, :].astype(BF16)
        pltpu.sync_copy(stage_ref, out_ref.at[k])


def _emit_col_shards(acc_ref, out_ref, stage_ref):
    cols = out_ref.shape[2]
    for k in range(N_DEV):
        stage_ref[...] = acc_ref[:, k * cols:(k + 1) * cols].astype(BF16)
        pltpu.sync_copy(stage_ref, out_ref.at[k])


def _res(shape):
    nd = len(shape)
    return pl.BlockSpec(shape, lambda i: (0,) * nd, pipeline_mode=pl.Buffered(1))


def _small(shape):
    nd = len(shape)
    return pl.BlockSpec(shape, lambda i: (0,) * nd)


def _tile(n):
    return pl.BlockSpec((TL, n), lambda i: (i, 0))


def _rtile(n, nt):
    return pl.BlockSpec((TL, n), lambda i: (nt - 1 - i, 0))


def _halo_before(n, nt, reverse):
    per = TL // HALO
    if reverse:
        return pl.BlockSpec((HALO, n), lambda i: (jnp.maximum((nt - 1 - i) * per - 1, 0), 0))
    return pl.BlockSpec((HALO, n), lambda i: (jnp.maximum(i * per - 1, 0), 0))


_ANY = pl.BlockSpec(memory_space=pl.ANY)


def _sds(shape, dtype=F32):
    return jax.ShapeDtypeStruct(shape, dtype)


def _peer(k):
    x, y, c = lax.axis_index("x"), lax.axis_index("y"), lax.axis_index("c")
    px = x ^ (k >> 2)
    py = y ^ ((k >> 1) & 1)
    pc = c ^ (k & 1)
    return (px, py, pc), 4 * px + 2 * py + pc


def _my_index():
    return 4 * lax.axis_index("x") + 2 * lax.axis_index("y") + lax.axis_index("c")


def _job_copies(kind, src_ref, out_ref, send_sems, recv_sems, local_sems, j):
    me = _my_index()
    mine = src_ref if kind == "ag" else src_ref.at[me]
    local = pltpu.make_async_copy(mine, out_ref.at[me], local_sems.at[j])
    sends, recvs = [], []
    for k in range(1, N_DEV):
        dev, idx = _peer(k)
        s = src_ref if kind == "ag" else src_ref.at[idx]
        sems = dict(send_sem=send_sems.at[j, k - 1], recv_sem=recv_sems.at[j, k - 1], device_id=dev, device_id_type=MESH)
        sends.append(pltpu.make_async_remote_copy(src_ref=s, dst_ref=out_ref.at[me], **sems))
        recvs.append(pltpu.make_async_remote_copy(src_ref=s, dst_ref=out_ref.at[idx], **sems))
    return local, sends, recvs


def _pcall(body, name, grid, in_specs, out_specs, out_shape, scratch_shapes, args, jobs=()):
    n_in, n_out, nj = len(in_specs), len(out_specs), len(jobs)
    last = grid[0] - 1
    kinds = [k for k, _ in jobs]

    def wrapped(*refs):
        ins = refs[:n_in]
        csrc = refs[n_in:n_in + nj]
        outs = refs[n_in + nj:n_in + nj + n_out]
        cout = refs[n_in + nj + n_out:n_in + 2 * nj + n_out]
        rest = refs[n_in + 2 * nj + n_out:]
        if nj:
            scratch, (send_sems, recv_sems, local_sems) = rest[:-3], rest[-3:]
            i = pl.program_id(0)

            @pl.when(i == 0)
            def _():
                for j in range(nj):
                    local, sends, _ = _job_copies(kinds[j], csrc[j], cout[j], send_sems, recv_sems, local_sems, j)
                    local.start()
                    for cp in sends:
                        cp.start()
        else:
            scratch = rest
        body(*ins, *outs, *scratch)
        if nj:
            @pl.when(i == last)
            def _():
                for j in range(nj):
                    local, sends, recvs = _job_copies(kinds[j], csrc[j], cout[j], send_sems, recv_sems, local_sems, j)
                    for cp in recvs:
                        cp.wait_recv()
                    for cp in sends:
                        cp.wait_send()
                    local.wait()

    job_shapes = []
    for kind, s in jobs:
        shp = (N_DEV,) + tuple(s.shape) if kind == "ag" else tuple(s.shape)
        job_shapes.append(_sds(shp, s.dtype))
    sems = [pltpu.SemaphoreType.DMA((nj, N_DEV - 1)), pltpu.SemaphoreType.DMA((nj, N_DEV - 1)),
            pltpu.SemaphoreType.DMA((nj,))] if nj else []
    res = pl.pallas_call(
        wrapped, name=name, grid=grid,
        in_specs=list(in_specs) + [_ANY] * nj,
        out_specs=list(out_specs) + [_ANY] * nj,
        out_shape=list(out_shape) + job_shapes,
        scratch_shapes=list(scratch_shapes) + sems,
        compiler_params=pltpu.CompilerParams(dimension_semantics=("arbitrary",), vmem_limit_bytes=VMEM_LIMIT,
                                             has_side_effects=bool(nj)),
    )(*args, *[s for _, s in jobs])
    return list(res[:n_out]), list(res[n_out:])


def exchange(jobs, name):
    def body(o_ref):
        o_ref[...] = jnp.zeros_like(o_ref)

    _, outs = _pcall(body, name, (1,), [], [_small((8, LANES))], [_sds((8, LANES))], [], [], jobs)
    return outs


def ffn_fwd(x, g_pre, w_up, conv_w, conv_b, w_down, g_post, name, jobs=()):
    L = x.shape[0]
    nt = L // TL
    F = FFN_F

    def body(x_ref, gpre_ref, wup_ref, cw_ref, cb_ref, wdn_ref, gpost_ref, up_ref, f_ref, xn_ref, carry_ref):
        i = pl.program_id(0)

        @pl.when(i == 0)
        def _():
            carry_ref[...] = jnp.zeros_like(carry_ref)

        x = x_ref[...]
        h = _rms(x, gpre_ref[...]).astype(BF16)
        up = _mm(h, wup_ref[...])
        up_ref[...] = up
        ug = up[:, :F]
        val = up[:, F:]
        ext = jnp.concatenate([carry_ref[...], ug], axis=0)
        gate = _conv_fwd(ext, cw_ref[...], FFN_KW, TL) + cb_ref[...]
        carry_ref[...] = ug[TL - HALO:, :]
        a = (gate * jax.nn.sigmoid(gate) * val).astype(BF16)
        f = _mm(a, wdn_ref[...])
        f_ref[...] = f
        xn_ref[...] = x + _rms(f, gpost_ref[...])

    return _pcall(body, name, (nt,),
                  [_tile(D_MODEL), _small((1, D_MODEL)), _res((D_MODEL, 2 * F)), _small((FFN_KW, F)), _small((1, F)),
                   _res((F, D_MODEL)), _small((1, D_MODEL))],
                  [_tile(2 * F), _tile(D_MODEL), _tile(D_MODEL)],
                  [_sds((L, 2 * F)), _sds((L, D_MODEL)), _sds((L, D_MODEL))],
                  [pltpu.VMEM((HALO, F), F32)],
                  [x, g_pre, w_up, conv_w, conv_b, w_down, g_post], jobs)


def ffn_bwd1(dxo, f, up, g_post, w_down, conv_w, conv_b, name, jobs=()):
    L = dxo.shape[0]
    nt = L // TL
    F = FFN_F
    rows = F // N_DEV

    def body(dxo_ref, f_ref, up_ref, halo_ref, gpost_ref, wdn_ref, cw_ref, cb_ref,
             dup_ref, dwdn_ref, dgp_ref, dcw_ref, dcb_ref, acc_ref, carry_ref, stage_ref):
        i = pl.program_id(0)
        t = nt - 1 - i

        @pl.when(i == 0)
        def _():
            acc_ref[...] = jnp.zeros_like(acc_ref)
            carry_ref[...] = jnp.zeros_like(carry_ref)
            dgp_ref[...] = jnp.zeros_like(dgp_ref)
            dcw_ref[...] = jnp.zeros_like(dcw_ref)
            dcb_ref[...] = jnp.zeros_like(dcb_ref)

        df, dgp = _rms_bwd(f_ref[...], gpost_ref[...], dxo_ref[...])
        dgp_ref[...] += dgp
        dfb = df.astype(BF16)
        da = _mm_nt(dfb, wdn_ref[...])
        up = up_ref[...]
        ug = up[:, :F]
        val = up[:, F:]
        halo = jnp.where(t == 0, 0.0, halo_ref[...])
        ext = jnp.concatenate([halo, ug], axis=0)
        w = cw_ref[...]
        gate = _conv_fwd(ext, w, FFN_KW, TL) + cb_ref[...]
        s, ds = _silu_parts(gate)
        acc_ref[...] += _mm_tn((s * val).astype(BF16), dfb)
        dval = da * s
        dgate = da * val * ds
        dcb_ref[...] += jnp.sum(dgate, axis=0, keepdims=True)
        dcw_ref[...] += _conv_bwd_w(ext, dgate, FFN_KW, TL)
        extd = jnp.concatenate([dgate, carry_ref[...]], axis=0)
        dug = _conv_bwd_in(extd, w, FFN_KW, TL)
        carry_ref[...] = dgate[:HALO, :]
        dup_ref[...] = jnp.concatenate([dug, dval], axis=1).astype(BF16)

        @pl.when(i == nt - 1)
        def _():
            _emit_row_shards(acc_ref, dwdn_ref, stage_ref)

    return _pcall(body, name, (nt,),
                  [_rtile(D_MODEL, nt), _rtile(D_MODEL, nt), _rtile(2 * F, nt), _halo_before(F, nt, True),
                   _small((1, D_MODEL)), _res((F, D_MODEL)), _small((FFN_KW, F)), _small((1, F))],
                  [_rtile(2 * F, nt), _ANY, _small((1, D_MODEL)), _small((FFN_KW, F)), _small((1, F))],
                  [_sds((L, 2 * F), BF16), _sds((N_DEV, rows, D_MODEL), BF16), _sds((1, D_MODEL)),
                   _sds((FFN_KW, F)), _sds((1, F))],
                  [pltpu.VMEM((F, D_MODEL), F32), pltpu.VMEM((HALO, F), F32), pltpu.VMEM((rows, D_MODEL), BF16)],
                  [dxo, f, up, up, g_post, w_down, conv_w, conv_b], jobs)


def inproj_bwd(x, g_pre, d, w, dxo, cols, name, jobs=()):
    L = x.shape[0]
    nt = L // TL
    N = w.shape[1]

    def body(x_ref, g_ref, d_ref, w_ref, dxo_ref, dx_ref, dw_ref, dg_ref, acc_ref, stage_ref):
        i = pl.program_id(0)

        @pl.when(i == 0)
        def _():
            acc_ref[...] = jnp.zeros_like(acc_ref)
            dg_ref[...] = jnp.zeros_like(dg_ref)

        x = x_ref[...]
        g = g_ref[...]
        d = d_ref[...]
        h = _rms(x, g).astype(BF16)
        dh = _mm_nt(d, w_ref[...])
        acc_ref[...] += _mm_tn(h, d)
        dxn, dg = _rms_bwd(x, g, dh)
        dx_ref[...] = dxo_ref[...] + dxn
        dg_ref[...] += dg

        @pl.when(i == nt - 1)
        def _():
            _emit_col_shards(acc_ref, dw_ref, stage_ref)

    return _pcall(body, name, (nt,),
                  [_tile(D_MODEL), _small((1, D_MODEL)), _tile(N), _res((D_MODEL, N)), _tile(D_MODEL)],
                  [_tile(D_MODEL), _ANY, _small((1, D_MODEL))],
                  [_sds((L, D_MODEL)), _sds((N_DEV, D_MODEL, cols), BF16), _sds((1, D_MODEL))],
                  [pltpu.VMEM((D_MODEL, N), F32), pltpu.VMEM((D_MODEL, cols), BF16)],
                  [x, g_pre, d, w, dxo], jobs)


def sc_fwd(x, g_pre, w_in, conv_w, w_out, g_post, name, jobs=()):
    L = x.shape[0]
    nt = L // TL
    W = D_MODEL

    def body(x_ref, gpre_ref, win_ref, cw_ref, wout_ref, gpost_ref, bcv_ref, m_ref, xn_ref, carry_ref):
        i = pl.program_id(0)

        @pl.when(i == 0)
        def _():
            carry_ref[...] = jnp.zeros_like(carry_ref)

        x = x_ref[...]
        h = _rms(x, gpre_ref[...]).astype(BF16)
        bcv = _mm(h, win_ref[...])
        bcv_ref[...] = bcv
        gb = bcv[:, :W]
        p = bcv[:, W:2 * W] * bcv[:, 2 * W:]
        ext = jnp.concatenate([carry_ref[...], p], axis=0)
        u = _conv_fwd(ext, cw_ref[...], SC_KW, TL)
        carry_ref[...] = p[TL - HALO:, :]
        m = _mm((gb * u).astype(BF16), wout_ref[...])
        m_ref[...] = m
        xn_ref[...] = x + _rms(m, gpost_ref[...])

    return _pcall(body, name, (nt,),
                  [_tile(W), _small((1, W)), _res((W, 3 * W)), _small((SC_KW, W)), _res((W, W)), _small((1, W))],
                  [_tile(3 * W), _tile(W), _tile(W)],
                  [_sds((L, 3 * W)), _sds((L, W)), _sds((L, W))],
                  [pltpu.VMEM((HALO, W), F32)],
                  [x, g_pre, w_in, conv_w, w_out, g_post], jobs)


def sc_bwd1(dxo, m, bcv, g_post, w_out, conv_w, name, jobs=()):
    L = dxo.shape[0]
    nt = L // TL
    W = D_MODEL
    rows = W // N_DEV

    def body(dxo_ref, m_ref, bcv_ref, halo_ref, gpost_ref, wout_ref, cw_ref,
             dbcv_ref, dwout_ref, dgp_ref, dcw_ref, acc_ref, carry_ref, stage_ref):
        i = pl.program_id(0)
        t = nt - 1 - i

        @pl.when(i == 0)
        def _():
            acc_ref[...] = jnp.zeros_like(acc_ref)
            carry_ref[...] = jnp.zeros_like(carry_ref)
            dgp_ref[...] = jnp.zeros_like(dgp_ref)
            dcw_ref[...] = jnp.zeros_like(dcw_ref)

        dm, dgp = _rms_bwd(m_ref[...], gpost_ref[...], dxo_ref[...])
        dgp_ref[...] += dgp
        dmb = dm.astype(BF16)
        dq = _mm_nt(dmb, wout_ref[...])
        bcv = bcv_ref[...]
        gb = bcv[:, :W]
        gc = bcv[:, W:2 * W]
        v = bcv[:, 2 * W:]
        hb = halo_ref[...]
        halo = jnp.where(t == 0, 0.0, hb[:, W:2 * W] * hb[:, 2 * W:])
        ext = jnp.concatenate([halo, gc * v], axis=0)
        w = cw_ref[...]
        u = _conv_fwd(ext, w, SC_KW, TL)
        acc_ref[...] += _mm_tn((gb * u).astype(BF16), dmb)
        dgb = dq * u
        du = dq * gb
        dcw_ref[...] += _conv_bwd_w(ext, du, SC_KW, TL)
        extd = jnp.concatenate([du, carry_ref[...]], axis=0)
        dp = _conv_bwd_in(extd, w, SC_KW, TL)
        carry_ref[...] = du[:HALO, :]
        dbcv_ref[...] = jnp.concatenate([dgb, dp * v, dp * gc], axis=1).astype(BF16)

        @pl.when(i == nt - 1)
        def _():
            _emit_row_shards(acc_ref, dwout_ref, stage_ref)

    return _pcall(body, name, (nt,),
                  [_rtile(W, nt), _rtile(W, nt), _rtile(3 * W, nt), _halo_before(3 * W, nt, True),
                   _small((1, W)), _res((W, W)), _small((SC_KW, W))],
                  [_rtile(3 * W, nt), _ANY, _small((1, W)), _small((SC_KW, W))],
                  [_sds((L, 3 * W), BF16), _sds((N_DEV, rows, W), BF16), _sds((1, W)), _sds((SC_KW, W))],
                  [pltpu.VMEM((W, W), F32), pltpu.VMEM((HALO, W), F32), pltpu.VMEM((rows, W), BF16)],
                  [dxo, m, bcv, bcv, g_post, w_out, conv_w], jobs)


def ssd_inproj(x, g_pre, w_in, conv_w, conv_b, name, jobs=()):
    L = x.shape[0]
    nt = L // TL

    def body(x_ref, gpre_ref, win_ref, cw_ref, cb_ref, z_ref, raw_ref, dt_ref, xh_ref, bm_ref, cm_ref, carry_ref):
        i = pl.program_id(0)

        @pl.when(i == 0)
        def _():
            carry_ref[...] = jnp.zeros_like(carry_ref)

        h = _rms(x_ref[...], gpre_ref[...]).astype(BF16)
        zx = _mm(h, win_ref[...])
        z_ref[...] = zx[:, :SSD_DI]
        raw = zx[:, SSD_DI:SSD_DI + SSD_CONV]
        raw_ref[...] = raw
        dt_ref[...] = zx[:, SSD_DI + SSD_CONV:SSD_IN]
        ext = jnp.concatenate([carry_ref[...], raw], axis=0)
        pre = _conv_fwd(ext, cw_ref[...], SSD_KW, TL) + cb_ref[...]
        carry_ref[...] = raw[TL - HALO:, :]
        act = pre * jax.nn.sigmoid(pre)
        for hh in range(SSD_H):
            xh_ref[hh] = act[:, hh * SSD_P:(hh + 1) * SSD_P]
        for g in range(SSD_G):
            bm_ref[g] = act[:, SSD_DI + g * SSD_N:SSD_DI + (g + 1) * SSD_N]
            cm_ref[g] = act[:, SSD_DI + (SSD_G + g) * SSD_N:SSD_DI + (SSD_G + g + 1) * SSD_N]

    return _pcall(body, name, (nt,),
                  [_tile(D_MODEL), _small((1, D_MODEL)), _res((D_MODEL, SSD_IN_PAD)), _small((SSD_KW, SSD_CONV)),
                   _small((1, SSD_CONV))],
                  [_tile(SSD_DI), _tile(SSD_CONV), _tile(SSD_H),
                   pl.BlockSpec((SSD_H, TL, SSD_P), lambda i: (0, i, 0)),
                   pl.BlockSpec((SSD_G, TL, SSD_N), lambda i: (0, i, 0)),
                   pl.BlockSpec((SSD_G, TL, SSD_N), lambda i: (0, i, 0))],
                  [_sds((L, SSD_DI)), _sds((L, SSD_CONV)), _sds((L, SSD_H)), _sds((SSD_H, L, SSD_P)),
                   _sds((SSD_G, L, SSD_N)), _sds((SSD_G, L, SSD_N))],
                  [pltpu.VMEM((HALO, SSD_CONV), F32)],
                  [x, g_pre, w_in, conv_w, conv_b], jobs)


def _per_head(v, rows):
    return jnp.stack([v[:, h:h + 1] for h in range(SSD_H)], axis=0)


def _heads_to_lanes(v):
    return jnp.concatenate([v[h] for h in range(SSD_H)], axis=1)


def _rep_heads(v):
    g, a, b = v.shape
    return jnp.broadcast_to(v[:, None], (g, SSD_R, a, b)).reshape(g * SSD_R, a, b)


def _sum_heads(v):
    h, a, b = v.shape
    return v.reshape(SSD_G, SSD_R, a, b).sum(axis=1)


def _ssd_decays(dtr, bias, a_log):
    T = CHUNK
    dt = jax.nn.softplus(dtr + bias)
    a_head = -jnp.exp(a_log)
    a = dt * a_head
    ii = lax.broadcasted_iota(jnp.int32, (T, T), 0)
    jj = lax.broadcasted_iota(jnp.int32, (T, T), 1)
    tri = ii >= jj
    cs = jnp.dot(tri.astype(F32), a, precision=lax.Precision.HIGHEST, preferred_element_type=F32)
    cs_t = cs.T
    csc = _per_head(cs, T)
    csr = jnp.stack([cs_t[h:h + 1, :] for h in range(SSD_H)], axis=0)
    dtc = _per_head(dt, T)
    cl = _per_head(cs[T - 1:T, :], 1)
    lmat = jnp.exp(jnp.where(tri[None], csc - csr, -jnp.inf))
    return dict(dt=dt, a_head=a_head, tri=tri, csc=csc, dtc=dtc, lmat=lmat,
                ecs=jnp.exp(csc), dsc=jnp.exp(cl - csc), cdc=jnp.exp(cl))


def ssd_scan_fwd(xh, bm, cm, dt_raw, dt_bias, a_log, d_skip, name, jobs=()):
    L = xh.shape[1]
    nc = L // CHUNK
    T = CHUNK

    def body(xh_ref, bm_ref, cm_ref, dt_ref, bias_ref, alog_ref, dsk_ref, y_ref, sp_ref, st_ref):
        c = pl.program_id(0)

        @pl.when(c == 0)
        def _():
            st_ref[...] = jnp.zeros_like(st_ref)

        dec = _ssd_decays(dt_ref[...], bias_ref[...], alog_ref[...])
        x = xh_ref[...]
        bgb = bm_ref[...].astype(BF16)
        cgb = cm_ref[...].astype(BF16)
        bh = _rep_heads(bgb)
        ch = _rep_heads(cgb)
        dh = _per_head(dsk_ref[...], 1)
        xt = x * dec["dtc"]
        cb = jnp.einsum("gln,gsn->gls", cgb, bgb, preferred_element_type=F32)
        mb = (_rep_heads(cb) * dec["lmat"]).astype(BF16)
        yd = jnp.einsum("hls,hsp->hlp", mb, xt.astype(BF16), preferred_element_type=F32)
        s = st_ref[...]
        sb = s.astype(BF16)
        yo = jnp.einsum("hln,hpn->hlp", ch, sb, preferred_element_type=F32) * dec["ecs"]
        y_ref[...] = yd + yo + x * dh
        sp_ref[0] = sb
        xd = (xt * dec["dsc"]).astype(BF16)
        st_ref[...] = s * dec["cdc"] + jnp.einsum("htp,htn->hpn", xd, bh, preferred_element_type=F32)

    hd = pl.BlockSpec((SSD_H, T, SSD_P), lambda c: (0, c, 0))
    gr = pl.BlockSpec((SSD_G, T, SSD_N), lambda c: (0, c, 0))
    return _pcall(body, name, (nc,),
                  [hd, gr, gr, pl.BlockSpec((T, SSD_H), lambda c: (c, 0)),
                   _small((1, SSD_H)), _small((1, SSD_H)), _small((1, SSD_H))],
                  [hd, pl.BlockSpec((1, SSD_H, SSD_P, SSD_N), lambda c: (c, 0, 0, 0))],
                  [_sds((SSD_H, L, SSD_P)), _sds((nc, SSD_H, SSD_P, SSD_N), BF16)],
                  [pltpu.VMEM((SSD_H, SSD_P, SSD_N), F32)],
                  [xh, bm, cm, dt_raw, dt_bias, a_log, d_skip], jobs)


def ssd_scan_bwd(dy, xh, bm, cm, dt_raw, sprev, dt_bias, a_log, d_skip, name, jobs=()):
    L = xh.shape[1]
    nc = L // CHUNK
    T = CHUNK

    def body(dy_ref, xh_ref, bm_ref, cm_ref, dt_ref, sp_ref, bias_ref, alog_ref, dsk_ref,
             dxh_ref, dbm_ref, dcm_ref, ddt_ref, dbias_ref, dalog_ref, ddsk_ref, g_ref):
        i = pl.program_id(0)

        @pl.when(i == 0)
        def _():
            g_ref[...] = jnp.zeros_like(g_ref)
            dbias_ref[...] = jnp.zeros_like(dbias_ref)
            dalog_ref[...] = jnp.zeros_like(dalog_ref)
            ddsk_ref[...] = jnp.zeros_like(ddsk_ref)

        dtr = dt_ref[...]
        bias = bias_ref[...]
        dec = _ssd_decays(dtr, bias, alog_ref[...])
        dt, a_head, tri = dec["dt"], dec["a_head"], dec["tri"]
        dtc, lmat, ecs, dsc, cdc = dec["dtc"], dec["lmat"], dec["ecs"], dec["dsc"], dec["cdc"]
        x = xh_ref[...]
        dyv = dy_ref[...]
        dyb = dyv.astype(BF16)
        bgb = bm_ref[...].astype(BF16)
        cgb = cm_ref[...].astype(BF16)
        bh = _rep_heads(bgb)
        ch = _rep_heads(cgb)
        sb = sp_ref[0]
        g = g_ref[...]
        gb = g.astype(BF16)
        dh = _per_head(dsk_ref[...], 1)
        xt = x * dtc
        xtb = xt.astype(BF16)
        cb = jnp.einsum("gln,gsn->gls", cgb, bgb, preferred_element_type=F32)
        mf = _rep_heads(cb) * lmat
        mb = mf.astype(BF16)
        ddsk = jnp.sum(dyv * x, axis=(1, 2), keepdims=True)
        dx = dyv * dh
        yo_raw = jnp.einsum("hln,hpn->hlp", ch, sb, preferred_element_type=F32)
        w1 = dyv * ecs
        w1b = w1.astype(BF16)
        ds_off = jnp.einsum("hlp,hln->hpn", w1b, ch, preferred_element_type=F32)
        dch = jnp.einsum("hlp,hpn->hln", w1b, sb, preferred_element_type=F32)
        dcs_c = jnp.sum(w1 * yo_raw, axis=2, keepdims=True)
        dm = jnp.einsum("hlp,hsp->hls", dyb, xtb, preferred_element_type=F32)
        dxt = jnp.einsum("hls,hlp->hsp", mb, dyb, preferred_element_type=F32)
        dcbb = _sum_heads(dm * lmat).astype(BF16)
        dseg = dm * mf
        dcs_c = dcs_c + jnp.sum(dseg, axis=2, keepdims=True)
        dcs_r = -jnp.sum(dseg, axis=1, keepdims=True)
        dc = jnp.einsum("gls,gsn->gln", dcbb, bgb, preferred_element_type=F32) + _sum_heads(dch)
        db = jnp.einsum("gls,gln->gsn", dcbb, cgb, preferred_element_type=F32)
        xd = xt * dsc
        dxd = jnp.einsum("htn,hpn->htp", bh, gb, preferred_element_type=F32)
        db = db + _sum_heads(jnp.einsum("htp,hpn->htn", xd.astype(BF16), gb, preferred_element_type=F32))
        dxt = dxt + dxd * dsc
        d_ds = jnp.sum(dxd * xt, axis=2, keepdims=True)
        d_cd = jnp.sum(g * sb.astype(F32), axis=(1, 2), keepdims=True)
        g_ref[...] = g * cdc + ds_off
        t1 = d_ds * dsc
        dcs_c = dcs_c - t1
        dcl = jnp.sum(t1, axis=1, keepdims=True) + d_cd * cdc
        ddt_c = jnp.sum(dxt * x, axis=2, keepdims=True)
        dxh_ref[...] = dx + dxt * dtc
        dbm_ref[...] = db
        dcm_ref[...] = dc
        rows_t = jnp.concatenate([dcs_r[h] for h in range(SSD_H)], axis=0).T
        last = (lax.broadcasted_iota(jnp.int32, (T, 1), 0) == T - 1).astype(F32)
        dcs = _heads_to_lanes(dcs_c) + rows_t + last * _heads_to_lanes(dcl)
        da = lax.dot_general(tri.astype(F32), dcs, (((0,), (0,)), ((), ())),
                             precision=lax.Precision.HIGHEST, preferred_element_type=F32)
        ddt = da * a_head + _heads_to_lanes(ddt_c)
        dalog_ref[...] += jnp.sum(da * dt, axis=0, keepdims=True)
        ddtr = ddt * jax.nn.sigmoid(dtr + bias)
        ddt_ref[...] = ddtr
        dbias_ref[...] += jnp.sum(ddtr, axis=0, keepdims=True)
        ddsk_ref[...] += _heads_to_lanes(ddsk)

        @pl.when(i == nc - 1)
        def _():
            dalog_ref[...] = dalog_ref[...] * a_head

    hd = pl.BlockSpec((SSD_H, T, SSD_P), lambda i: (0, nc - 1 - i, 0))
    gr = pl.BlockSpec((SSD_G, T, SSD_N), lambda i: (0, nc - 1 - i, 0))
    tk = pl.BlockSpec((T, SSD_H), lambda i: (nc - 1 - i, 0))
    return _pcall(body, name, (nc,),
                  [hd, hd, gr, gr, tk, pl.BlockSpec((1, SSD_H, SSD_P, SSD_N), lambda i: (nc - 1 - i, 0, 0, 0)),
                   _small((1, SSD_H)), _small((1, SSD_H)), _small((1, SSD_H))],
                  [hd, gr, gr, tk, _small((1, SSD_H)), _small((1, SSD_H)), _small((1, SSD_H))],
                  [_sds((SSD_H, L, SSD_P)), _sds((SSD_G, L, SSD_N)), _sds((SSD_G, L, SSD_N)), _sds((L, SSD_H)),
                   _sds((1, SSD_H)), _sds((1, SSD_H)), _sds((1, SSD_H))],
                  [pltpu.VMEM((SSD_H, SSD_P, SSD_N), F32)],
                  [dy, xh, bm, cm, dt_raw, sprev, dt_bias, a_log, d_skip], jobs)


def _heads_to_tokens(y_ref):
    return jnp.concatenate([y_ref[h] for h in range(SSD_H)], axis=1)


def ssd_out_fwd(x, y, z, norm_w, w_out, g_post, name, jobs=()):
    L = x.shape[0]
    nt = L // TL

    def body(x_ref, y_ref, z_ref, nw_ref, wout_ref, gpost_ref, m_ref, xn_ref):
        z = z_ref[...]
        yg = _heads_to_tokens(y_ref) * (z * jax.nn.sigmoid(z))
        yn = _rms(yg, nw_ref[...]).astype(BF16)
        m = _mm(yn, wout_ref[...])
        m_ref[...] = m
        xn_ref[...] = x_ref[...] + _rms(m, gpost_ref[...])

    return _pcall(body, name, (nt,),
                  [_tile(D_MODEL), pl.BlockSpec((SSD_H, TL, SSD_P), lambda i: (0, i, 0)), _tile(SSD_DI),
                   _small((1, SSD_DI)), _res((SSD_DI, D_MODEL)), _small((1, D_MODEL))],
                  [_tile(D_MODEL), _tile(D_MODEL)],
                  [_sds((L, D_MODEL)), _sds((L, D_MODEL))],
                  [],
                  [x, y, z, norm_w, w_out, g_post], jobs)


def ssd_out_bwd(dxo, m, y, z, norm_w, w_out, g_post, name, jobs=()):
    L = dxo.shape[0]
    nt = L // TL
    rows = SSD_DI // N_DEV

    def body(dxo_ref, m_ref, y_ref, z_ref, nw_ref, wout_ref, gpost_ref,
             dy_ref, dz_ref, dwout_ref, dgp_ref, dnw_ref, acc_ref, stage_ref):
        i = pl.program_id(0)

        @pl.when(i == 0)
        def _():
            acc_ref[...] = jnp.zeros_like(acc_ref)
            dgp_ref[...] = jnp.zeros_like(dgp_ref)
            dnw_ref[...] = jnp.zeros_like(dnw_ref)

        dm, dgp = _rms_bwd(m_ref[...], gpost_ref[...], dxo_ref[...])
        dgp_ref[...] += dgp
        dmb = dm.astype(BF16)
        dyn = _mm_nt(dmb, wout_ref[...])
        z = z_ref[...]
        y = _heads_to_tokens(y_ref)
        sil, dsil = _silu_parts(z)
        yg = y * sil
        nw = nw_ref[...]
        acc_ref[...] += _mm_tn(_rms(yg, nw).astype(BF16), dmb)
        dyg, dnw = _rms_bwd(yg, nw, dyn)
        dnw_ref[...] += dnw
        dyv = dyg * sil
        dz_ref[...] = dyg * y * dsil
        for h in range(SSD_H):
            dy_ref[h] = dyv[:, h * SSD_P:(h + 1) * SSD_P]

        @pl.when(i == nt - 1)
        def _():
            _emit_row_shards(acc_ref, dwout_ref, stage_ref)

    hd = pl.BlockSpec((SSD_H, TL, SSD_P), lambda i: (0, i, 0))
    return _pcall(body, name, (nt,),
                  [_tile(D_MODEL), _tile(D_MODEL), hd, _tile(SSD_DI), _small((1, SSD_DI)), _res((SSD_DI, D_MODEL)),
                   _small((1, D_MODEL))],
                  [hd, _tile(SSD_DI), _ANY, _small((1, D_MODEL)), _small((1, SSD_DI))],
                  [_sds((SSD_H, L, SSD_P)), _sds((L, SSD_DI)), _sds((N_DEV, rows, D_MODEL), BF16),
                   _sds((1, D_MODEL)), _sds((1, SSD_DI))],
                  [pltpu.VMEM((SSD_DI, D_MODEL), F32), pltpu.VMEM((rows, D_MODEL), BF16)],
                  [dxo, m, y, z, norm_w, w_out, g_post], jobs)


def ssd_conv_bwd(dxh, dbm, dcm, xbc_raw, dz, ddt_raw, conv_w, conv_b, name, jobs=()):
    L = xbc_raw.shape[0]
    nt = L // TL

    def body(dxh_ref, dbm_ref, dcm_ref, raw_ref, halo_ref, dz_ref, ddt_ref, cw_ref, cb_ref,
             d_ref, dcw_ref, dcb_ref, carry_ref):
        i = pl.program_id(0)
        t = nt - 1 - i

        @pl.when(i == 0)
        def _():
            carry_ref[...] = jnp.zeros_like(carry_ref)
            dcw_ref[...] = jnp.zeros_like(dcw_ref)
            dcb_ref[...] = jnp.zeros_like(dcb_ref)

        dact = jnp.concatenate([dxh_ref[h] for h in range(SSD_H)] + [dbm_ref[g] for g in range(SSD_G)]
                               + [dcm_ref[g] for g in range(SSD_G)], axis=1)
        halo = jnp.where(t == 0, 0.0, halo_ref[...])
        ext = jnp.concatenate([halo, raw_ref[...]], axis=0)
        w = cw_ref[...]
        pre = _conv_fwd(ext, w, SSD_KW, TL) + cb_ref[...]
        _, dsil = _silu_parts(pre)
        dpre = dact * dsil
        dcb_ref[...] += jnp.sum(dpre, axis=0, keepdims=True)
        dcw_ref[...] += _conv_bwd_w(ext, dpre, SSD_KW, TL)
        extd = jnp.concatenate([dpre, carry_ref[...]], axis=0)
        draw = _conv_bwd_in(extd, w, SSD_KW, TL)
        carry_ref[...] = dpre[:HALO, :]
        d_ref[:, :SSD_DI] = dz_ref[...].astype(BF16)
        d_ref[:, SSD_DI:SSD_DI + SSD_CONV] = draw.astype(BF16)
        tail = jnp.concatenate([ddt_ref[...], jnp.zeros((TL, SSD_IN_PAD - SSD_IN), F32)], axis=1)
        d_ref[:, SSD_DI + SSD_CONV:] = tail.astype(BF16)

    hd = pl.BlockSpec((SSD_H, TL, SSD_P), lambda i: (0, nt - 1 - i, 0))
    gr = pl.BlockSpec((SSD_G, TL, SSD_N), lambda i: (0, nt - 1 - i, 0))
    return _pcall(body, name, (nt,),
                  [hd, gr, gr, _rtile(SSD_CONV, nt), _halo_before(SSD_CONV, nt, True), _rtile(SSD_DI, nt),
                   _rtile(SSD_H, nt), _small((SSD_KW, SSD_CONV)), _small((1, SSD_CONV))],
                  [_rtile(SSD_IN_PAD, nt), _small((SSD_KW, SSD_CONV)), _small((1, SSD_CONV))],
                  [_sds((L, SSD_IN_PAD), BF16), _sds((SSD_KW, SSD_CONV)), _sds((1, SSD_CONV))],
                  [pltpu.VMEM((HALO, SSD_CONV), F32)],
                  [dxh, dbm, dcm, xbc_raw, xbc_raw, dz, ddt_raw, conv_w, conv_b], jobs)


def loss_fwd_bwd(y, target, name, jobs=()):
    L = y.shape[0]
    nt = L // TL

    def body(y_ref, t_ref, loss_ref, dy_ref):
        i = pl.program_id(0)

        @pl.when(i == 0)
        def _():
            loss_ref[...] = jnp.zeros_like(loss_ref)

        err = y_ref[...] - t_ref[...]
        dy_ref[...] = err * (1.0 / D_MODEL)
        loss_ref[...] += 0.5 * jnp.sum(jnp.mean(err * err, axis=-1, keepdims=True), axis=0, keepdims=True)

    return _pcall(body, name, (nt,),
                  [_tile(D_MODEL), _tile(D_MODEL)],
                  [_small((1, 1)), _tile(D_MODEL)],
                  [_sds((1, 1)), _sds((L, D_MODEL))],
                  [],
                  [y, target], jobs)


def _adamw_math(w, g, m, v):
    m = ADAM_B1 * m + (1.0 - ADAM_B1) * g
    v = ADAM_B2 * v + (1.0 - ADAM_B2) * (g * g)
    m_hat = m / (1.0 - ADAM_B1 ** ADAM_STEP)
    v_hat = v / (1.0 - ADAM_B2 ** ADAM_STEP)
    delta = -ADAM_LR * (m_hat / (jnp.sqrt(v_hat) + ADAM_EPS) + ADAM_WD * w)
    return delta, m, v


def _row_tile(rows):
    for cand in (256, 176, 128, 64, 32, 16, 8):
        if rows % cand == 0:
            return cand
    return rows


def reduce_adamw(recvs, w, m, v, name):
    nl = len(recvs)
    _, R, C = recvs[0].shape
    tr = _row_tile(R)

    def body(*refs):
        r_refs = refs[:nl]
        w_ref, m_ref, v_ref, g_out, d_out, m_out, v_out = refs[nl:]
        layer = pl.program_id(0)
        for ll in range(nl):
            @pl.when(layer == ll)
            def _(ll=ll):
                g = r_refs[ll][0].astype(F32)
                for j in range(1, N_DEV):
                    g = g + r_refs[ll][j].astype(F32)
                delta, mn, vn = _adamw_math(w_ref[0], g, m_ref[0], v_ref[0])
                g_out[0] = g
                d_out[0] = delta
                m_out[0] = mn
                v_out[0] = vn

    def recv_spec(ll):
        return pl.BlockSpec((N_DEV, tr, C), lambda l, r: (0, jnp.where(l == ll, r, 0), 0))

    blk = pl.BlockSpec((1, tr, C), lambda l, r: (l, r, 0))
    return pl.pallas_call(
        body, name=name, grid=(nl, R // tr),
        in_specs=[recv_spec(ll) for ll in range(nl)] + [blk, blk, blk],
        out_specs=[blk] * 4,
        out_shape=[_sds((nl, R, C))] * 4,
        compiler_params=pltpu.CompilerParams(dimension_semantics=("arbitrary", "arbitrary"),
                                             vmem_limit_bytes=VMEM_LIMIT),
    )(*recvs, w, m, v)


def small_reduce(gathered, name):
    _, R, C = gathered.shape

    def body(r_ref, o_ref):
        g = r_ref[0]
        for j in range(1, N_DEV):
            g = g + r_ref[j]
        o_ref[...] = g

    return pl.pallas_call(body, name=name, out_shape=_sds((R, C)))(gathered)


def small_adamw(g, w, m, v, name):
    def body(g_ref, w_ref, m_ref, v_ref, d_out, m_out, v_out):
        delta, mn, vn = _adamw_math(w_ref[...], g_ref[...], m_ref[...], v_ref[...])
        d_out[...] = delta
        m_out[...] = mn
        v_out[...] = vn

    return pl.pallas_call(body, name=name, out_shape=[_sds(g.shape)] * 3)(g, w, m, v)


def _pack(arrs):
    flat = jnp.concatenate([a.reshape(-1) for a in arrs])
    n = flat.shape[0]
    rows = -(-n // (8 * LANES)) * 8
    flat = jnp.pad(flat, (0, rows * LANES - n))
    return flat.reshape(rows, LANES)


def _unpack(packed, shapes):
    flat = packed.reshape(-1)
    out = []
    off = 0
    for s in shapes:
        n = 1
        for d in s:
            n *= d
        out.append(flat[off:off + n].reshape(s))
        off += n
    return out


def kernel(x, mix_pre_g, mix_post_g, ffn_pre_g, ffn_post_g, ssd_w_in, ssd_conv_w, ssd_conv_b, ssd_dt_bias, ssd_A_log, ssd_D, ssd_norm_w, ssd_w_out, sc_w_in, sc_conv_w, sc_w_out, ffn_w_up, ffn_conv_w, ffn_conv_b, ffn_w_down, loss_target, m_mix_pre_g, m_mix_post_g, m_ffn_pre_g, m_ffn_post_g, m_ssd_w_in, m_ssd_conv_w, m_ssd_conv_b, m_ssd_dt_bias, m_ssd_A_log, m_ssd_D, m_ssd_norm_w, m_ssd_w_out, m_sc_w_in, m_sc_conv_w, m_sc_w_out, m_ffn_w_up, m_ffn_conv_w, m_ffn_conv_b, m_ffn_w_down, v_mix_pre_g, v_mix_post_g, v_ffn_pre_g, v_ffn_post_g, v_ssd_w_in, v_ssd_conv_w, v_ssd_conv_b, v_ssd_dt_bias, v_ssd_A_log, v_ssd_D, v_ssd_norm_w, v_ssd_w_out, v_sc_w_in, v_sc_conv_w, v_sc_w_out, v_ffn_w_up, v_ffn_conv_w, v_ffn_conv_b, v_ffn_w_down):
    me = _my_index()
    x0 = x[0]
    target = loss_target[0]
    row = lambda a: a.reshape(1, -1)

    shards = {"ssd_in": ssd_w_in, "ssd_out": ssd_w_out, "sc_in": sc_w_in, "sc_out": sc_w_out,
              "up": ffn_w_up, "down": ffn_w_down}
    col_sharded = {"ssd_in": SSD_IN_PAD, "sc_in": None, "up": None}
    weights = {}

    def ag_jobs(keys):
        return [("ag", shards[n][l].astype(BF16)) for n, l in keys]

    def store_weights(keys, outs):
        for (n, l), g in zip(keys, outs):
            _, R, C = g.shape
            if n in col_sharded:
                full = jnp.transpose(g, (1, 0, 2)).reshape(R, N_DEV * C)
                if col_sharded[n] is not None:
                    full = jnp.pad(full, ((0, 0), (0, col_sharded[n] - N_DEV * C)))
            else:
                full = g.reshape(N_DEV * R, C)
            weights[(n, l)] = full

    fwd_sched = {
        "ssd_inproj_0": [("up", 0)], "ssd_scan_fwd_0": [("down", 0), ("sc_in", 0)], "ssd_out_fwd_0": [("sc_out", 0)],
        "ffn_fwd_0": [("up", 1)], "sc_fwd_0": [("down", 1)], "ffn_fwd_1": [("ssd_in", 1)],
        "ssd_inproj_1": [("ssd_out", 1), ("down", 2)], "ssd_scan_fwd_1": [("up", 2)], "ssd_out_fwd_1": [("sc_out", 1)],
        "ffn_fwd_2": [("sc_in", 1), ("down", 3)], "sc_fwd_1": [("up", 3)],
    }
    bwd_sched = {
        "ffn_bwd2_3": [("down", 3)], "sc_bwd2_1": [("sc_out", 1)], "ffn_bwd1_2": [("up", 3)], "ffn_bwd2_2": [("sc_in", 1)],
        "ssd_out_bwd_1": [("down", 2)], "ssd_scan_bwd_1": [("up", 2), ("ssd_out", 1)], "ffn_bwd1_1": [("ssd_in", 1)],
        "ffn_bwd2_1": [("down", 1)], "sc_bwd2_0": [("sc_out", 0)], "ffn_bwd1_0": [("up", 1)], "ffn_bwd2_0": [("sc_in", 0)],
        "ssd_out_bwd_0": [("down", 0)], "ssd_scan_bwd_0": [("up", 0), ("ssd_out", 0)],
    }

    first = [("ssd_in", 0), ("ssd_out", 0)]
    outs = exchange(ag_jobs(first) + [("ag", ssd_conv_w), ("ag", sc_conv_w), ("ag", ffn_conv_w)], "ag_first")
    store_weights(first, outs[:2])

    def taps(g):
        _, nl, K, C = g.shape
        return jnp.transpose(g, (1, 2, 0, 3)).reshape(nl, K, N_DEV * C)

    CW_ssd, CW_sc, CW_ffn = taps(outs[2]), taps(outs[3]), taps(outs[4])

    def fwd(fn, name, *args):
        keys = fwd_sched.get(name, [])
        res, got = fn(*args, name, ag_jobs(keys))
        store_weights(keys, got)
        return res

    saved = []
    h = x0
    for i in range(DEPTH):
        j = i // 2
        blk = dict(x_mix=h)
        if i % 2 == 0:
            z, raw, dt_raw, xh, bm, cm = fwd(ssd_inproj, f"ssd_inproj_{j}", h, row(mix_pre_g[i]), weights[("ssd_in", j)],
                                             CW_ssd[j], row(ssd_conv_b[j]))
            y, sprev = fwd(ssd_scan_fwd, f"ssd_scan_fwd_{j}", xh, bm, cm, dt_raw, row(ssd_dt_bias[j]),
                           row(ssd_A_log[j]), row(ssd_D[j]))
            m, h = fwd(ssd_out_fwd, f"ssd_out_fwd_{j}", h, y, z, row(ssd_norm_w[j]), weights[("ssd_out", j)],
                       row(mix_post_g[i]))
            blk.update(z=z, raw=raw, dt_raw=dt_raw, xh=xh, bm=bm, cm=cm, y=y, sprev=sprev, m=m)
        else:
            bcv, m, h = fwd(sc_fwd, f"sc_fwd_{j}", h, row(mix_pre_g[i]), weights[("sc_in", j)], CW_sc[j],
                            weights[("sc_out", j)], row(mix_post_g[i]))
            blk.update(bcv=bcv, m=m)
        blk["x_ffn"] = h
        up, f, h = fwd(ffn_fwd, f"ffn_fwd_{i}", h, row(ffn_pre_g[i]), weights[("up", i)], CW_ffn[i],
                       row(ffn_conv_b[i]), weights[("down", i)], row(ffn_post_g[i]))
        blk.update(up=up, f=f)
        saved.append(blk)

    (loss_dev, dh), _ = loss_fwd_bwd(h, target, "loss")
    loss = lax.psum(loss_dev[0, 0], ("x", "y", "c"))

    parts, recvd = {}, {}

    def bwd(fn, name, *args):
        keys = bwd_sched.get(name, [])
        res, got = fn(*args, name, [("a2a", parts[k]) for k in keys])
        for k, g in zip(keys, got):
            recvd[k] = g
        return res

    g_mix_pre, g_mix_post, g_ffn_pre, g_ffn_post = [None] * DEPTH, [None] * DEPTH, [None] * DEPTH, [None] * DEPTH
    g_ffn_cw, g_ffn_cb = [None] * DEPTH, [None] * DEPTH
    g_ssd_cw, g_ssd_cb, g_ssd_dtb, g_ssd_alog, g_ssd_d, g_ssd_nw = ([None] * 2 for _ in range(6))
    g_sc_cw = [None] * 2
    for i in reversed(range(DEPTH)):
        j = i // 2
        blk = saved[i]
        dup, parts[("down", i)], g_ffn_post[i], g_ffn_cw[i], g_ffn_cb[i] = bwd(
            ffn_bwd1, f"ffn_bwd1_{i}", dh, blk["f"], blk["up"], row(ffn_post_g[i]), weights[("down", i)], CW_ffn[i],
            row(ffn_conv_b[i]))
        dh, parts[("up", i)], g_ffn_pre[i] = bwd(inproj_bwd, f"ffn_bwd2_{i}", blk["x_ffn"], row(ffn_pre_g[i]), dup,
                                                  weights[("up", i)], dh, 2 * FFN_F // N_DEV)
        if i % 2 == 0:
            dy, dz, parts[("ssd_out", j)], g_mix_post[i], g_ssd_nw[j] = bwd(
                ssd_out_bwd, f"ssd_out_bwd_{j}", dh, blk["m"], blk["y"], blk["z"], row(ssd_norm_w[j]),
                weights[("ssd_out", j)], row(mix_post_g[i]))
            dxh, dbm, dcm, ddt, g_ssd_dtb[j], g_ssd_alog[j], g_ssd_d[j] = bwd(
                ssd_scan_bwd, f"ssd_scan_bwd_{j}", dy, blk["xh"], blk["bm"], blk["cm"], blk["dt_raw"], blk["sprev"],
                row(ssd_dt_bias[j]), row(ssd_A_log[j]), row(ssd_D[j]))
            d_in, g_ssd_cw[j], g_ssd_cb[j] = bwd(ssd_conv_bwd, f"ssd_conv_bwd_{j}", dxh, dbm, dcm, blk["raw"], dz, ddt,
                                                 CW_ssd[j], row(ssd_conv_b[j]))
            dh, parts[("ssd_in", j)], g_mix_pre[i] = bwd(inproj_bwd, f"ssd_bwd2_{j}", blk["x_mix"], row(mix_pre_g[i]),
                                                          d_in, weights[("ssd_in", j)], dh, SSD_IN // N_DEV)
        else:
            dbcv, parts[("sc_out", j)], g_mix_post[i], g_sc_cw[j] = bwd(
                sc_bwd1, f"sc_bwd1_{j}", dh, blk["m"], blk["bcv"], row(mix_post_g[i]), weights[("sc_out", j)], CW_sc[j])
            dh, parts[("sc_in", j)], g_mix_pre[i] = bwd(inproj_bwd, f"sc_bwd2_{j}", blk["x_mix"], row(mix_pre_g[i]),
                                                         dbcv, weights[("sc_in", j)], dh, 3 * D_MODEL // N_DEV)
    grad_x = dh[None]

    st = lambda lst: jnp.concatenate(lst, axis=0)
    small_full = [
        st(g_mix_pre), st(g_mix_post), st(g_ffn_pre), st(g_ffn_post),
        jnp.stack(g_ssd_cw), st(g_ssd_cb), st(g_ssd_dtb), st(g_ssd_alog), st(g_ssd_d), st(g_ssd_nw),
        jnp.stack(g_sc_cw), jnp.stack(g_ffn_cw), st(g_ffn_cb),
    ]
    full_shapes = [a.shape for a in small_full]
    last_out = exchange([("a2a", parts[("ssd_in", 0)]), ("ag", _pack(small_full))], "a2a_last")
    recvd[("ssd_in", 0)] = last_out[0]

    def finish(n, nl, w, m, v):
        return reduce_adamw([recvd[(n, l)] for l in range(nl)], w, m, v, "adamw_" + n)

    r_ssd_in = finish("ssd_in", 2, ssd_w_in, m_ssd_w_in, v_ssd_w_in)
    r_ssd_out = finish("ssd_out", 2, ssd_w_out, m_ssd_w_out, v_ssd_w_out)
    r_sc_in = finish("sc_in", 2, sc_w_in, m_sc_w_in, v_sc_w_in)
    r_sc_out = finish("sc_out", 2, sc_w_out, m_sc_w_out, v_sc_w_out)
    r_up = finish("up", DEPTH, ffn_w_up, m_ffn_w_up, v_ffn_w_up)
    r_down = finish("down", DEPTH, ffn_w_down, m_ffn_w_down, v_ffn_w_down)

    summed = small_reduce(last_out[1], "small_reduce")
    (s_mix_pre, s_mix_post, s_ffn_pre, s_ffn_post, s_ssd_cw, s_ssd_cb, s_ssd_dtb, s_ssd_alog, s_ssd_d, s_ssd_nw,
     s_sc_cw, s_ffn_cw, s_ffn_cb) = _unpack(summed, full_shapes)

    def my_cols(a, width):
        return lax.dynamic_slice_in_dim(a, me * width, width, axis=a.ndim - 1)

    s_ssd_cw = my_cols(s_ssd_cw, SSD_CONV // N_DEV)
    s_sc_cw = my_cols(s_sc_cw, D_MODEL // N_DEV)
    s_ffn_cw = my_cols(s_ffn_cw, FFN_F // N_DEV)

    small_g = [s_mix_pre, s_mix_post, s_ffn_pre, s_ffn_post, s_ssd_cw, s_ssd_cb, s_ssd_dtb, s_ssd_alog, s_ssd_d,
               s_ssd_nw, s_sc_cw, s_ffn_cw, s_ffn_cb]
    small_w = [mix_pre_g, mix_post_g, ffn_pre_g, ffn_post_g, ssd_conv_w, ssd_conv_b, ssd_dt_bias, ssd_A_log, ssd_D,
               ssd_norm_w, sc_conv_w, ffn_conv_w, ffn_conv_b]
    small_m = [m_mix_pre_g, m_mix_post_g, m_ffn_pre_g, m_ffn_post_g, m_ssd_conv_w, m_ssd_conv_b, m_ssd_dt_bias,
               m_ssd_A_log, m_ssd_D, m_ssd_norm_w, m_sc_conv_w, m_ffn_conv_w, m_ffn_conv_b]
    small_v = [v_mix_pre_g, v_mix_post_g, v_ffn_pre_g, v_ffn_post_g, v_ssd_conv_w, v_ssd_conv_b, v_ssd_dt_bias,
               v_ssd_A_log, v_ssd_D, v_ssd_norm_w, v_sc_conv_w, v_ffn_conv_w, v_ffn_conv_b]
    local_shapes = [a.shape for a in small_w]
    pd, pm, pv = small_adamw(_pack(small_g), _pack(small_w), _pack(small_m), _pack(small_v), "small_adamw")
    sd = _unpack(pd, local_shapes)
    sm = _unpack(pm, local_shapes)
    sv = _unpack(pv, local_shapes)

    def ordered(small, big):
        (mix_pre, mix_post, ffn_pre, ffn_post, ssd_cw, ssd_cb, dtb, alog, dsk, nw, sc_cw, ffn_cw, ffn_cb) = small
        (b_ssd_in, b_ssd_out, b_sc_in, b_sc_out, b_up, b_down) = big
        return [mix_pre, mix_post, ffn_pre, ffn_post, b_ssd_in, ssd_cw, ssd_cb, dtb, alog, dsk, nw, b_ssd_out,
                b_sc_in, sc_cw, b_sc_out, b_up, ffn_cw, ffn_cb, b_down]

    bigs = [r_ssd_in, r_ssd_out, r_sc_in, r_sc_out, r_up, r_down]
    grads = ordered(small_g, [r[0] for r in bigs])
    deltas = ordered(sd, [r[1] for r in bigs])
    new_m = ordered(sm, [r[2] for r in bigs])
    new_v = ordered(sv, [r[3] for r in bigs])
    return (loss, grad_x, *grads, *deltas, *new_m, *new_v)
```

```python
import functools

import jax
import jax.numpy as jnp
from jax import lax
from jax.experimental import pallas as pl
from jax.experimental.pallas import tpu as pltpu

F32 = jnp.float32
BF16 = jnp.bfloat16

EPS = 1e-6
D_MODEL = 1024
DEPTH = 4
N_DEV = 8
CHUNK = 64
SSD_DI = 2048
SSD_H = 32
SSD_P = 64
SSD_G = 8
SSD_R = SSD_H // SSD_G
SSD_N = 128
SSD_CONV = SSD_DI + 2 * SSD_G * SSD_N
SSD_IN = SSD_DI + SSD_CONV + SSD_H
LANES = 128
SSD_IN_PAD = -(-SSD_IN // LANES) * LANES
SSD_KW = 4
SC_KW = 3
FFN_F = 2816
FFN_KW = 3
TL = 256
HALO = 8
VMEM_LIMIT = 60 * 1024 * 1024

ADAM_LR = 0.001
ADAM_B1 = 0.9
ADAM_B2 = 0.999
ADAM_EPS = 1e-08
ADAM_WD = 0.01
ADAM_STEP = 10

MESH = pl.DeviceIdType.MESH


def _rms(x, g):
    r = lax.rsqrt(jnp.mean(x * x, axis=-1, keepdims=True) + EPS)
    return x * r * g


def _rms_bwd(x, g, dy):
    r = lax.rsqrt(jnp.mean(x * x, axis=-1, keepdims=True) + EPS)
    xh = x * r
    dg = jnp.sum(dy * xh, axis=0, keepdims=True)
    dxh = dy * g
    dx = r * (dxh - xh * jnp.mean(dxh * xh, axis=-1, keepdims=True))
    return dx, dg


def _mm(a, b):
    return jnp.dot(a, b, preferred_element_type=F32)


def _mm_nt(a, b):
    return lax.dot_general(a, b, (((1,), (1,)), ((), ())), preferred_element_type=F32)


def _mm_tn(a, b):
    return lax.dot_general(a, b, (((0,), (0,)), ((), ())), preferred_element_type=F32)


def _silu_parts(x):
    sg = jax.nn.sigmoid(x)
    return x * sg, sg * (1.0 + x * (1.0 - sg))


def _taps(ext, kw, tl):
    base = HALO - (kw - 1)
    return [ext[base + j:base + j + tl] for j in range(kw)]


def _conv_fwd(taps, w):
    out = taps[0] * w[0:1]
    for j in range(1, len(taps)):
        out = out + taps[j] * w[j:j + 1]
    return out


def _conv_bwd_in(extd, w, kw, tl):
    out = extd[kw - 1:kw - 1 + tl] * w[0:1]
    for j in range(1, kw):
        out = out + extd[kw - 1 - j:kw - 1 - j + tl] * w[j:j + 1]
    return out


def _conv_bwd_w(taps, dy):
    return jnp.concatenate([jnp.sum(dy * t, axis=0, keepdims=True) for t in taps], axis=0)


def _emit_row_shards(acc_ref, out_ref, stage_ref):
    rows = out_ref.shape[1]
    for k in range(N_DEV):
        stage_ref[...] = acc_ref[k * rows:(k + 1) * rows, :].astype(BF16)
        pltpu.sync_copy(stage_ref, out_ref.at[k])


def _emit_col_shards(acc_ref, out_ref, stage_ref):
    cols = out_ref.shape[2]
    for k in range(N_DEV):
        stage_ref[...] = acc_ref[:, k * cols:(k + 1) * cols].astype(BF16)
        pltpu.sync_copy(stage_ref, out_ref.at[k])


def _res(shape):
    nd = len(shape)
    return pl.BlockSpec(shape, lambda i: (0,) * nd, pipeline_mode=pl.Buffered(1))


def _small(shape):
    nd = len(shape)
    return pl.BlockSpec(shape, lambda i: (0,) * nd)


def _tile(n):
    return pl.BlockSpec((TL, n), lambda i: (i, 0))


def _rtile(n, nt):
    return pl.BlockSpec((TL, n), lambda i: (nt - 1 - i, 0))


def _halo_before(n, nt, reverse):
    per = TL // HALO
    if reverse:
        return pl.BlockSpec((HALO, n), lambda i: (jnp.maximum((nt - 1 - i) * per - 1, 0), 0))
    return pl.BlockSpec((HALO, n), lambda i: (jnp.maximum(i * per - 1, 0), 0))


_ANY = pl.BlockSpec(memory_space=pl.ANY)


def _sds(shape, dtype=F32):
    return jax.ShapeDtypeStruct(shape, dtype)


def _peer(k):
    x, y, c = lax.axis_index("x"), lax.axis_index("y"), lax.axis_index("c")
    px = x ^ (k >> 2)
    py = y ^ ((k >> 1) & 1)
    pc = c ^ (k & 1)
    return (px, py, pc), 4 * px + 2 * py + pc


def _my_index():
    return 4 * lax.axis_index("x") + 2 * lax.axis_index("y") + lax.axis_index("c")


SIBLING = 1
SAME_CORE_CHIPS = (2, 4, 6)


def _job_copies(kind, src_ref, out_ref, send_sems, recv_sems, local_sems, j):
    me = _my_index()
    sends, recvs = [], []

    def pair(pattern, sem, src, put_slot, get_slot):
        dev, _ = _peer(pattern)
        sems = dict(send_sem=send_sems.at[j, sem], recv_sem=recv_sems.at[j, sem], device_id=dev, device_id_type=MESH)
        sends.append(pltpu.make_async_remote_copy(src_ref=src, dst_ref=out_ref.at[put_slot], **sems))
        recvs.append(pltpu.make_async_remote_copy(src_ref=src, dst_ref=out_ref.at[get_slot], **sems))

    if kind == "agB":
        for k in SAME_CORE_CHIPS:
            _, mine_from_k = _peer(k)
            _, sib_from_k = _peer(k | SIBLING)
            pair(SIBLING, k, out_ref.at[mine_from_k], mine_from_k, sib_from_k)
        return None, sends, recvs
    patterns = (SIBLING,) + SAME_CORE_CHIPS if kind == "agA" else range(1, N_DEV)
    mine = src_ref.at[me] if kind == "a2a" else src_ref
    local = pltpu.make_async_copy(mine, out_ref.at[me], local_sems.at[j])
    for k in patterns:
        _, idx = _peer(k)
        pair(k, k - 1, src_ref.at[idx] if kind == "a2a" else src_ref, me, idx)
    return local, sends, recvs


def _pcall(body, name, grid, in_specs, out_specs, out_shape, scratch_shapes, args, jobs=()):
    n_in, n_out, nj = len(in_specs), len(out_specs), len(jobs)
    last = grid[0] - 1
    kinds = [k for k, _ in jobs]

    def wrapped(*refs):
        ins = refs[:n_in]
        csrc = refs[n_in:n_in + nj]
        outs = refs[n_in + nj:n_in + nj + n_out]
        cout = refs[n_in + nj + n_out:n_in + 2 * nj + n_out]
        rest = refs[n_in + 2 * nj + n_out:]

        def copies(j, kind):
            return _job_copies(kind, csrc[j], cout[j], send_sems, recv_sems, local_sems, j)

        def start(j, kind):
            local, sends, _ = copies(j, kind)
            if local is not None:
                local.start()
            for cp in sends:
                cp.start()

        def finish(j, kind, arrivals_only=False):
            local, sends, recvs = copies(j, kind)
            for cp in recvs:
                cp.wait_recv()
            if not arrivals_only:
                for cp in sends:
                    cp.wait_send()
                if local is not None:
                    local.wait()

        if nj:
            scratch, (send_sems, recv_sems, local_sems) = rest[:-3], rest[-3:]
            i = pl.program_id(0)

            @pl.when(i == 0)
            def _():
                for j in range(nj):
                    start(j, "agA" if kinds[j] == "ag2" else kinds[j])
                for j in range(nj):
                    if kinds[j] == "ag2":
                        finish(j, "agA", arrivals_only=True)
                        start(j, "agB")
        else:
            scratch = rest
        body(*ins, *outs, *scratch)
        if nj:
            @pl.when(i == last)
            def _():
                for j in range(nj):
                    if kinds[j] == "ag2":
                        finish(j, "agB")
                        _, sends, _ = copies(j, "agA")
                        for cp in sends:
                            cp.wait_send()
                        copies(j, "agA")[0].wait()
                    else:
                        finish(j, kinds[j])

    job_shapes = []
    aliases = {}
    for j, (kind, s) in enumerate(jobs):
        shp = (N_DEV,) + tuple(s.shape) if kind in ("ag", "agA", "ag2") else tuple(s.shape)
        job_shapes.append(_sds(shp, s.dtype))
        if kind == "agB":
            aliases[n_in + j] = n_out + j
    sems = [pltpu.SemaphoreType.DMA((nj, N_DEV - 1)), pltpu.SemaphoreType.DMA((nj, N_DEV - 1)),
            pltpu.SemaphoreType.DMA((nj,))] if nj else []
    res = pl.pallas_call(
        wrapped, name=name, grid=grid,
        in_specs=list(in_specs) + [_ANY] * nj,
        out_specs=list(out_specs) + [_ANY] * nj,
        out_shape=list(out_shape) + job_shapes,
        scratch_shapes=list(scratch_shapes) + sems,
        input_output_aliases=aliases,
        compiler_params=pltpu.CompilerParams(dimension_semantics=("arbitrary",), vmem_limit_bytes=VMEM_LIMIT,
                                             has_side_effects=bool(nj)),
    )(*args, *[s for _, s in jobs])
    return list(res[:n_out]), list(res[n_out:])


def exchange(jobs, name):
    def body(o_ref):
        o_ref[...] = jnp.zeros_like(o_ref)

    _, outs = _pcall(body, name, (1,), [], [_small((8, LANES))], [_sds((8, LANES))], [], [], jobs)
    return outs


def ffn_fwd(x, g_pre, w_up, conv_w, conv_b, w_down, g_post, name, jobs=()):
    L = x.shape[0]
    nt = L // TL
    F = FFN_F

    def body(x_ref, gpre_ref, wup_ref, cw_ref, cb_ref, wdn_ref, gpost_ref, up_ref, f_ref, xn_ref, carry_ref):
        i = pl.program_id(0)

        @pl.when(i == 0)
        def _():
            carry_ref[...] = jnp.zeros_like(carry_ref)

        x = x_ref[...]
        h = _rms(x, gpre_ref[...]).astype(BF16)
        up = _mm(h, wup_ref[...])
        up_ref[...] = up
        ug = up[:, :F]
        val = up[:, F:]
        ext = jnp.concatenate([carry_ref[...], ug], axis=0)
        gate = _conv_fwd(_taps(ext, FFN_KW, TL), cw_ref[...]) + cb_ref[...]
        carry_ref[...] = ug[TL - HALO:, :]
        a = (gate * jax.nn.sigmoid(gate) * val).astype(BF16)
        f = _mm(a, wdn_ref[...])
        f_ref[...] = f
        xn_ref[...] = x + _rms(f, gpost_ref[...])

    return _pcall(body, name, (nt,),
                  [_tile(D_MODEL), _small((1, D_MODEL)), _res((D_MODEL, 2 * F)), _small((FFN_KW, F)), _small((1, F)),
                   _res((F, D_MODEL)), _small((1, D_MODEL))],
                  [_tile(2 * F), _tile(D_MODEL), _tile(D_MODEL)],
                  [_sds((L, 2 * F)), _sds((L, D_MODEL)), _sds((L, D_MODEL))],
                  [pltpu.VMEM((HALO, F), F32)],
                  [x, g_pre, w_up, conv_w, conv_b, w_down, g_post], jobs)


def ffn_bwd1(dxo, f, up, g_post, w_down, conv_w, conv_b, name, jobs=()):
    L = dxo.shape[0]
    nt = L // TL
    F = FFN_F
    rows = F // N_DEV

    def body(dxo_ref, f_ref, up_ref, halo_ref, gpost_ref, wdn_ref, cw_ref, cb_ref,
             dup_ref, dwdn_ref, dgp_ref, dcw_ref, dcb_ref, acc_ref, carry_ref, stage_ref):
        i = pl.program_id(0)
        t = nt - 1 - i

        @pl.when(i == 0)
        def _():
            acc_ref[...] = jnp.zeros_like(acc_ref)
            carry_ref[...] = jnp.zeros_like(carry_ref)
            dgp_ref[...] = jnp.zeros_like(dgp_ref)
            dcw_ref[...] = jnp.zeros_like(dcw_ref)
            dcb_ref[...] = jnp.zeros_like(dcb_ref)

        df, dgp = _rms_bwd(f_ref[...], gpost_ref[...], dxo_ref[...])
        dgp_ref[...] += dgp
        dfb = df.astype(BF16)
        da = _mm_nt(dfb, wdn_ref[...])
        up = up_ref[...]
        ug = up[:, :F]
        val = up[:, F:]
        halo = jnp.where(t == 0, 0.0, halo_ref[...])
        ext = jnp.concatenate([halo, ug], axis=0)
        w = cw_ref[...]
        taps = _taps(ext, FFN_KW, TL)
        gate = _conv_fwd(taps, w) + cb_ref[...]
        s, ds = _silu_parts(gate)
        acc_ref[...] += _mm_tn((s * val).astype(BF16), dfb)
        dval = da * s
        dgate = da * val * ds
        dcb_ref[...] += jnp.sum(dgate, axis=0, keepdims=True)
        dcw_ref[...] += _conv_bwd_w(taps, dgate)
        extd = jnp.concatenate([dgate, carry_ref[...]], axis=0)
        dug = _conv_bwd_in(extd, w, FFN_KW, TL)
        carry_ref[...] = dgate[:HALO, :]
        dup_ref[...] = jnp.concatenate([dug, dval], axis=1).astype(BF16)

        @pl.when(i == nt - 1)
        def _():
            _emit_row_shards(acc_ref, dwdn_ref, stage_ref)

    return _pcall(body, name, (nt,),
                  [_rtile(D_MODEL, nt), _rtile(D_MODEL, nt), _rtile(2 * F, nt), _halo_before(F, nt, True),
                   _small((1, D_MODEL)), _res((F, D_MODEL)), _small((FFN_KW, F)), _small((1, F))],
                  [_rtile(2 * F, nt), _ANY, _small((1, D_MODEL)), _small((FFN_KW, F)), _small((1, F))],
                  [_sds((L, 2 * F), BF16), _sds((N_DEV, rows, D_MODEL), BF16), _sds((1, D_MODEL)),
                   _sds((FFN_KW, F)), _sds((1, F))],
                  [pltpu.VMEM((F, D_MODEL), F32), pltpu.VMEM((HALO, F), F32), pltpu.VMEM((rows, D_MODEL), BF16)],
                  [dxo, f, up, up, g_post, w_down, conv_w, conv_b], jobs)


def inproj_bwd(x, g_pre, d, w, dxo, cols, name, jobs=()):
    L = x.shape[0]
    nt = L // TL
    N = w.shape[1]

    def body(x_ref, g_ref, d_ref, w_ref, dxo_ref, dx_ref, dw_ref, dg_ref, acc_ref, stage_ref):
        i = pl.program_id(0)

        @pl.when(i == 0)
        def _():
            acc_ref[...] = jnp.zeros_like(acc_ref)
            dg_ref[...] = jnp.zeros_like(dg_ref)

        x = x_ref[...]
        g = g_ref[...]
        d = d_ref[...]
        h = _rms(x, g).astype(BF16)
        dh = _mm_nt(d, w_ref[...])
        acc_ref[...] += _mm_tn(h, d)
        dxn, dg = _rms_bwd(x, g, dh)
        dx_ref[...] = dxo_ref[...] + dxn
        dg_ref[...] += dg

        @pl.when(i == nt - 1)
        def _():
            _emit_col_shards(acc_ref, dw_ref, stage_ref)

    return _pcall(body, name, (nt,),
                  [_tile(D_MODEL), _small((1, D_MODEL)), _tile(N), _res((D_MODEL, N)), _tile(D_MODEL)],
                  [_tile(D_MODEL), _ANY, _small((1, D_MODEL))],
                  [_sds((L, D_MODEL)), _sds((N_DEV, D_MODEL, cols), BF16), _sds((1, D_MODEL))],
                  [pltpu.VMEM((D_MODEL, N), F32), pltpu.VMEM((D_MODEL, cols), BF16)],
                  [x, g_pre, d, w, dxo], jobs)


def sc_fwd(x, g_pre, w_in, conv_w, w_out, g_post, name, jobs=()):
    L = x.shape[0]
    nt = L // TL
    W = D_MODEL

    def body(x_ref, gpre_ref, win_ref, cw_ref, wout_ref, gpost_ref, bcv_ref, m_ref, xn_ref, carry_ref):
        i = pl.program_id(0)

        @pl.when(i == 0)
        def _():
            carry_ref[...] = jnp.zeros_like(carry_ref)

        x = x_ref[...]
        h = _rms(x, gpre_ref[...]).astype(BF16)
        bcv = _mm(h, win_ref[...])
        bcv_ref[...] = bcv
        gb = bcv[:, :W]
        p = bcv[:, W:2 * W] * bcv[:, 2 * W:]
        ext = jnp.concatenate([carry_ref[...], p], axis=0)
        u = _conv_fwd(_taps(ext, SC_KW, TL), cw_ref[...])
        carry_ref[...] = p[TL - HALO:, :]
        m = _mm((gb * u).astype(BF16), wout_ref[...])
        m_ref[...] = m
        xn_ref[...] = x + _rms(m, gpost_ref[...])

    return _pcall(body, name, (nt,),
                  [_tile(W), _small((1, W)), _res((W, 3 * W)), _small((SC_KW, W)), _res((W, W)), _small((1, W))],
                  [_tile(3 * W), _tile(W), _tile(W)],
                  [_sds((L, 3 * W)), _sds((L, W)), _sds((L, W))],
                  [pltpu.VMEM((HALO, W), F32)],
                  [x, g_pre, w_in, conv_w, w_out, g_post], jobs)


def sc_bwd1(dxo, m, bcv, g_post, w_out, conv_w, name, jobs=()):
    L = dxo.shape[0]
    nt = L // TL
    W = D_MODEL
    rows = W // N_DEV

    def body(dxo_ref, m_ref, bcv_ref, halo_ref, gpost_ref, wout_ref, cw_ref,
             dbcv_ref, dwout_ref, dgp_ref, dcw_ref, acc_ref, carry_ref, stage_ref):
        i = pl.program_id(0)
        t = nt - 1 - i

        @pl.when(i == 0)
        def _():
            acc_ref[...] = jnp.zeros_like(acc_ref)
            carry_ref[...] = jnp.zeros_like(carry_ref)
            dgp_ref[...] = jnp.zeros_like(dgp_ref)
            dcw_ref[...] = jnp.zeros_like(dcw_ref)

        dm, dgp = _rms_bwd(m_ref[...], gpost_ref[...], dxo_ref[...])
        dgp_ref[...] += dgp
        dmb = dm.astype(BF16)
        dq = _mm_nt(dmb, wout_ref[...])
        bcv = bcv_ref[...]
        gb = bcv[:, :W]
        gc = bcv[:, W:2 * W]
        v = bcv[:, 2 * W:]
        hb = halo_ref[...]
        halo = jnp.where(t == 0, 0.0, hb[:, W:2 * W] * hb[:, 2 * W:])
        ext = jnp.concatenate([halo, gc * v], axis=0)
        w = cw_ref[...]
        taps = _taps(ext, SC_KW, TL)
        u = _conv_fwd(taps, w)
        acc_ref[...] += _mm_tn((gb * u).astype(BF16), dmb)
        dgb = dq * u
        du = dq * gb
        dcw_ref[...] += _conv_bwd_w(taps, du)
        extd = jnp.concatenate([du, carry_ref[...]], axis=0)
        dp = _conv_bwd_in(extd, w, SC_KW, TL)
        carry_ref[...] = du[:HALO, :]
        dbcv_ref[...] = jnp.concatenate([dgb, dp * v, dp * gc], axis=1).astype(BF16)

        @pl.when(i == nt - 1)
        def _():
            _emit_row_shards(acc_ref, dwout_ref, stage_ref)

    return _pcall(body, name, (nt,),
                  [_rtile(W, nt), _rtile(W, nt), _rtile(3 * W, nt), _halo_before(3 * W, nt, True),
                   _small((1, W)), _res((W, W)), _small((SC_KW, W))],
                  [_rtile(3 * W, nt), _ANY, _small((1, W)), _small((SC_KW, W))],
                  [_sds((L, 3 * W), BF16), _sds((N_DEV, rows, W), BF16), _sds((1, W)), _sds((SC_KW, W))],
                  [pltpu.VMEM((W, W), F32), pltpu.VMEM((HALO, W), F32), pltpu.VMEM((rows, W), BF16)],
                  [dxo, m, bcv, bcv, g_post, w_out, conv_w], jobs)


def ssd_inproj(x, g_pre, w_in, conv_w, conv_b, name, jobs=()):
    L = x.shape[0]
    nt = L // TL

    def body(x_ref, gpre_ref, win_ref, cw_ref, cb_ref, z_ref, raw_ref, dt_ref, xh_ref, bm_ref, cm_ref, carry_ref):
        i = pl.program_id(0)

        @pl.when(i == 0)
        def _():
            carry_ref[...] = jnp.zeros_like(carry_ref)

        h = _rms(x_ref[...], gpre_ref[...]).astype(BF16)
        zx = _mm(h, win_ref[...])
        z_ref[...] = zx[:, :SSD_DI]
        raw = zx[:, SSD_DI:SSD_DI + SSD_CONV]
        raw_ref[...] = raw
        dt_ref[...] = zx[:, SSD_DI + SSD_CONV:SSD_IN]
        ext = jnp.concatenate([carry_ref[...], raw], axis=0)
        pre = _conv_fwd(_taps(ext, SSD_KW, TL), cw_ref[...]) + cb_ref[...]
        carry_ref[...] = raw[TL - HALO:, :]
        act = pre * jax.nn.sigmoid(pre)
        for hh in range(SSD_H):
            xh_ref[hh] = act[:, hh * SSD_P:(hh + 1) * SSD_P]
        for g in range(SSD_G):
            bm_ref[g] = act[:, SSD_DI + g * SSD_N:SSD_DI + (g + 1) * SSD_N]
            cm_ref[g] = act[:, SSD_DI + (SSD_G + g) * SSD_N:SSD_DI + (SSD_G + g + 1) * SSD_N]

    return _pcall(body, name, (nt,),
                  [_tile(D_MODEL), _small((1, D_MODEL)), _res((D_MODEL, SSD_IN_PAD)), _small((SSD_KW, SSD_CONV)),
                   _small((1, SSD_CONV))],
                  [_tile(SSD_DI), _tile(SSD_CONV), _tile(SSD_H),
                   pl.BlockSpec((SSD_H, TL, SSD_P), lambda i: (0, i, 0)),
                   pl.BlockSpec((SSD_G, TL, SSD_N), lambda i: (0, i, 0)),
                   pl.BlockSpec((SSD_G, TL, SSD_N), lambda i: (0, i, 0))],
                  [_sds((L, SSD_DI)), _sds((L, SSD_CONV)), _sds((L, SSD_H)), _sds((SSD_H, L, SSD_P)),
                   _sds((SSD_G, L, SSD_N)), _sds((SSD_G, L, SSD_N))],
                  [pltpu.VMEM((HALO, SSD_CONV), F32)],
                  [x, g_pre, w_in, conv_w, conv_b], jobs)


def _per_head(v, rows):
    return jnp.stack([v[:, h:h + 1] for h in range(SSD_H)], axis=0)


def _heads_to_lanes(v):
    return jnp.concatenate([v[h] for h in range(SSD_H)], axis=1)


def _rep_heads(v):
    g, a, b = v.shape
    return jnp.broadcast_to(v[:, None], (g, SSD_R, a, b)).reshape(g * SSD_R, a, b)


def _sum_heads(v):
    h, a, b = v.shape
    return v.reshape(SSD_G, SSD_R, a, b).sum(axis=1)


def _ssd_decays(dtr, bias, a_log):
    T = CHUNK
    dt = jax.nn.softplus(dtr + bias)
    a_head = -jnp.exp(a_log)
    a = dt * a_head
    ii = lax.broadcasted_iota(jnp.int32, (T, T), 0)
    jj = lax.broadcasted_iota(jnp.int32, (T, T), 1)
    tri = ii >= jj
    cs = jnp.dot(tri.astype(F32), a, precision=lax.Precision.HIGHEST, preferred_element_type=F32)
    cs_t = cs.T
    csc = _per_head(cs, T)
    csr = jnp.stack([cs_t[h:h + 1, :] for h in range(SSD_H)], axis=0)
    dtc = _per_head(dt, T)
    cl = _per_head(cs[T - 1:T, :], 1)
    lmat = jnp.exp(jnp.where(tri[None], csc - csr, -jnp.inf))
    return dict(dt=dt, a_head=a_head, tri=tri, csc=csc, dtc=dtc, lmat=lmat,
                ecs=jnp.exp(csc), dsc=jnp.exp(cl - csc), cdc=jnp.exp(cl))


def ssd_scan_fwd(xh, bm, cm, dt_raw, dt_bias, a_log, d_skip, name, jobs=()):
    L = xh.shape[1]
    nc = L // CHUNK
    T = CHUNK

    def body(xh_ref, bm_ref, cm_ref, dt_ref, bias_ref, alog_ref, dsk_ref, y_ref, sp_ref, st_ref):
        c = pl.program_id(0)

        @pl.when(c == 0)
        def _():
            st_ref[...] = jnp.zeros_like(st_ref)

        dec = _ssd_decays(dt_ref[...], bias_ref[...], alog_ref[...])
        x = xh_ref[...]
        bgb = bm_ref[...].astype(BF16)
        cgb = cm_ref[...].astype(BF16)
        bh = _rep_heads(bgb)
        ch = _rep_heads(cgb)
        dh = _per_head(dsk_ref[...], 1)
        xt = x * dec["dtc"]
        cb = jnp.einsum("gln,gsn->gls", cgb, bgb, preferred_element_type=F32)
        mb = (_rep_heads(cb) * dec["lmat"]).astype(BF16)
        yd = jnp.einsum("hls,hsp->hlp", mb, xt.astype(BF16), preferred_element_type=F32)
        s = st_ref[...]
        sb = s.astype(BF16)
        yo = jnp.einsum("hln,hpn->hlp", ch, sb, preferred_element_type=F32) * dec["ecs"]
        y_ref[...] = yd + yo + x * dh
        sp_ref[0] = sb
        xd = (xt * dec["dsc"]).astype(BF16)
        st_ref[...] = s * dec["cdc"] + jnp.einsum("htp,htn->hpn", xd, bh, preferred_element_type=F32)

    hd = pl.BlockSpec((SSD_H, T, SSD_P), lambda c: (0, c, 0))
    gr = pl.BlockSpec((SSD_G, T, SSD_N), lambda c: (0, c, 0))
    return _pcall(body, name, (nc,),
                  [hd, gr, gr, pl.BlockSpec((T, SSD_H), lambda c: (c, 0)),
                   _small((1, SSD_H)), _small((1, SSD_H)), _small((1, SSD_H))],
                  [hd, pl.BlockSpec((1, SSD_H, SSD_P, SSD_N), lambda c: (c, 0, 0, 0))],
                  [_sds((SSD_H, L, SSD_P)), _sds((nc, SSD_H, SSD_P, SSD_N), BF16)],
                  [pltpu.VMEM((SSD_H, SSD_P, SSD_N), F32)],
                  [xh, bm, cm, dt_raw, dt_bias, a_log, d_skip], jobs)


def ssd_scan_bwd(dy, xh, bm, cm, dt_raw, sprev, dt_bias, a_log, d_skip, name, jobs=()):
    L = xh.shape[1]
    nc = L // CHUNK
    T = CHUNK

    def body(dy_ref, xh_ref, bm_ref, cm_ref, dt_ref, sp_ref, bias_ref, alog_ref, dsk_ref,
             dxh_ref, dbm_ref, dcm_ref, ddt_ref, dbias_ref, dalog_ref, ddsk_ref, g_ref):
        i = pl.program_id(0)

        @pl.when(i == 0)
        def _():
            g_ref[...] = jnp.zeros_like(g_ref)
            dbias_ref[...] = jnp.zeros_like(dbias_ref)
            dalog_ref[...] = jnp.zeros_like(dalog_ref)
            ddsk_ref[...] = jnp.zeros_like(ddsk_ref)

        dtr = dt_ref[...]
        bias = bias_ref[...]
        dec = _ssd_decays(dtr, bias, alog_ref[...])
        dt, a_head, tri = dec["dt"], dec["a_head"], dec["tri"]
        dtc, lmat, ecs, dsc, cdc = dec["dtc"], dec["lmat"], dec["ecs"], dec["dsc"], dec["cdc"]
        x = xh_ref[...]
        dyv = dy_ref[...]
        dyb = dyv.astype(BF16)
        bgb = bm_ref[...].astype(BF16)
        cgb = cm_ref[...].astype(BF16)
        bh = _rep_heads(bgb)
        ch = _rep_heads(cgb)
        sb = sp_ref[0]
        g = g_ref[...]
        gb = g.astype(BF16)
        dh = _per_head(dsk_ref[...], 1)
        xt = x * dtc
        xtb = xt.astype(BF16)
        cb = jnp.einsum("gln,gsn->gls", cgb, bgb, preferred_element_type=F32)
        mf = _rep_heads(cb) * lmat
        mb = mf.astype(BF16)
        ddsk = jnp.sum(dyv * x, axis=(1, 2), keepdims=True)
        dx = dyv * dh
        yo_raw = jnp.einsum("hln,hpn->hlp", ch, sb, preferred_element_type=F32)
        w1 = dyv * ecs
        w1b = w1.astype(BF16)
        ds_off = jnp.einsum("hlp,hln->hpn", w1b, ch, preferred_element_type=F32)
        dch = jnp.einsum("hlp,hpn->hln", w1b, sb, preferred_element_type=F32)
        dcs_c = jnp.sum(w1 * yo_raw, axis=2, keepdims=True)
        dm = jnp.einsum("hlp,hsp->hls", dyb, xtb, preferred_element_type=F32)
        dxt = jnp.einsum("hls,hlp->hsp", mb, dyb, preferred_element_type=F32)
        dcbb = _sum_heads(dm * lmat).astype(BF16)
        dseg = dm * mf
        dcs_c = dcs_c + jnp.sum(dseg, axis=2, keepdims=True)
        dcs_r = -jnp.sum(dseg, axis=1, keepdims=True)
        dc = jnp.einsum("gls,gsn->gln", dcbb, bgb, preferred_element_type=F32) + _sum_heads(dch)
        db = jnp.einsum("gls,gln->gsn", dcbb, cgb, preferred_element_type=F32)
        xd = xt * dsc
        dxd = jnp.einsum("htn,hpn->htp", bh, gb, preferred_element_type=F32)
        db = db + _sum_heads(jnp.einsum("htp,hpn->htn", xd.astype(BF16), gb, preferred_element_type=F32))
        dxt = dxt + dxd * dsc
        d_ds = jnp.sum(dxd * xt, axis=2, keepdims=True)
        d_cd = jnp.sum(g * sb.astype(F32), axis=(1, 2), keepdims=True)
        g_ref[...] = g * cdc + ds_off
        t1 = d_ds * dsc
        dcs_c = dcs_c - t1
        dcl = jnp.sum(t1, axis=1, keepdims=True) + d_cd * cdc
        ddt_c = jnp.sum(dxt * x, axis=2, keepdims=True)
        dxh_ref[...] = dx + dxt * dtc
        dbm_ref[...] = db
        dcm_ref[...] = dc
        rows_t = jnp.concatenate([dcs_r[h] for h in range(SSD_H)], axis=0).T
        last = (lax.broadcasted_iota(jnp.int32, (T, 1), 0) == T - 1).astype(F32)
        dcs = _heads_to_lanes(dcs_c) + rows_t + last * _heads_to_lanes(dcl)
        da = lax.dot_general(tri.astype(F32), dcs, (((0,), (0,)), ((), ())),
                             precision=lax.Precision.HIGHEST, preferred_element_type=F32)
        ddt = da * a_head + _heads_to_lanes(ddt_c)
        dalog_ref[...] += jnp.sum(da * dt, axis=0, keepdims=True)
        ddtr = ddt * jax.nn.sigmoid(dtr + bias)
        ddt_ref[...] = ddtr
        dbias_ref[...] += jnp.sum(ddtr, axis=0, keepdims=True)
        ddsk_ref[...] += _heads_to_lanes(ddsk)

        @pl.when(i == nc - 1)
        def _():
            dalog_ref[...] = dalog_ref[...] * a_head

    hd = pl.BlockSpec((SSD_H, T, SSD_P), lambda i: (0, nc - 1 - i, 0))
    gr = pl.BlockSpec((SSD_G, T, SSD_N), lambda i: (0, nc - 1 - i, 0))
    tk = pl.BlockSpec((T, SSD_H), lambda i: (nc - 1 - i, 0))
    return _pcall(body, name, (nc,),
                  [hd, hd, gr, gr, tk, pl.BlockSpec((1, SSD_H, SSD_P, SSD_N), lambda i: (nc - 1 - i, 0, 0, 0)),
                   _small((1, SSD_H)), _small((1, SSD_H)), _small((1, SSD_H))],
                  [hd, gr, gr, tk, _small((1, SSD_H)), _small((1, SSD_H)), _small((1, SSD_H))],
                  [_sds((SSD_H, L, SSD_P)), _sds((SSD_G, L, SSD_N)), _sds((SSD_G, L, SSD_N)), _sds((L, SSD_H)),
                   _sds((1, SSD_H)), _sds((1, SSD_H)), _sds((1, SSD_H))],
                  [pltpu.VMEM((SSD_H, SSD_P, SSD_N), F32)],
                  [dy, xh, bm, cm, dt_raw, sprev, dt_bias, a_log, d_skip], jobs)


def _heads_to_tokens(y_ref):
    return jnp.concatenate([y_ref[h] for h in range(SSD_H)], axis=1)


def ssd_out_fwd(x, y, z, norm_w, w_out, g_post, name, jobs=()):
    L = x.shape[0]
    nt = L // TL

    def body(x_ref, y_ref, z_ref, nw_ref, wout_ref, gpost_ref, m_ref, xn_ref):
        z = z_ref[...]
        yg = _heads_to_tokens(y_ref) * (z * jax.nn.sigmoid(z))
        yn = _rms(yg, nw_ref[...]).astype(BF16)
        m = _mm(yn, wout_ref[...])
        m_ref[...] = m
        xn_ref[...] = x_ref[...] + _rms(m, gpost_ref[...])

    return _pcall(body, name, (nt,),
                  [_tile(D_MODEL), pl.BlockSpec((SSD_H, TL, SSD_P), lambda i: (0, i, 0)), _tile(SSD_DI),
                   _small((1, SSD_DI)), _res((SSD_DI, D_MODEL)), _small((1, D_MODEL))],
                  [_tile(D_MODEL), _tile(D_MODEL)],
                  [_sds((L, D_MODEL)), _sds((L, D_MODEL))],
                  [],
                  [x, y, z, norm_w, w_out, g_post], jobs)


def ssd_out_bwd(dxo, m, y, z, norm_w, w_out, g_post, name, jobs=()):
    L = dxo.shape[0]
    nt = L // TL
    rows = SSD_DI // N_DEV

    def body(dxo_ref, m_ref, y_ref, z_ref, nw_ref, wout_ref, gpost_ref,
             dy_ref, dz_ref, dwout_ref, dgp_ref, dnw_ref, acc_ref, stage_ref):
        i = pl.program_id(0)

        @pl.when(i == 0)
        def _():
            acc_ref[...] = jnp.zeros_like(acc_ref)
            dgp_ref[...] = jnp.zeros_like(dgp_ref)
            dnw_ref[...] = jnp.zeros_like(dnw_ref)

        dm, dgp = _rms_bwd(m_ref[...], gpost_ref[...], dxo_ref[...])
        dgp_ref[...] += dgp
        dmb = dm.astype(BF16)
        dyn = _mm_nt(dmb, wout_ref[...])
        z = z_ref[...]
        y = _heads_to_tokens(y_ref)
        sil, dsil = _silu_parts(z)
        yg = y * sil
        nw = nw_ref[...]
        acc_ref[...] += _mm_tn(_rms(yg, nw).astype(BF16), dmb)
        dyg, dnw = _rms_bwd(yg, nw, dyn)
        dnw_ref[...] += dnw
        dyv = dyg * sil
        dz_ref[...] = dyg * y * dsil
        for h in range(SSD_H):
            dy_ref[h] = dyv[:, h * SSD_P:(h + 1) * SSD_P]

        @pl.when(i == nt - 1)
        def _():
            _emit_row_shards(acc_ref, dwout_ref, stage_ref)

    hd = pl.BlockSpec((SSD_H, TL, SSD_P), lambda i: (0, i, 0))
    return _pcall(body, name, (nt,),
                  [_tile(D_MODEL), _tile(D_MODEL), hd, _tile(SSD_DI), _small((1, SSD_DI)), _res((SSD_DI, D_MODEL)),
                   _small((1, D_MODEL))],
                  [hd, _tile(SSD_DI), _ANY, _small((1, D_MODEL)), _small((1, SSD_DI))],
                  [_sds((SSD_H, L, SSD_P)), _sds((L, SSD_DI)), _sds((N_DEV, rows, D_MODEL), BF16),
                   _sds((1, D_MODEL)), _sds((1, SSD_DI))],
                  [pltpu.VMEM((SSD_DI, D_MODEL), F32), pltpu.VMEM((rows, D_MODEL), BF16)],
                  [dxo, m, y, z, norm_w, w_out, g_post], jobs)


def ssd_conv_bwd(dxh, dbm, dcm, xbc_raw, dz, ddt_raw, conv_w, conv_b, name, jobs=()):
    L = xbc_raw.shape[0]
    nt = L // TL

    def body(dxh_ref, dbm_ref, dcm_ref, raw_ref, halo_ref, dz_ref, ddt_ref, cw_ref, cb_ref,
             d_ref, dcw_ref, dcb_ref, carry_ref):
        i = pl.program_id(0)
        t = nt - 1 - i

        @pl.when(i == 0)
        def _():
            carry_ref[...] = jnp.zeros_like(carry_ref)
            dcw_ref[...] = jnp.zeros_like(dcw_ref)
            dcb_ref[...] = jnp.zeros_like(dcb_ref)

        dact = jnp.concatenate([dxh_ref[h] for h in range(SSD_H)] + [dbm_ref[g] for g in range(SSD_G)]
                               + [dcm_ref[g] for g in range(SSD_G)], axis=1)
        halo = jnp.where(t == 0, 0.0, halo_ref[...])
        ext = jnp.concatenate([halo, raw_ref[...]], axis=0)
        w = cw_ref[...]
        taps = _taps(ext, SSD_KW, TL)
        pre = _conv_fwd(taps, w) + cb_ref[...]
        _, dsil = _silu_parts(pre)
        dpre = dact * dsil
        dcb_ref[...] += jnp.sum(dpre, axis=0, keepdims=True)
        dcw_ref[...] += _conv_bwd_w(taps, dpre)
        extd = jnp.concatenate([dpre, carry_ref[...]], axis=0)
        draw = _conv_bwd_in(extd, w, SSD_KW, TL)
        carry_ref[...] = dpre[:HALO, :]
        d_ref[:, :SSD_DI] = dz_ref[...].astype(BF16)
        d_ref[:, SSD_DI:SSD_DI + SSD_CONV] = draw.astype(BF16)
        tail = jnp.concatenate([ddt_ref[...], jnp.zeros((TL, SSD_IN_PAD - SSD_IN), F32)], axis=1)
        d_ref[:, SSD_DI + SSD_CONV:] = tail.astype(BF16)

    hd = pl.BlockSpec((SSD_H, TL, SSD_P), lambda i: (0, nt - 1 - i, 0))
    gr = pl.BlockSpec((SSD_G, TL, SSD_N), lambda i: (0, nt - 1 - i, 0))
    return _pcall(body, name, (nt,),
                  [hd, gr, gr, _rtile(SSD_CONV, nt), _halo_before(SSD_CONV, nt, True), _rtile(SSD_DI, nt),
                   _rtile(SSD_H, nt), _small((SSD_KW, SSD_CONV)), _small((1, SSD_CONV))],
                  [_rtile(SSD_IN_PAD, nt), _small((SSD_KW, SSD_CONV)), _small((1, SSD_CONV))],
                  [_sds((L, SSD_IN_PAD), BF16), _sds((SSD_KW, SSD_CONV)), _sds((1, SSD_CONV))],
                  [pltpu.VMEM((HALO, SSD_CONV), F32)],
                  [dxh, dbm, dcm, xbc_raw, xbc_raw, dz, ddt_raw, conv_w, conv_b], jobs)


def loss_fwd_bwd(y, target, name, jobs=()):
    L = y.shape[0]
    nt = L // TL

    def body(y_ref, t_ref, loss_ref, dy_ref):
        i = pl.program_id(0)

        @pl.when(i == 0)
        def _():
            loss_ref[...] = jnp.zeros_like(loss_ref)

        err = y_ref[...] - t_ref[...]
        dy_ref[...] = err * (1.0 / D_MODEL)
        loss_ref[...] += 0.5 * jnp.sum(jnp.mean(err * err, axis=-1, keepdims=True), axis=0, keepdims=True)

    return _pcall(body, name, (nt,),
                  [_tile(D_MODEL), _tile(D_MODEL)],
                  [_small((1, 1)), _tile(D_MODEL)],
                  [_sds((1, 1)), _sds((L, D_MODEL))],
                  [],
                  [y, target], jobs)


def _adamw_math(w, g, m, v):
    m = ADAM_B1 * m + (1.0 - ADAM_B1) * g
    v = ADAM_B2 * v + (1.0 - ADAM_B2) * (g * g)
    m_hat = m / (1.0 - ADAM_B1 ** ADAM_STEP)
    v_hat = v / (1.0 - ADAM_B2 ** ADAM_STEP)
    delta = -ADAM_LR * (m_hat / (jnp.sqrt(v_hat) + ADAM_EPS) + ADAM_WD * w)
    return delta, m, v


def _row_tile(rows):
    for cand in (256, 176, 128, 64, 32, 16, 8):
        if rows % cand == 0:
            return cand
    return rows


def reduce_adamw(recvs, w, m, v, name):
    nl = len(recvs)
    _, R, C = recvs[0].shape
    tr = _row_tile(R)

    def body(*refs):
        r_refs = refs[:nl]
        w_ref, m_ref, v_ref, g_out, d_out, m_out, v_out = refs[nl:]
        layer = pl.program_id(0)
        for ll in range(nl):
            @pl.when(layer == ll)
            def _(ll=ll):
                g = r_refs[ll][0].astype(F32)
                for j in range(1, N_DEV):
                    g = g + r_refs[ll][j].astype(F32)
                delta, mn, vn = _adamw_math(w_ref[0], g, m_ref[0], v_ref[0])
                g_out[0] = g
                d_out[0] = delta
                m_out[0] = mn
                v_out[0] = vn

    def recv_spec(ll):
        return pl.BlockSpec((N_DEV, tr, C), lambda l, r: (0, jnp.where(l == ll, r, 0), 0))

    blk = pl.BlockSpec((1, tr, C), lambda l, r: (l, r, 0))
    return pl.pallas_call(
        body, name=name, grid=(nl, R // tr),
        in_specs=[recv_spec(ll) for ll in range(nl)] + [blk, blk, blk],
        out_specs=[blk] * 4,
        out_shape=[_sds((nl, R, C))] * 4,
        compiler_params=pltpu.CompilerParams(dimension_semantics=("arbitrary", "arbitrary"),
                                             vmem_limit_bytes=VMEM_LIMIT),
    )(*recvs, w, m, v)


def small_reduce(gathered, name):
    _, R, C = gathered.shape

    def body(r_ref, o_ref):
        g = r_ref[0]
        for j in range(1, N_DEV):
            g = g + r_ref[j]
        o_ref[...] = g

    return pl.pallas_call(body, name=name, out_shape=_sds((R, C)))(gathered)


def small_adamw(g, w, m, v, name):
    def body(g_ref, w_ref, m_ref, v_ref, d_out, m_out, v_out):
        delta, mn, vn = _adamw_math(w_ref[...], g_ref[...], m_ref[...], v_ref[...])
        d_out[...] = delta
        m_out[...] = mn
        v_out[...] = vn

    return pl.pallas_call(body, name=name, out_shape=[_sds(g.shape)] * 3)(g, w, m, v)


def _pack(arrs):
    flat = jnp.concatenate([a.reshape(-1) for a in arrs])
    n = flat.shape[0]
    rows = -(-n // (8 * LANES)) * 8
    flat = jnp.pad(flat, (0, rows * LANES - n))
    return flat.reshape(rows, LANES)


def _unpack(packed, shapes):
    flat = packed.reshape(-1)
    out = []
    off = 0
    for s in shapes:
        n = 1
        for d in s:
            n *= d
        out.append(flat[off:off + n].reshape(s))
        off += n
    return out


def kernel(x, mix_pre_g, mix_post_g, ffn_pre_g, ffn_post_g, ssd_w_in, ssd_conv_w, ssd_conv_b, ssd_dt_bias, ssd_A_log, ssd_D, ssd_norm_w, ssd_w_out, sc_w_in, sc_conv_w, sc_w_out, ffn_w_up, ffn_conv_w, ffn_conv_b, ffn_w_down, loss_target, m_mix_pre_g, m_mix_post_g, m_ffn_pre_g, m_ffn_post_g, m_ssd_w_in, m_ssd_conv_w, m_ssd_conv_b, m_ssd_dt_bias, m_ssd_A_log, m_ssd_D, m_ssd_norm_w, m_ssd_w_out, m_sc_w_in, m_sc_conv_w, m_sc_w_out, m_ffn_w_up, m_ffn_conv_w, m_ffn_conv_b, m_ffn_w_down, v_mix_pre_g, v_mix_post_g, v_ffn_pre_g, v_ffn_post_g, v_ssd_w_in, v_ssd_conv_w, v_ssd_conv_b, v_ssd_dt_bias, v_ssd_A_log, v_ssd_D, v_ssd_norm_w, v_ssd_w_out, v_sc_w_in, v_sc_conv_w, v_sc_w_out, v_ffn_w_up, v_ffn_conv_w, v_ffn_conv_b, v_ffn_w_down):
    me = _my_index()
    x0 = x[0]
    target = loss_target[0]
    row = lambda a: a.reshape(1, -1)

    shards = {"ssd_in": ssd_w_in, "ssd_out": ssd_w_out, "sc_in": sc_w_in, "sc_out": sc_w_out,
              "up": ffn_w_up, "down": ffn_w_down}
    col_sharded = {"ssd_in": SSD_IN_PAD, "sc_in": None, "up": None}
    weights = {}

    def shard_bf16(key):
        n, l = key
        return shards[n][l].astype(BF16)

    def store_weights(keys, outs):
        for (n, l), g in zip(keys, outs):
            _, R, C = g.shape
            if n in col_sharded:
                full = jnp.transpose(g, (1, 0, 2)).reshape(R, N_DEV * C)
                if col_sharded[n] is not None:
                    full = jnp.pad(full, ((0, 0), (0, col_sharded[n] - N_DEV * C)))
            else:
                full = g.reshape(N_DEV * R, C)
            weights[(n, l)] = full

    fwd_first_half = {
        "ssd_inproj_0": [("up", 0), ("down", 0)],
        "ssd_scan_fwd_0": [("sc_in", 0), ("sc_out", 0), ("up", 1)],
        "ssd_out_fwd_0": [("down", 1)],
        "ffn_fwd_0": [("ssd_in", 1)],
        "sc_fwd_0": [("ssd_out", 1), ("down", 2)],
        "ffn_fwd_1": [("up", 2)],
        "ssd_inproj_1": [("sc_in", 1), ("sc_out", 1), ("down", 3)],
        "ssd_scan_fwd_1": [("up", 3)],
    }
    bwd_sched = {
        "ffn_bwd2_3": [("down", 3)], "sc_bwd2_1": [("sc_out", 1)], "ffn_bwd1_2": [("up", 3)], "ffn_bwd2_2": [("sc_in", 1)],
        "ssd_out_bwd_1": [("down", 2)], "ssd_scan_bwd_1": [("up", 2), ("ssd_out", 1)], "ffn_bwd1_1": [("ssd_in", 1)],
        "ffn_bwd2_1": [("down", 1)], "sc_bwd2_0": [("sc_out", 0)], "ffn_bwd1_0": [("up", 1)], "ffn_bwd2_0": [("sc_in", 0)],
        "ssd_out_bwd_0": [("down", 0)], "ssd_scan_bwd_0": [("up", 0), ("ssd_out", 0)],
    }

    first = [("ssd_in", 0), ("ssd_out", 0)]
    outs = exchange([("ag2", shard_bf16(k)) for k in first]
                    + [("ag", ssd_conv_w), ("ag", sc_conv_w), ("ag", ffn_conv_w)], "ag_first")
    store_weights(first, outs[:2])

    def taps(g):
        _, nl, K, C = g.shape
        return jnp.transpose(g, (1, 2, 0, 3)).reshape(nl, K, N_DEV * C)

    CW_ssd, CW_sc, CW_ffn = taps(outs[2]), taps(outs[3]), taps(outs[4])

    half_done = []

    def fwd(fn, name, *args):
        second = list(half_done)
        starting = fwd_first_half.get(name, [])
        jobs = [("agB", buf) for _, buf in second] + [("agA", shard_bf16(k)) for k in starting]
        res, got = fn(*args, name, jobs)
        store_weights([k for k, _ in second], got[:len(second)])
        half_done[:] = list(zip(starting, got[len(second):]))
        return res

    saved = []
    h = x0
    for i in range(DEPTH):
        j = i // 2
        blk = dict(x_mix=h)
        if i % 2 == 0:
            z, raw, dt_raw, xh, bm, cm = fwd(ssd_inproj, f"ssd_inproj_{j}", h, row(mix_pre_g[i]), weights[("ssd_in", j)],
                                             CW_ssd[j], row(ssd_conv_b[j]))
            y, sprev = fwd(ssd_scan_fwd, f"ssd_scan_fwd_{j}", xh, bm, cm, dt_raw, row(ssd_dt_bias[j]),
                           row(ssd_A_log[j]), row(ssd_D[j]))
            m, h = fwd(ssd_out_fwd, f"ssd_out_fwd_{j}", h, y, z, row(ssd_norm_w[j]), weights[("ssd_out", j)],
                       row(mix_post_g[i]))
            blk.update(z=z, raw=raw, dt_raw=dt_raw, xh=xh, bm=bm, cm=cm, y=y, sprev=sprev, m=m)
        else:
            bcv, m, h = fwd(sc_fwd, f"sc_fwd_{j}", h, row(mix_pre_g[i]), weights[("sc_in", j)], CW_sc[j],
                            weights[("sc_out", j)], row(mix_post_g[i]))
            blk.update(bcv=bcv, m=m)
        blk["x_ffn"] = h
        up, f, h = fwd(ffn_fwd, f"ffn_fwd_{i}", h, row(ffn_pre_g[i]), weights[("up", i)], CW_ffn[i],
                       row(ffn_conv_b[i]), weights[("down", i)], row(ffn_post_g[i]))
        blk.update(up=up, f=f)
        saved.append(blk)

    (loss_dev, dh), _ = loss_fwd_bwd(h, target, "loss")
    loss = lax.psum(loss_dev[0, 0], ("x", "y", "c"))

    parts, recvd = {}, {}

    def bwd(fn, name, *args):
        keys = bwd_sched.get(name, [])
        res, got = fn(*args, name, [("a2a", parts[k]) for k in keys])
        for k, g in zip(keys, got):
            recvd[k] = g
        return res

    g_mix_pre, g_mix_post, g_ffn_pre, g_ffn_post = [None] * DEPTH, [None] * DEPTH, [None] * DEPTH, [None] * DEPTH
    g_ffn_cw, g_ffn_cb = [None] * DEPTH, [None] * DEPTH
    g_ssd_cw, g_ssd_cb, g_ssd_dtb, g_ssd_alog, g_ssd_d, g_ssd_nw = ([None] * 2 for _ in range(6))
    g_sc_cw = [None] * 2
    for i in reversed(range(DEPTH)):
        j = i // 2
        blk = saved[i]
        dup, parts[("down", i)], g_ffn_post[i], g_ffn_cw[i], g_ffn_cb[i] = bwd(
            ffn_bwd1, f"ffn_bwd1_{i}", dh, blk["f"], blk["up"], row(ffn_post_g[i]), weights[("down", i)], CW_ffn[i],
            row(ffn_conv_b[i]))
        dh, parts[("up", i)], g_ffn_pre[i] = bwd(inproj_bwd, f"ffn_bwd2_{i}", blk["x_ffn"], row(ffn_pre_g[i]), dup,
                                                  weights[("up", i)], dh, 2 * FFN_F // N_DEV)
        if i % 2 == 0:
            dy, dz, parts[("ssd_out", j)], g_mix_post[i], g_ssd_nw[j] = bwd(
                ssd_out_bwd, f"ssd_out_bwd_{j}", dh, blk["m"], blk["y"], blk["z"], row(ssd_norm_w[j]),
                weights[("ssd_out", j)], row(mix_post_g[i]))
            dxh, dbm, dcm, ddt, g_ssd_dtb[j], g_ssd_alog[j], g_ssd_d[j] = bwd(
                ssd_scan_bwd, f"ssd_scan_bwd_{j}", dy, blk["xh"], blk["bm"], blk["cm"], blk["dt_raw"], blk["sprev"],
                row(ssd_dt_bias[j]), row(ssd_A_log[j]), row(ssd_D[j]))
            d_in, g_ssd_cw[j], g_ssd_cb[j] = bwd(ssd_conv_bwd, f"ssd_conv_bwd_{j}", dxh, dbm, dcm, blk["raw"], dz, ddt,
                                                 CW_ssd[j], row(ssd_conv_b[j]))
            dh, parts[("ssd_in", j)], g_mix_pre[i] = bwd(inproj_bwd, f"ssd_bwd2_{j}", blk["x_mix"], row(mix_pre_g[i]),
                                                          d_in, weights[("ssd_in", j)], dh, SSD_IN // N_DEV)
        else:
            dbcv, parts[("sc_out", j)], g_mix_post[i], g_sc_cw[j] = bwd(
                sc_bwd1, f"sc_bwd1_{j}", dh, blk["m"], blk["bcv"], row(mix_post_g[i]), weights[("sc_out", j)], CW_sc[j])
            dh, parts[("sc_in", j)], g_mix_pre[i] = bwd(inproj_bwd, f"sc_bwd2_{j}", blk["x_mix"], row(mix_pre_g[i]),
                                                         dbcv, weights[("sc_in", j)], dh, 3 * D_MODEL // N_DEV)
    grad_x = dh[None]

    st = lambda lst: jnp.concatenate(lst, axis=0)
    small_full = [
        st(g_mix_pre), st(g_mix_post), st(g_ffn_pre), st(g_ffn_post),
        jnp.stack(g_ssd_cw), st(g_ssd_cb), st(g_ssd_dtb), st(g_ssd_alog), st(g_ssd_d), st(g_ssd_nw),
        jnp.stack(g_sc_cw), jnp.stack(g_ffn_cw), st(g_ffn_cb),
    ]
    full_shapes = [a.shape for a in small_full]
    last_out = exchange([("a2a", parts[("ssd_in", 0)]), ("ag", _pack(small_full))], "a2a_last")
    recvd[("ssd_in", 0)] = last_out[0]

    def finish(n, nl, w, m, v):
        return reduce_adamw([recvd[(n, l)] for l in range(nl)], w, m, v, "adamw_" + n)

    r_ssd_in = finish("ssd_in", 2, ssd_w_in, m_ssd_w_in, v_ssd_w_in)
    r_ssd_out = finish("ssd_out", 2, ssd_w_out, m_ssd_w_out, v_ssd_w_out)
    r_sc_in = finish("sc_in", 2, sc_w_in, m_sc_w_in, v_sc_w_in)
    r_sc_out = finish("sc_out", 2, sc_w_out, m_sc_w_out, v_sc_w_out)
    r_up = finish("up", DEPTH, ffn_w_up, m_ffn_w_up, v_ffn_w_up)
    r_down = finish("down", DEPTH, ffn_w_down, m_ffn_w_down, v_ffn_w_down)

    summed = small_reduce(last_out[1], "small_reduce")
    (s_mix_pre, s_mix_post, s_ffn_pre, s_ffn_post, s_ssd_cw, s_ssd_cb, s_ssd_dtb, s_ssd_alog, s_ssd_d, s_ssd_nw,
     s_sc_cw, s_ffn_cw, s_ffn_cb) = _unpack(summed, full_shapes)

    def my_cols(a, width):
        return lax.dynamic_slice_in_dim(a, me * width, width, axis=a.ndim - 1)

    s_ssd_cw = my_cols(s_ssd_cw, SSD_CONV // N_DEV)
    s_sc_cw = my_cols(s_sc_cw, D_MODEL // N_DEV)
    s_ffn_cw = my_cols(s_ffn_cw, FFN_F // N_DEV)

    small_g = [s_mix_pre, s_mix_post, s_ffn_pre, s_ffn_post, s_ssd_cw, s_ssd_cb, s_ssd_dtb, s_ssd_alog, s_ssd_d,
               s_ssd_nw, s_sc_cw, s_ffn_cw, s_ffn_cb]
    small_w = [mix_pre_g, mix_post_g, ffn_pre_g, ffn_post_g, ssd_conv_w, ssd_conv_b, ssd_dt_bias, ssd_A_log, ssd_D,
               ssd_norm_w, sc_conv_w, ffn_conv_w, ffn_conv_b]
    small_m = [m_mix_pre_g, m_mix_post_g, m_ffn_pre_g, m_ffn_post_g, m_ssd_conv_w, m_ssd_conv_b, m_ssd_dt_bias,
               m_ssd_A_log, m_ssd_D, m_ssd_norm_w, m_sc_conv_w, m_ffn_conv_w, m_ffn_conv_b]
    small_v = [v_mix_pre_g, v_mix_post_g, v_ffn_pre_g, v_ffn_post_g, v_ssd_conv_w, v_ssd_conv_b, v_ssd_dt_bias,
               v_ssd_A_log, v_ssd_D, v_ssd_norm_w, v_sc_conv_w, v_ffn_conv_w, v_ffn_conv_b]
    local_shapes = [a.shape for a in small_w]
    pd, pm, pv = small_adamw(_pack(small_g), _pack(small_w), _pack(small_m), _pack(small_v), "small_adamw")
    sd = _unpack(pd, local_shapes)
    sm = _unpack(pm, local_shapes)
    sv = _unpack(pv, local_shapes)

    def ordered(small, big):
        (mix_pre, mix_post, ffn_pre, ffn_post, ssd_cw, ssd_cb, dtb, alog, dsk, nw, sc_cw, ffn_cw, ffn_cb) = small
        (b_ssd_in, b_ssd_out, b_sc_in, b_sc_out, b_up, b_down) = big
        return [mix_pre, mix_post, ffn_pre, ffn_post, b_ssd_in, ssd_cw, ssd_cb, dtb, alog, dsk, nw, b_ssd_out,
                b_sc_in, sc_cw, b_sc_out, b_up, ffn_cw, ffn_cb, b_down]

    bigs = [r_ssd_in, r_ssd_out, r_sc_in, r_sc_out, r_up, r_down]
    grads = ordered(small_g, [r[0] for r in bigs])
    deltas = ordered(sd, [r[1] for r in bigs])
    new_m = ordered(sm, [r[2] for r in bigs])
    new_v = ordered(sv, [r[3] for r in bigs])
    return (loss, grad_x, *grads, *deltas, *new_m, *new_v)
```

```python
import functools

import jax
import jax.numpy as jnp
from jax import lax
from jax.experimental import pallas as pl
from jax.experimental.pallas import tpu as pltpu

F32 = jnp.float32
BF16 = jnp.bfloat16

EPS = 1e-6
D_MODEL = 1024
DEPTH = 4
N_DEV = 8
CHUNK = 64
SSD_DI = 2048
SSD_H = 32
SSD_P = 64
SSD_G = 8
SSD_R = SSD_H // SSD_G
SSD_N = 128
SSD_CONV = SSD_DI + 2 * SSD_G * SSD_N
SSD_IN = SSD_DI + SSD_CONV + SSD_H
LANES = 128
SSD_IN_PAD = -(-SSD_IN // LANES) * LANES
SSD_KW = 4
SC_KW = 3
FFN_F = 2816
FFN_KW = 3
SCAN_CPS = 2
TL = 256
HALO = 8
VMEM_LIMIT = 60 * 1024 * 1024

ADAM_LR = 0.001
ADAM_B1 = 0.9
ADAM_B2 = 0.999
ADAM_EPS = 1e-08
ADAM_WD = 0.01
ADAM_STEP = 10

MESH = pl.DeviceIdType.MESH


def _rms(x, g):
    r = lax.rsqrt(jnp.mean(x * x, axis=-1, keepdims=True) + EPS)
    return x * r * g


def _rms_bwd(x, g, dy):
    r = lax.rsqrt(jnp.mean(x * x, axis=-1, keepdims=True) + EPS)
    xh = x * r
    dg = jnp.sum(dy * xh, axis=0, keepdims=True)
    dxh = dy * g
    dx = r * (dxh - xh * jnp.mean(dxh * xh, axis=-1, keepdims=True))
    return dx, dg


def _mm(a, b):
    return jnp.dot(a, b, preferred_element_type=F32)


def _mm_nt(a, b):
    return lax.dot_general(a, b, (((1,), (1,)), ((), ())), preferred_element_type=F32)


def _mm_tn(a, b):
    return lax.dot_general(a, b, (((0,), (0,)), ((), ())), preferred_element_type=F32)


def _silu_parts(x):
    sg = jax.nn.sigmoid(x)
    return x * sg, sg * (1.0 + x * (1.0 - sg))


def _taps(ext, kw, tl):
    base = HALO - (kw - 1)
    return [ext[base + j:base + j + tl] for j in range(kw)]


def _conv_fwd(taps, w):
    out = taps[0] * w[0:1]
    for j in range(1, len(taps)):
        out = out + taps[j] * w[j:j + 1]
    return out


def _conv_bwd_in(extd, w, kw, tl):
    out = extd[kw - 1:kw - 1 + tl] * w[0:1]
    for j in range(1, kw):
        out = out + extd[kw - 1 - j:kw - 1 - j + tl] * w[j:j + 1]
    return out


def _conv_bwd_w(taps, dy):
    return jnp.concatenate([jnp.sum(dy * t, axis=0, keepdims=True) for t in taps], axis=0)


def _emit_row_shards(acc_ref, out_ref, stage_ref):
    rows = out_ref.shape[1]
    for k in range(N_DEV):
        stage_ref[...] = acc_ref[k * rows:(k + 1) * rows, :].astype(BF16)
        pltpu.sync_copy(stage_ref, out_ref.at[k])


def _emit_col_shards(acc_ref, out_ref, stage_ref):
    cols = out_ref.shape[2]
    for k in range(N_DEV):
        stage_ref[...] = acc_ref[:, k * cols:(k + 1) * cols].astype(BF16)
        pltpu.sync_copy(stage_ref, out_ref.at[k])


def _res(shape):
    nd = len(shape)
    return pl.BlockSpec(shape, lambda i: (0,) * nd, pipeline_mode=pl.Buffered(1))


def _small(shape):
    nd = len(shape)
    return pl.BlockSpec(shape, lambda i: (0,) * nd)


def _tile(n):
    return pl.BlockSpec((TL, n), lambda i: (i, 0))


def _rtile(n, nt):
    return pl.BlockSpec((TL, n), lambda i: (nt - 1 - i, 0))


def _halo_before(n, nt, reverse):
    per = TL // HALO
    if reverse:
        return pl.BlockSpec((HALO, n), lambda i: (jnp.maximum((nt - 1 - i) * per - 1, 0), 0))
    return pl.BlockSpec((HALO, n), lambda i: (jnp.maximum(i * per - 1, 0), 0))


_ANY = pl.BlockSpec(memory_space=pl.ANY)


def _sds(shape, dtype=F32):
    return jax.ShapeDtypeStruct(shape, dtype)


def _peer(k):
    x, y, c = lax.axis_index("x"), lax.axis_index("y"), lax.axis_index("c")
    px = x ^ (k >> 2)
    py = y ^ ((k >> 1) & 1)
    pc = c ^ (k & 1)
    return (px, py, pc), 4 * px + 2 * py + pc


def _my_index():
    return 4 * lax.axis_index("x") + 2 * lax.axis_index("y") + lax.axis_index("c")


SIBLING = 1
SAME_CORE_CHIPS = (2, 4, 6)


def _job_copies(kind, src_ref, out_ref, send_sems, recv_sems, local_sems, j):
    me = _my_index()
    sends, recvs = [], []

    def pair(pattern, sem, src, put_slot, get_slot):
        dev, _ = _peer(pattern)
        sems = dict(send_sem=send_sems.at[j, sem], recv_sem=recv_sems.at[j, sem], device_id=dev, device_id_type=MESH)
        sends.append(pltpu.make_async_remote_copy(src_ref=src, dst_ref=out_ref.at[put_slot], **sems))
        recvs.append(pltpu.make_async_remote_copy(src_ref=src, dst_ref=out_ref.at[get_slot], **sems))

    if kind == "agB":
        for k in SAME_CORE_CHIPS:
            _, mine_from_k = _peer(k)
            _, sib_from_k = _peer(k | SIBLING)
            pair(SIBLING, k, out_ref.at[mine_from_k], mine_from_k, sib_from_k)
        return None, sends, recvs
    patterns = (SIBLING,) + SAME_CORE_CHIPS if kind == "agA" else range(1, N_DEV)
    mine = src_ref.at[me] if kind == "a2a" else src_ref
    local = pltpu.make_async_copy(mine, out_ref.at[me], local_sems.at[j])
    for k in patterns:
        _, idx = _peer(k)
        pair(k, k - 1, src_ref.at[idx] if kind == "a2a" else src_ref, me, idx)
    return local, sends, recvs


def _pcall(body, name, grid, in_specs, out_specs, out_shape, scratch_shapes, args, jobs=()):
    n_in, n_out, nj = len(in_specs), len(out_specs), len(jobs)
    last = grid[0] - 1
    kinds = [k for k, _ in jobs]

    def wrapped(*refs):
        ins = refs[:n_in]
        csrc = refs[n_in:n_in + nj]
        outs = refs[n_in + nj:n_in + nj + n_out]
        cout = refs[n_in + nj + n_out:n_in + 2 * nj + n_out]
        rest = refs[n_in + 2 * nj + n_out:]

        def copies(j, kind):
            return _job_copies(kind, csrc[j], cout[j], send_sems, recv_sems, local_sems, j)

        def start(j, kind):
            local, sends, _ = copies(j, kind)
            if local is not None:
                local.start()
            for cp in sends:
                cp.start()

        def finish(j, kind, arrivals_only=False):
            local, sends, recvs = copies(j, kind)
            for cp in recvs:
                cp.wait_recv()
            if not arrivals_only:
                for cp in sends:
                    cp.wait_send()
                if local is not None:
                    local.wait()

        if nj:
            scratch, (send_sems, recv_sems, local_sems) = rest[:-3], rest[-3:]
            i = pl.program_id(0)

            @pl.when(i == 0)
            def _():
                for j in range(nj):
                    start(j, "agA" if kinds[j] == "ag2" else kinds[j])
                for j in range(nj):
                    if kinds[j] == "ag2":
                        finish(j, "agA", arrivals_only=True)
                        start(j, "agB")
        else:
            scratch = rest
        body(*ins, *outs, *scratch)
        if nj:
            @pl.when(i == last)
            def _():
                for j in range(nj):
                    if kinds[j] == "ag2":
                        finish(j, "agB")
                        _, sends, _ = copies(j, "agA")
                        for cp in sends:
                            cp.wait_send()
                        copies(j, "agA")[0].wait()
                    else:
                        finish(j, kinds[j])

    job_shapes = []
    aliases = {}
    for j, (kind, s) in enumerate(jobs):
        shp = (N_DEV,) + tuple(s.shape) if kind in ("ag", "agA", "ag2") else tuple(s.shape)
        job_shapes.append(_sds(shp, s.dtype))
        if kind == "agB":
            aliases[n_in + j] = n_out + j
    sems = [pltpu.SemaphoreType.DMA((nj, N_DEV - 1)), pltpu.SemaphoreType.DMA((nj, N_DEV - 1)),
            pltpu.SemaphoreType.DMA((nj,))] if nj else []
    res = pl.pallas_call(
        wrapped, name=name, grid=grid,
        in_specs=list(in_specs) + [_ANY] * nj,
        out_specs=list(out_specs) + [_ANY] * nj,
        out_shape=list(out_shape) + job_shapes,
        scratch_shapes=list(scratch_shapes) + sems,
        input_output_aliases=aliases,
        compiler_params=pltpu.CompilerParams(dimension_semantics=("arbitrary",), vmem_limit_bytes=VMEM_LIMIT,
                                             has_side_effects=bool(nj)),
    )(*args, *[s for _, s in jobs])
    return list(res[:n_out]), list(res[n_out:])


def exchange(jobs, name):
    def body(o_ref):
        o_ref[...] = jnp.zeros_like(o_ref)

    _, outs = _pcall(body, name, (1,), [], [_small((8, LANES))], [_sds((8, LANES))], [], [], jobs)
    return outs


def ffn_fwd(x, g_pre, w_up, conv_w, conv_b, w_down, g_post, name, jobs=()):
    L = x.shape[0]
    nt = L // TL
    F = FFN_F

    def body(x_ref, gpre_ref, wup_ref, cw_ref, cb_ref, wdn_ref, gpost_ref, up_ref, f_ref, xn_ref, carry_ref):
        i = pl.program_id(0)

        @pl.when(i == 0)
        def _():
            carry_ref[...] = jnp.zeros_like(carry_ref)

        x = x_ref[...]
        h = _rms(x, gpre_ref[...]).astype(BF16)
        up = _mm_nt(h, wup_ref[...])
        up_ref[...] = up
        ug = up[:, :F]
        val = up[:, F:]
        ext = jnp.concatenate([carry_ref[...], ug], axis=0)
        gate = _conv_fwd(_taps(ext, FFN_KW, TL), cw_ref[...]) + cb_ref[...]
        carry_ref[...] = ug[TL - HALO:, :]
        a = (gate * jax.nn.sigmoid(gate) * val).astype(BF16)
        f = _mm(a, wdn_ref[...])
        f_ref[...] = f
        xn_ref[...] = x + _rms(f, gpost_ref[...])

    return _pcall(body, name, (nt,),
                  [_tile(D_MODEL), _small((1, D_MODEL)), _res((2 * F, D_MODEL)), _small((FFN_KW, F)), _small((1, F)),
                   _res((F, D_MODEL)), _small((1, D_MODEL))],
                  [_tile(2 * F), _tile(D_MODEL), _tile(D_MODEL)],
                  [_sds((L, 2 * F)), _sds((L, D_MODEL)), _sds((L, D_MODEL))],
                  [pltpu.VMEM((HALO, F), F32)],
                  [x, g_pre, w_up, conv_w, conv_b, w_down, g_post], jobs)


def ffn_bwd1(dxo, f, up, g_post, w_down, conv_w, conv_b, name, jobs=()):
    L = dxo.shape[0]
    nt = L // TL
    F = FFN_F
    rows = F // N_DEV

    def body(dxo_ref, f_ref, up_ref, halo_ref, gpost_ref, wdn_ref, cw_ref, cb_ref,
             dup_ref, dwdn_ref, dgp_ref, dcw_ref, dcb_ref, acc_ref, carry_ref, stage_ref):
        i = pl.program_id(0)
        t = nt - 1 - i

        @pl.when(i == 0)
        def _():
            acc_ref[...] = jnp.zeros_like(acc_ref)
            carry_ref[...] = jnp.zeros_like(carry_ref)
            dgp_ref[...] = jnp.zeros_like(dgp_ref)
            dcw_ref[...] = jnp.zeros_like(dcw_ref)
            dcb_ref[...] = jnp.zeros_like(dcb_ref)

        df, dgp = _rms_bwd(f_ref[...], gpost_ref[...], dxo_ref[...])
        dgp_ref[...] += dgp
        dfb = df.astype(BF16)
        da = _mm_nt(dfb, wdn_ref[...])
        up = up_ref[...]
        ug = up[:, :F]
        val = up[:, F:]
        halo = jnp.where(t == 0, 0.0, halo_ref[...])
        ext = jnp.concatenate([halo, ug], axis=0)
        w = cw_ref[...]
        taps = _taps(ext, FFN_KW, TL)
        gate = _conv_fwd(taps, w) + cb_ref[...]
        s, ds = _silu_parts(gate)
        acc_ref[...] += _mm_tn((s * val).astype(BF16), dfb)
        dval = da * s
        dgate = da * val * ds
        dcb_ref[...] += jnp.sum(dgate, axis=0, keepdims=True)
        dcw_ref[...] += _conv_bwd_w(taps, dgate)
        extd = jnp.concatenate([dgate, carry_ref[...]], axis=0)
        dug = _conv_bwd_in(extd, w, FFN_KW, TL)
        carry_ref[...] = dgate[:HALO, :]
        dup_ref[...] = jnp.concatenate([dug, dval], axis=1).astype(BF16)

        @pl.when(i == nt - 1)
        def _():
            _emit_row_shards(acc_ref, dwdn_ref, stage_ref)

    return _pcall(body, name, (nt,),
                  [_rtile(D_MODEL, nt), _rtile(D_MODEL, nt), _rtile(2 * F, nt), _halo_before(F, nt, True),
                   _small((1, D_MODEL)), _res((F, D_MODEL)), _small((FFN_KW, F)), _small((1, F))],
                  [_rtile(2 * F, nt), _ANY, _small((1, D_MODEL)), _small((FFN_KW, F)), _small((1, F))],
                  [_sds((L, 2 * F), BF16), _sds((N_DEV, rows, D_MODEL), BF16), _sds((1, D_MODEL)),
                   _sds((FFN_KW, F)), _sds((1, F))],
                  [pltpu.VMEM((F, D_MODEL), F32), pltpu.VMEM((HALO, F), F32), pltpu.VMEM((rows, D_MODEL), BF16)],
                  [dxo, f, up, up, g_post, w_down, conv_w, conv_b], jobs)


def inproj_bwd(x, g_pre, d, w, dxo, cols, name, jobs=(), transposed=False):
    L = x.shape[0]
    nt = L // TL
    N = d.shape[1]
    w_shape = (N, D_MODEL) if transposed else (D_MODEL, N)
    shard_shape = (cols, D_MODEL) if transposed else (D_MODEL, cols)

    def body(x_ref, g_ref, d_ref, w_ref, dxo_ref, dx_ref, dw_ref, dg_ref, acc_ref, stage_ref):
        i = pl.program_id(0)

        @pl.when(i == 0)
        def _():
            acc_ref[...] = jnp.zeros_like(acc_ref)
            dg_ref[...] = jnp.zeros_like(dg_ref)

        x = x_ref[...]
        g = g_ref[...]
        d = d_ref[...]
        h = _rms(x, g).astype(BF16)
        if transposed:
            dh = _mm(d, w_ref[...])
            acc_ref[...] += _mm_tn(d, h)
        else:
            dh = _mm_nt(d, w_ref[...])
            acc_ref[...] += _mm_tn(h, d)
        dxn, dg = _rms_bwd(x, g, dh)
        dx_ref[...] = dxo_ref[...] + dxn
        dg_ref[...] += dg

        @pl.when(i == nt - 1)
        def _():
            (_emit_row_shards if transposed else _emit_col_shards)(acc_ref, dw_ref, stage_ref)

    return _pcall(body, name, (nt,),
                  [_tile(D_MODEL), _small((1, D_MODEL)), _tile(N), _res(w_shape), _tile(D_MODEL)],
                  [_tile(D_MODEL), _ANY, _small((1, D_MODEL))],
                  [_sds((L, D_MODEL)), _sds((N_DEV,) + shard_shape, BF16), _sds((1, D_MODEL))],
                  [pltpu.VMEM(w_shape, F32), pltpu.VMEM(shard_shape, BF16)],
                  [x, g_pre, d, w, dxo], jobs)


def sc_fwd(x, g_pre, w_in, conv_w, w_out, g_post, name, jobs=()):
    L = x.shape[0]
    nt = L // TL
    W = D_MODEL

    def body(x_ref, gpre_ref, win_ref, cw_ref, wout_ref, gpost_ref, bcv_ref, m_ref, xn_ref, carry_ref):
        i = pl.program_id(0)

        @pl.when(i == 0)
        def _():
            carry_ref[...] = jnp.zeros_like(carry_ref)

        x = x_ref[...]
        h = _rms(x, gpre_ref[...]).astype(BF16)
        bcv = _mm(h, win_ref[...])
        bcv_ref[...] = bcv
        gb = bcv[:, :W]
        p = bcv[:, W:2 * W] * bcv[:, 2 * W:]
        ext = jnp.concatenate([carry_ref[...], p], axis=0)
        u = _conv_fwd(_taps(ext, SC_KW, TL), cw_ref[...])
        carry_ref[...] = p[TL - HALO:, :]
        m = _mm((gb * u).astype(BF16), wout_ref[...])
        m_ref[...] = m
        xn_ref[...] = x + _rms(m, gpost_ref[...])

    return _pcall(body, name, (nt,),
                  [_tile(W), _small((1, W)), _res((W, 3 * W)), _small((SC_KW, W)), _res((W, W)), _small((1, W))],
                  [_tile(3 * W), _tile(W), _tile(W)],
                  [_sds((L, 3 * W)), _sds((L, W)), _sds((L, W))],
                  [pltpu.VMEM((HALO, W), F32)],
                  [x, g_pre, w_in, conv_w, w_out, g_post], jobs)


def sc_bwd1(dxo, m, bcv, g_post, w_out, conv_w, name, jobs=()):
    L = dxo.shape[0]
    nt = L // TL
    W = D_MODEL
    rows = W // N_DEV

    def body(dxo_ref, m_ref, bcv_ref, halo_ref, gpost_ref, wout_ref, cw_ref,
             dbcv_ref, dwout_ref, dgp_ref, dcw_ref, acc_ref, carry_ref, stage_ref):
        i = pl.program_id(0)
        t = nt - 1 - i

        @pl.when(i == 0)
        def _():
            acc_ref[...] = jnp.zeros_like(acc_ref)
            carry_ref[...] = jnp.zeros_like(carry_ref)
            dgp_ref[...] = jnp.zeros_like(dgp_ref)
            dcw_ref[...] = jnp.zeros_like(dcw_ref)

        dm, dgp = _rms_bwd(m_ref[...], gpost_ref[...], dxo_ref[...])
        dgp_ref[...] += dgp
        dmb = dm.astype(BF16)
        dq = _mm_nt(dmb, wout_ref[...])
        bcv = bcv_ref[...]
        gb = bcv[:, :W]
        gc = bcv[:, W:2 * W]
        v = bcv[:, 2 * W:]
        hb = halo_ref[...]
        halo = jnp.where(t == 0, 0.0, hb[:, W:2 * W] * hb[:, 2 * W:])
        ext = jnp.concatenate([halo, gc * v], axis=0)
        w = cw_ref[...]
        taps = _taps(ext, SC_KW, TL)
        u = _conv_fwd(taps, w)
        acc_ref[...] += _mm_tn((gb * u).astype(BF16), dmb)
        dgb = dq * u
        du = dq * gb
        dcw_ref[...] += _conv_bwd_w(taps, du)
        extd = jnp.concatenate([du, carry_ref[...]], axis=0)
        dp = _conv_bwd_in(extd, w, SC_KW, TL)
        carry_ref[...] = du[:HALO, :]
        dbcv_ref[...] = jnp.concatenate([dgb, dp * v, dp * gc], axis=1).astype(BF16)

        @pl.when(i == nt - 1)
        def _():
            _emit_row_shards(acc_ref, dwout_ref, stage_ref)

    return _pcall(body, name, (nt,),
                  [_rtile(W, nt), _rtile(W, nt), _rtile(3 * W, nt), _halo_before(3 * W, nt, True),
                   _small((1, W)), _res((W, W)), _small((SC_KW, W))],
                  [_rtile(3 * W, nt), _ANY, _small((1, W)), _small((SC_KW, W))],
                  [_sds((L, 3 * W), BF16), _sds((N_DEV, rows, W), BF16), _sds((1, W)), _sds((SC_KW, W))],
                  [pltpu.VMEM((W, W), F32), pltpu.VMEM((HALO, W), F32), pltpu.VMEM((rows, W), BF16)],
                  [dxo, m, bcv, bcv, g_post, w_out, conv_w], jobs)


def ssd_inproj(x, g_pre, w_in, conv_w, conv_b, name, jobs=()):
    L = x.shape[0]
    nt = L // TL

    def body(x_ref, gpre_ref, win_ref, cw_ref, cb_ref, z_ref, raw_ref, dt_ref, xh_ref, bm_ref, cm_ref, carry_ref):
        i = pl.program_id(0)

        @pl.when(i == 0)
        def _():
            carry_ref[...] = jnp.zeros_like(carry_ref)

        h = _rms(x_ref[...], gpre_ref[...]).astype(BF16)
        zx = _mm_nt(h, win_ref[...])
        z_ref[...] = zx[:, :SSD_DI]
        raw = zx[:, SSD_DI:SSD_DI + SSD_CONV]
        raw_ref[...] = raw
        dt_ref[...] = zx[:, SSD_DI + SSD_CONV:SSD_IN]
        ext = jnp.concatenate([carry_ref[...], raw], axis=0)
        pre = _conv_fwd(_taps(ext, SSD_KW, TL), cw_ref[...]) + cb_ref[...]
        carry_ref[...] = raw[TL - HALO:, :]
        act = pre * jax.nn.sigmoid(pre)
        for hh in range(SSD_H):
            xh_ref[hh] = act[:, hh * SSD_P:(hh + 1) * SSD_P]
        for g in range(SSD_G):
            bm_ref[g] = act[:, SSD_DI + g * SSD_N:SSD_DI + (g + 1) * SSD_N]
            cm_ref[g] = act[:, SSD_DI + (SSD_G + g) * SSD_N:SSD_DI + (SSD_G + g + 1) * SSD_N]

    return _pcall(body, name, (nt,),
                  [_tile(D_MODEL), _small((1, D_MODEL)), _res((SSD_IN_PAD, D_MODEL)), _small((SSD_KW, SSD_CONV)),
                   _small((1, SSD_CONV))],
                  [_tile(SSD_DI), _tile(SSD_CONV), _tile(SSD_H),
                   pl.BlockSpec((SSD_H, TL, SSD_P), lambda i: (0, i, 0)),
                   pl.BlockSpec((SSD_G, TL, SSD_N), lambda i: (0, i, 0)),
                   pl.BlockSpec((SSD_G, TL, SSD_N), lambda i: (0, i, 0))],
                  [_sds((L, SSD_DI)), _sds((L, SSD_CONV)), _sds((L, SSD_H)), _sds((SSD_H, L, SSD_P)),
                   _sds((SSD_G, L, SSD_N)), _sds((SSD_G, L, SSD_N))],
                  [pltpu.VMEM((HALO, SSD_CONV), F32)],
                  [x, g_pre, w_in, conv_w, conv_b], jobs)


def _per_head(v, heads):
    return jnp.stack([v[:, h:h + 1] for h in heads], axis=0)


def _heads_to_lanes(cols):
    return jnp.concatenate(cols, axis=1)


def _rep_heads(v):
    g, a, b = v.shape
    return jnp.broadcast_to(v[:, None], (g, SSD_R, a, b)).reshape(g * SSD_R, a, b)


def _sum_heads(v):
    h, a, b = v.shape
    return v.reshape(SSD_G, SSD_R, a, b).sum(axis=1)


def _chunk_terms(dtr, bias, a_log):
    T = CHUNK
    dt = jax.nn.softplus(dtr + bias)
    a_head = -jnp.exp(a_log)
    ii = lax.broadcasted_iota(jnp.int32, (T, T), 0)
    jj = lax.broadcasted_iota(jnp.int32, (T, T), 1)
    tri = ii >= jj
    cs = jnp.dot(tri.astype(F32), dt * a_head, precision=lax.Precision.HIGHEST, preferred_element_type=F32)
    return dict(dt=dt, a_head=a_head, tri=tri, cs=cs, cs_t=cs.T)


def _head_terms(ct, heads):
    T = CHUNK
    cs, cs_t, tri = ct["cs"], ct["cs_t"], ct["tri"]
    csc = _per_head(cs, heads)
    csr = jnp.stack([cs_t[h:h + 1, :] for h in heads], axis=0)
    cl = _per_head(cs[T - 1:T, :], heads)
    lmat = jnp.exp(jnp.where(tri[None], csc - csr, -jnp.inf))
    return dict(dtc=_per_head(ct["dt"], heads), lmat=lmat, ecs=jnp.exp(csc), dsc=jnp.exp(cl - csc), cdc=jnp.exp(cl))


def ssd_scan_fwd(xh, bm, cm, dt_raw, dt_bias, a_log, d_skip, name, jobs=()):
    L = xh.shape[1]
    nc = L // CHUNK
    T = CHUNK
    TS = SCAN_CPS * T

    def body(xh_ref, bm_ref, cm_ref, dt_ref, bias_ref, alog_ref, dsk_ref, y_ref, sp_ref, st_ref):
        c = pl.program_id(0)

        @pl.when(c == 0)
        def _():
            st_ref[...] = jnp.zeros_like(st_ref)

        heads = range(SSD_H)
        dh = _per_head(dsk_ref[...], heads)
        s = st_ref[...]
        for k in range(SCAN_CPS):
            rows = slice(k * T, (k + 1) * T)
            ht = _head_terms(_chunk_terms(dt_ref[rows, :], bias_ref[...], alog_ref[...]), heads)
            x = xh_ref[:, rows, :]
            bgb = bm_ref[:, rows, :].astype(BF16)
            cgb = cm_ref[:, rows, :].astype(BF16)
            bh = _rep_heads(bgb)
            ch = _rep_heads(cgb)
            xt = x * ht["dtc"]
            cb = jnp.einsum("gln,gsn->gls", cgb, bgb, preferred_element_type=F32)
            mb = (_rep_heads(cb) * ht["lmat"]).astype(BF16)
            yd = jnp.einsum("hls,hsp->hlp", mb, xt.astype(BF16), preferred_element_type=F32)
            sb = s.astype(BF16)
            yo = jnp.einsum("hln,hpn->hlp", ch, sb, preferred_element_type=F32) * ht["ecs"]
            y_ref[:, rows, :] = yd + yo + x * dh
            sp_ref[k] = sb
            xd = (xt * ht["dsc"]).astype(BF16)
            s = s * ht["cdc"] + jnp.einsum("htp,htn->hpn", xd, bh, preferred_element_type=F32)
        st_ref[...] = s

    hd = pl.BlockSpec((SSD_H, TS, SSD_P), lambda c: (0, c, 0))
    gr = pl.BlockSpec((SSD_G, TS, SSD_N), lambda c: (0, c, 0))
    return _pcall(body, name, (nc // SCAN_CPS,),
                  [hd, gr, gr, pl.BlockSpec((TS, SSD_H), lambda c: (c, 0)),
                   _small((1, SSD_H)), _small((1, SSD_H)), _small((1, SSD_H))],
                  [hd, pl.BlockSpec((SCAN_CPS, SSD_H, SSD_P, SSD_N), lambda c: (c, 0, 0, 0))],
                  [_sds((SSD_H, L, SSD_P)), _sds((nc, SSD_H, SSD_P, SSD_N), BF16)],
                  [pltpu.VMEM((SSD_H, SSD_P, SSD_N), F32)],
                  [xh, bm, cm, dt_raw, dt_bias, a_log, d_skip], jobs)


def ssd_scan_bwd(dy, xh, bm, cm, dt_raw, sprev, dt_bias, a_log, d_skip, name, jobs=()):
    L = xh.shape[1]
    nc = L // CHUNK
    T = CHUNK

    def body(dy_ref, xh_ref, bm_ref, cm_ref, dt_ref, sp_ref, bias_ref, alog_ref, dsk_ref,
             dxh_ref, dbm_ref, dcm_ref, ddt_ref, dbias_ref, dalog_ref, ddsk_ref, g_ref):
        i = pl.program_id(0)

        @pl.when(i == 0)
        def _():
            g_ref[...] = jnp.zeros_like(g_ref)
            dbias_ref[...] = jnp.zeros_like(dbias_ref)
            dalog_ref[...] = jnp.zeros_like(dalog_ref)
            ddsk_ref[...] = jnp.zeros_like(ddsk_ref)

        bias = bias_ref[...]
        heads = range(SSD_H)
        dh = _per_head(dsk_ref[...], heads)
        g = g_ref[...]
        for k in reversed(range(SCAN_CPS)):
            rows = slice(k * T, (k + 1) * T)
            dtr = dt_ref[rows, :]
            ct = _chunk_terms(dtr, bias, alog_ref[...])
            dt, a_head, tri = ct["dt"], ct["a_head"], ct["tri"]
            ht = _head_terms(ct, heads)
            dtc, lmat, ecs, dsc, cdc = ht["dtc"], ht["lmat"], ht["ecs"], ht["dsc"], ht["cdc"]
            x = xh_ref[:, rows, :]
            dyv = dy_ref[:, rows, :]
            dyb = dyv.astype(BF16)
            bgb = bm_ref[:, rows, :].astype(BF16)
            cgb = cm_ref[:, rows, :].astype(BF16)
            bh = _rep_heads(bgb)
            ch = _rep_heads(cgb)
            sb = sp_ref[k]
            gb = g.astype(BF16)
            xt = x * dtc
            xtb = xt.astype(BF16)
            mf = _rep_heads(jnp.einsum("gln,gsn->gls", cgb, bgb, preferred_element_type=F32)) * lmat
            mb = mf.astype(BF16)
            ddsk = jnp.sum(dyv * x, axis=(1, 2), keepdims=True)
            dx = dyv * dh
            yo_raw = jnp.einsum("hln,hpn->hlp", ch, sb, preferred_element_type=F32)
            w1 = dyv * ecs
            w1b = w1.astype(BF16)
            ds_off = jnp.einsum("hlp,hln->hpn", w1b, ch, preferred_element_type=F32)
            dch = jnp.einsum("hlp,hpn->hln", w1b, sb, preferred_element_type=F32)
            dcs_c = jnp.sum(w1 * yo_raw, axis=2, keepdims=True)
            dm = jnp.einsum("hlp,hsp->hls", dyb, xtb, preferred_element_type=F32)
            dxt = jnp.einsum("hls,hlp->hsp", mb, dyb, preferred_element_type=F32)
            dcbb = _sum_heads(dm * lmat).astype(BF16)
            dseg = dm * mf
            dcs_c = dcs_c + jnp.sum(dseg, axis=2, keepdims=True)
            dcs_r = -jnp.sum(dseg, axis=1, keepdims=True)
            dc = jnp.einsum("gls,gsn->gln", dcbb, bgb, preferred_element_type=F32) + _sum_heads(dch)
            db = jnp.einsum("gls,gln->gsn", dcbb, cgb, preferred_element_type=F32)
            xd = xt * dsc
            dxd = jnp.einsum("htn,hpn->htp", bh, gb, preferred_element_type=F32)
            db = db + _sum_heads(jnp.einsum("htp,hpn->htn", xd.astype(BF16), gb, preferred_element_type=F32))
            dxt = dxt + dxd * dsc
            d_ds = jnp.sum(dxd * xt, axis=2, keepdims=True)
            d_cd = jnp.sum(g * sb.astype(F32), axis=(1, 2), keepdims=True)
            g = g * cdc + ds_off
            t1 = d_ds * dsc
            dcs_c = dcs_c - t1
            dcl = jnp.sum(t1, axis=1, keepdims=True) + d_cd * cdc
            ddt_c = jnp.sum(dxt * x, axis=2, keepdims=True)
            dxh_ref[:, rows, :] = dx + dxt * dtc
            dbm_ref[:, rows, :] = db
            dcm_ref[:, rows, :] = dc
            lanes = lambda v: _heads_to_lanes([v[h] for h in heads])
            rows_t = jnp.concatenate([dcs_r[h] for h in heads], axis=0).T
            last = (lax.broadcasted_iota(jnp.int32, (T, 1), 0) == T - 1).astype(F32)
            dcs = lanes(dcs_c) + rows_t + last * lanes(dcl)
            da = lax.dot_general(tri.astype(F32), dcs, (((0,), (0,)), ((), ())),
                                 precision=lax.Precision.HIGHEST, preferred_element_type=F32)
            ddt = da * a_head + lanes(ddt_c)
            dalog_ref[...] += jnp.sum(da * dt, axis=0, keepdims=True)
            ddtr = ddt * jax.nn.sigmoid(dtr + bias)
            ddt_ref[rows, :] = ddtr
            dbias_ref[...] += jnp.sum(ddtr, axis=0, keepdims=True)
            ddsk_ref[...] += lanes(ddsk)
        g_ref[...] = g

        @pl.when(i == nb - 1)
        def _():
            dalog_ref[...] = dalog_ref[...] * (-jnp.exp(alog_ref[...]))

    nb = nc // SCAN_CPS
    TS = SCAN_CPS * T
    hd = pl.BlockSpec((SSD_H, TS, SSD_P), lambda i: (0, nb - 1 - i, 0))
    gr = pl.BlockSpec((SSD_G, TS, SSD_N), lambda i: (0, nb - 1 - i, 0))
    tk = pl.BlockSpec((TS, SSD_H), lambda i: (nb - 1 - i, 0))
    return _pcall(body, name, (nb,),
                  [hd, hd, gr, gr, tk, pl.BlockSpec((SCAN_CPS, SSD_H, SSD_P, SSD_N), lambda i: (nb - 1 - i, 0, 0, 0)),
                   _small((1, SSD_H)), _small((1, SSD_H)), _small((1, SSD_H))],
                  [hd, gr, gr, tk, _small((1, SSD_H)), _small((1, SSD_H)), _small((1, SSD_H))],
                  [_sds((SSD_H, L, SSD_P)), _sds((SSD_G, L, SSD_N)), _sds((SSD_G, L, SSD_N)), _sds((L, SSD_H)),
                   _sds((1, SSD_H)), _sds((1, SSD_H)), _sds((1, SSD_H))],
                  [pltpu.VMEM((SSD_H, SSD_P, SSD_N), F32)],
                  [dy, xh, bm, cm, dt_raw, sprev, dt_bias, a_log, d_skip], jobs)


def _heads_to_tokens(y_ref):
    return jnp.concatenate([y_ref[h] for h in range(SSD_H)], axis=1)


def ssd_out_fwd(x, y, z, norm_w, w_out, g_post, name, jobs=()):
    L = x.shape[0]
    nt = L // TL

    def body(x_ref, y_ref, z_ref, nw_ref, wout_ref, gpost_ref, m_ref, xn_ref):
        z = z_ref[...]
        yg = _heads_to_tokens(y_ref) * (z * jax.nn.sigmoid(z))
        yn = _rms(yg, nw_ref[...]).astype(BF16)
        m = _mm(yn, wout_ref[...])
        m_ref[...] = m
        xn_ref[...] = x_ref[...] + _rms(m, gpost_ref[...])

    return _pcall(body, name, (nt,),
                  [_tile(D_MODEL), pl.BlockSpec((SSD_H, TL, SSD_P), lambda i: (0, i, 0)), _tile(SSD_DI),
                   _small((1, SSD_DI)), _res((SSD_DI, D_MODEL)), _small((1, D_MODEL))],
                  [_tile(D_MODEL), _tile(D_MODEL)],
                  [_sds((L, D_MODEL)), _sds((L, D_MODEL))],
                  [],
                  [x, y, z, norm_w, w_out, g_post], jobs)


def ssd_out_bwd(dxo, m, y, z, norm_w, w_out, g_post, name, jobs=()):
    L = dxo.shape[0]
    nt = L // TL
    rows = SSD_DI // N_DEV

    def body(dxo_ref, m_ref, y_ref, z_ref, nw_ref, wout_ref, gpost_ref,
             dy_ref, dz_ref, dwout_ref, dgp_ref, dnw_ref, acc_ref, stage_ref):
        i = pl.program_id(0)

        @pl.when(i == 0)
        def _():
            acc_ref[...] = jnp.zeros_like(acc_ref)
            dgp_ref[...] = jnp.zeros_like(dgp_ref)
            dnw_ref[...] = jnp.zeros_like(dnw_ref)

        dm, dgp = _rms_bwd(m_ref[...], gpost_ref[...], dxo_ref[...])
        dgp_ref[...] += dgp
        dmb = dm.astype(BF16)
        dyn = _mm_nt(dmb, wout_ref[...])
        z = z_ref[...]
        y = _heads_to_tokens(y_ref)
        sil, dsil = _silu_parts(z)
        yg = y * sil
        nw = nw_ref[...]
        acc_ref[...] += _mm_tn(_rms(yg, nw).astype(BF16), dmb)
        dyg, dnw = _rms_bwd(yg, nw, dyn)
        dnw_ref[...] += dnw
        dyv = dyg * sil
        dz_ref[...] = dyg * y * dsil
        for h in range(SSD_H):
            dy_ref[h] = dyv[:, h * SSD_P:(h + 1) * SSD_P]

        @pl.when(i == nt - 1)
        def _():
            _emit_row_shards(acc_ref, dwout_ref, stage_ref)

    hd = pl.BlockSpec((SSD_H, TL, SSD_P), lambda i: (0, i, 0))
    return _pcall(body, name, (nt,),
                  [_tile(D_MODEL), _tile(D_MODEL), hd, _tile(SSD_DI), _small((1, SSD_DI)), _res((SSD_DI, D_MODEL)),
                   _small((1, D_MODEL))],
                  [hd, _tile(SSD_DI), _ANY, _small((1, D_MODEL)), _small((1, SSD_DI))],
                  [_sds((SSD_H, L, SSD_P)), _sds((L, SSD_DI)), _sds((N_DEV, rows, D_MODEL), BF16),
                   _sds((1, D_MODEL)), _sds((1, SSD_DI))],
                  [pltpu.VMEM((SSD_DI, D_MODEL), F32), pltpu.VMEM((rows, D_MODEL), BF16)],
                  [dxo, m, y, z, norm_w, w_out, g_post], jobs)


def ssd_conv_bwd(dxh, dbm, dcm, xbc_raw, dz, ddt_raw, conv_w, conv_b, name, jobs=()):
    L = xbc_raw.shape[0]
    nt = L // TL

    def body(dxh_ref, dbm_ref, dcm_ref, raw_ref, halo_ref, dz_ref, ddt_ref, cw_ref, cb_ref,
             d_ref, dcw_ref, dcb_ref, carry_ref):
        i = pl.program_id(0)
        t = nt - 1 - i

        @pl.when(i == 0)
        def _():
            carry_ref[...] = jnp.zeros_like(carry_ref)
            dcw_ref[...] = jnp.zeros_like(dcw_ref)
            dcb_ref[...] = jnp.zeros_like(dcb_ref)

        dact = jnp.concatenate([dxh_ref[h] for h in range(SSD_H)] + [dbm_ref[g] for g in range(SSD_G)]
                               + [dcm_ref[g] for g in range(SSD_G)], axis=1)
        halo = jnp.where(t == 0, 0.0, halo_ref[...])
        ext = jnp.concatenate([halo, raw_ref[...]], axis=0)
        w = cw_ref[...]
        taps = _taps(ext, SSD_KW, TL)
        pre = _conv_fwd(taps, w) + cb_ref[...]
        _, dsil = _silu_parts(pre)
        dpre = dact * dsil
        dcb_ref[...] += jnp.sum(dpre, axis=0, keepdims=True)
        dcw_ref[...] += _conv_bwd_w(taps, dpre)
        extd = jnp.concatenate([dpre, carry_ref[...]], axis=0)
        draw = _conv_bwd_in(extd, w, SSD_KW, TL)
        carry_ref[...] = dpre[:HALO, :]
        d_ref[:, :SSD_DI] = dz_ref[...].astype(BF16)
        d_ref[:, SSD_DI:SSD_DI + SSD_CONV] = draw.astype(BF16)
        tail = jnp.concatenate([ddt_ref[...], jnp.zeros((TL, SSD_IN_PAD - SSD_IN), F32)], axis=1)
        d_ref[:, SSD_DI + SSD_CONV:] = tail.astype(BF16)

    hd = pl.BlockSpec((SSD_H, TL, SSD_P), lambda i: (0, nt - 1 - i, 0))
    gr = pl.BlockSpec((SSD_G, TL, SSD_N), lambda i: (0, nt - 1 - i, 0))
    return _pcall(body, name, (nt,),
                  [hd, gr, gr, _rtile(SSD_CONV, nt), _halo_before(SSD_CONV, nt, True), _rtile(SSD_DI, nt),
                   _rtile(SSD_H, nt), _small((SSD_KW, SSD_CONV)), _small((1, SSD_CONV))],
                  [_rtile(SSD_IN_PAD, nt), _small((SSD_KW, SSD_CONV)), _small((1, SSD_CONV))],
                  [_sds((L, SSD_IN_PAD), BF16), _sds((SSD_KW, SSD_CONV)), _sds((1, SSD_CONV))],
                  [pltpu.VMEM((HALO, SSD_CONV), F32)],
                  [dxh, dbm, dcm, xbc_raw, xbc_raw, dz, ddt_raw, conv_w, conv_b], jobs)


def loss_fwd_bwd(y, target, name, jobs=()):
    L = y.shape[0]
    nt = L // TL

    def body(y_ref, t_ref, loss_ref, dy_ref):
        i = pl.program_id(0)

        @pl.when(i == 0)
        def _():
            loss_ref[...] = jnp.zeros_like(loss_ref)

        err = y_ref[...] - t_ref[...]
        dy_ref[...] = err * (1.0 / D_MODEL)
        loss_ref[...] += 0.5 * jnp.sum(jnp.mean(err * err, axis=-1, keepdims=True), axis=0, keepdims=True)

    return _pcall(body, name, (nt,),
                  [_tile(D_MODEL), _tile(D_MODEL)],
                  [_small((1, 1)), _tile(D_MODEL)],
                  [_sds((1, 1)), _sds((L, D_MODEL))],
                  [],
                  [y, target], jobs)


def _adamw_math(w, g, m, v):
    m = ADAM_B1 * m + (1.0 - ADAM_B1) * g
    v = ADAM_B2 * v + (1.0 - ADAM_B2) * (g * g)
    m_hat = m / (1.0 - ADAM_B1 ** ADAM_STEP)
    v_hat = v / (1.0 - ADAM_B2 ** ADAM_STEP)
    delta = -ADAM_LR * (m_hat / (jnp.sqrt(v_hat) + ADAM_EPS) + ADAM_WD * w)
    return delta, m, v


def _row_tile(rows):
    for cand in (256, 176, 128, 64, 32, 16, 8):
        if rows % cand == 0:
            return cand
    return rows


def reduce_adamw(recvs, w, m, v, name, jobs=()):
    nl = len(recvs)
    _, R, C = recvs[0].shape
    if R % 16 == 0:
        tr, tc = _row_tile(R), C
    else:
        tr, tc = R, 2 * LANES
    nr = (R // tr) * (C // tc)

    def body(*refs):
        r_refs = refs[:nl]
        w_ref, m_ref, v_ref, g_out, d_out, m_out, v_out = refs[nl:]
        layer = pl.program_id(0) // nr
        for ll in range(nl):
            @pl.when(layer == ll)
            def _(ll=ll):
                g = r_refs[ll][0].astype(F32)
                for j in range(1, N_DEV):
                    g = g + r_refs[ll][j].astype(F32)
                delta, mn, vn = _adamw_math(w_ref[0], g, m_ref[0], v_ref[0])
                g_out[0] = g
                d_out[0] = delta
                m_out[0] = mn
                v_out[0] = vn

    def recv_spec(ll):
        def index(i):
            t = jnp.where(i // nr == ll, i % nr, 0)
            return (0, t, 0) if tc == C else (0, 0, t)
        return pl.BlockSpec((N_DEV, tr, tc), index)

    blk = pl.BlockSpec((1, tr, tc), lambda i: (i // nr, i % nr, 0) if tc == C else (i // nr, 0, i % nr))
    return _pcall(body, name, (nl * nr,),
                  [recv_spec(ll) for ll in range(nl)] + [blk, blk, blk],
                  [blk] * 4,
                  [_sds((nl, R, C))] * 4,
                  [],
                  [*recvs, w, m, v], jobs)


def small_reduce(gathered, name):
    _, R, C = gathered.shape

    def body(r_ref, o_ref):
        g = r_ref[0]
        for j in range(1, N_DEV):
            g = g + r_ref[j]
        o_ref[...] = g

    return pl.pallas_call(body, name=name, out_shape=_sds((R, C)))(gathered)


def small_adamw(g, w, m, v, name):
    def body(g_ref, w_ref, m_ref, v_ref, d_out, m_out, v_out):
        delta, mn, vn = _adamw_math(w_ref[...], g_ref[...], m_ref[...], v_ref[...])
        d_out[...] = delta
        m_out[...] = mn
        v_out[...] = vn

    return pl.pallas_call(body, name=name, out_shape=[_sds(g.shape)] * 3)(g, w, m, v)


def _pack(arrs):
    flat = jnp.concatenate([a.reshape(-1) for a in arrs])
    n = flat.shape[0]
    rows = -(-n // (8 * LANES)) * 8
    flat = jnp.pad(flat, (0, rows * LANES - n))
    return flat.reshape(rows, LANES)


def _unpack(packed, shapes):
    flat = packed.reshape(-1)
    out = []
    off = 0
    for s in shapes:
        n = 1
        for d in s:
            n *= d
        out.append(flat[off:off + n].reshape(s))
        off += n
    return out


def kernel(x, mix_pre_g, mix_post_g, ffn_pre_g, ffn_post_g, ssd_w_in, ssd_conv_w, ssd_conv_b, ssd_dt_bias, ssd_A_log, ssd_D, ssd_norm_w, ssd_w_out, sc_w_in, sc_conv_w, sc_w_out, ffn_w_up, ffn_conv_w, ffn_conv_b, ffn_w_down, loss_target, m_mix_pre_g, m_mix_post_g, m_ffn_pre_g, m_ffn_post_g, m_ssd_w_in, m_ssd_conv_w, m_ssd_conv_b, m_ssd_dt_bias, m_ssd_A_log, m_ssd_D, m_ssd_norm_w, m_ssd_w_out, m_sc_w_in, m_sc_conv_w, m_sc_w_out, m_ffn_w_up, m_ffn_conv_w, m_ffn_conv_b, m_ffn_w_down, v_mix_pre_g, v_mix_post_g, v_ffn_pre_g, v_ffn_post_g, v_ssd_w_in, v_ssd_conv_w, v_ssd_conv_b, v_ssd_dt_bias, v_ssd_A_log, v_ssd_D, v_ssd_norm_w, v_ssd_w_out, v_sc_w_in, v_sc_conv_w, v_sc_w_out, v_ffn_w_up, v_ffn_conv_w, v_ffn_conv_b, v_ffn_w_down):
    me = _my_index()
    x0 = x[0]
    target = loss_target[0]
    row = lambda a: a.reshape(1, -1)

    tr = lambda a: jnp.transpose(a, (0, 2, 1))
    shards = {"ssd_in": tr(ssd_w_in), "ssd_out": ssd_w_out, "sc_in": sc_w_in, "sc_out": sc_w_out,
              "up": tr(ffn_w_up), "down": ffn_w_down}
    col_sharded = ("sc_in",)
    padded_rows = {"ssd_in": SSD_IN_PAD}
    weights = {}

    def shard_bf16(key):
        n, l = key
        return shards[n][l].astype(BF16)

    def store_weights(keys, outs):
        for (n, l), g in zip(keys, outs):
            _, R, C = g.shape
            if n in col_sharded:
                full = jnp.transpose(g, (1, 0, 2)).reshape(R, N_DEV * C)
            else:
                full = g.reshape(N_DEV * R, C)
                if n in padded_rows:
                    full = jnp.pad(full, ((0, padded_rows[n] - N_DEV * R), (0, 0)))
            weights[(n, l)] = full

    fwd_first_half = {
        "ssd_inproj_0": [("up", 0), ("down", 0)],
        "ssd_scan_fwd_0": [("sc_in", 0), ("sc_out", 0), ("up", 1)],
        "ssd_out_fwd_0": [("down", 1)],
        "ffn_fwd_0": [("ssd_in", 1)],
        "sc_fwd_0": [("ssd_out", 1), ("down", 2)],
        "ffn_fwd_1": [("up", 2)],
        "ssd_inproj_1": [("sc_in", 1), ("sc_out", 1), ("down", 3)],
        "ssd_scan_fwd_1": [("up", 3)],
    }
    bwd_sched = {
        "ffn_bwd2_3": [("down", 3)], "sc_bwd2_1": [("sc_out", 1)], "ffn_bwd1_2": [("up", 3)], "ffn_bwd2_2": [("sc_in", 1)],
        "ssd_out_bwd_1": [("down", 2)], "ssd_scan_bwd_1": [("up", 2), ("ssd_out", 1)], "ffn_bwd1_1": [("ssd_in", 1)],
        "ffn_bwd2_1": [("down", 1)], "sc_bwd2_0": [("sc_out", 0)], "ffn_bwd1_0": [("up", 1)], "ffn_bwd2_0": [("sc_in", 0)],
        "ssd_out_bwd_0": [("down", 0)], "ssd_scan_bwd_0": [("up", 0), ("ssd_out", 0)],
    }

    first = [("ssd_in", 0), ("ssd_out", 0)]
    outs = exchange([("ag2", shard_bf16(k)) for k in first]
                    + [("ag", ssd_conv_w), ("ag", sc_conv_w), ("ag", ffn_conv_w)], "ag_first")
    store_weights(first, outs[:2])

    def taps(g):
        _, nl, K, C = g.shape
        return jnp.transpose(g, (1, 2, 0, 3)).reshape(nl, K, N_DEV * C)

    CW_ssd, CW_sc, CW_ffn = taps(outs[2]), taps(outs[3]), taps(outs[4])

    half_done = []

    def fwd(fn, name, *args):
        second = list(half_done)
        starting = fwd_first_half.get(name, [])
        jobs = [("agB", buf) for _, buf in second] + [("agA", shard_bf16(k)) for k in starting]
        res, got = fn(*args, name, jobs)
        store_weights([k for k, _ in second], got[:len(second)])
        half_done[:] = list(zip(starting, got[len(second):]))
        return res

    saved = []
    h = x0
    for i in range(DEPTH):
        j = i // 2
        blk = dict(x_mix=h)
        if i % 2 == 0:
            z, raw, dt_raw, xh, bm, cm = fwd(ssd_inproj, f"ssd_inproj_{j}", h, row(mix_pre_g[i]), weights[("ssd_in", j)],
                                             CW_ssd[j], row(ssd_conv_b[j]))
            y, sprev = fwd(ssd_scan_fwd, f"ssd_scan_fwd_{j}", xh, bm, cm, dt_raw, row(ssd_dt_bias[j]),
                           row(ssd_A_log[j]), row(ssd_D[j]))
            m, h = fwd(ssd_out_fwd, f"ssd_out_fwd_{j}", h, y, z, row(ssd_norm_w[j]), weights[("ssd_out", j)],
                       row(mix_post_g[i]))
            blk.update(z=z, raw=raw, dt_raw=dt_raw, xh=xh, bm=bm, cm=cm, y=y, sprev=sprev, m=m)
        else:
            bcv, m, h = fwd(sc_fwd, f"sc_fwd_{j}", h, row(mix_pre_g[i]), weights[("sc_in", j)], CW_sc[j],
                            weights[("sc_out", j)], row(mix_post_g[i]))
            blk.update(bcv=bcv, m=m)
        blk["x_ffn"] = h
        up, f, h = fwd(ffn_fwd, f"ffn_fwd_{i}", h, row(ffn_pre_g[i]), weights[("up", i)], CW_ffn[i],
                       row(ffn_conv_b[i]), weights[("down", i)], row(ffn_post_g[i]))
        blk.update(up=up, f=f)
        saved.append(blk)

    (loss_dev, dh), _ = loss_fwd_bwd(h, target, "loss")
    loss = lax.psum(loss_dev[0, 0], ("x", "y", "c"))

    parts, recvd = {}, {}

    def bwd(fn, name, *args):
        keys = bwd_sched.get(name, [])
        res, got = fn(*args, name, [("a2a", parts[k]) for k in keys])
        for k, g in zip(keys, got):
            recvd[k] = g
        return res

    g_mix_pre, g_mix_post, g_ffn_pre, g_ffn_post = [None] * DEPTH, [None] * DEPTH, [None] * DEPTH, [None] * DEPTH
    g_ffn_cw, g_ffn_cb = [None] * DEPTH, [None] * DEPTH
    g_ssd_cw, g_ssd_cb, g_ssd_dtb, g_ssd_alog, g_ssd_d, g_ssd_nw = ([None] * 2 for _ in range(6))
    g_sc_cw = [None] * 2
    for i in reversed(range(DEPTH)):
        j = i // 2
        blk = saved[i]
        dup, parts[("down", i)], g_ffn_post[i], g_ffn_cw[i], g_ffn_cb[i] = bwd(
            ffn_bwd1, f"ffn_bwd1_{i}", dh, blk["f"], blk["up"], row(ffn_post_g[i]), weights[("down", i)], CW_ffn[i],
            row(ffn_conv_b[i]))
        dh, parts[("up", i)], g_ffn_pre[i] = bwd(functools.partial(inproj_bwd, transposed=True), f"ffn_bwd2_{i}",
                                                  blk["x_ffn"], row(ffn_pre_g[i]), dup, weights[("up", i)], dh,
                                                  2 * FFN_F // N_DEV)
        if i % 2 == 0:
            dy, dz, parts[("ssd_out", j)], g_mix_post[i], g_ssd_nw[j] = bwd(
                ssd_out_bwd, f"ssd_out_bwd_{j}", dh, blk["m"], blk["y"], blk["z"], row(ssd_norm_w[j]),
                weights[("ssd_out", j)], row(mix_post_g[i]))
            dxh, dbm, dcm, ddt, g_ssd_dtb[j], g_ssd_alog[j], g_ssd_d[j] = bwd(
                ssd_scan_bwd, f"ssd_scan_bwd_{j}", dy, blk["xh"], blk["bm"], blk["cm"], blk["dt_raw"], blk["sprev"],
                row(ssd_dt_bias[j]), row(ssd_A_log[j]), row(ssd_D[j]))
            d_in, g_ssd_cw[j], g_ssd_cb[j] = bwd(ssd_conv_bwd, f"ssd_conv_bwd_{j}", dxh, dbm, dcm, blk["raw"], dz, ddt,
                                                 CW_ssd[j], row(ssd_conv_b[j]))
            dh, parts[("ssd_in", j)], g_mix_pre[i] = bwd(functools.partial(inproj_bwd, transposed=True),
                                                          f"ssd_bwd2_{j}", blk["x_mix"], row(mix_pre_g[i]), d_in,
                                                          weights[("ssd_in", j)], dh, SSD_IN // N_DEV)
        else:
            dbcv, parts[("sc_out", j)], g_mix_post[i], g_sc_cw[j] = bwd(
                sc_bwd1, f"sc_bwd1_{j}", dh, blk["m"], blk["bcv"], row(mix_post_g[i]), weights[("sc_out", j)], CW_sc[j])
            dh, parts[("sc_in", j)], g_mix_pre[i] = bwd(inproj_bwd, f"sc_bwd2_{j}", blk["x_mix"], row(mix_pre_g[i]),
                                                         dbcv, weights[("sc_in", j)], dh, 3 * D_MODEL // N_DEV)
    grad_x = dh[None]

    st = lambda lst: jnp.concatenate(lst, axis=0)
    small_full = [
        st(g_mix_pre), st(g_mix_post), st(g_ffn_pre), st(g_ffn_post),
        jnp.stack(g_ssd_cw), st(g_ssd_cb), st(g_ssd_dtb), st(g_ssd_alog), st(g_ssd_d), st(g_ssd_nw),
        jnp.stack(g_sc_cw), jnp.stack(g_ffn_cw), st(g_ffn_cb),
    ]
    full_shapes = [a.shape for a in small_full]

    recvd[("ssd_in", 0)], small_gathered = exchange([("a2a", parts[("ssd_in", 0)]), ("ag", _pack(small_full))],
                                                    "a2a_last")

    def finish(n, nl, w, m, v):
        return reduce_adamw([recvd[(n, l)] for l in range(nl)], w, m, v, "adamw_" + n)[0]

    r_up = [tr(a) for a in finish("up", DEPTH, tr(ffn_w_up), tr(m_ffn_w_up), tr(v_ffn_w_up))]
    r_down = finish("down", DEPTH, ffn_w_down, m_ffn_w_down, v_ffn_w_down)
    r_ssd_out = finish("ssd_out", 2, ssd_w_out, m_ssd_w_out, v_ssd_w_out)
    r_sc_in = finish("sc_in", 2, sc_w_in, m_sc_w_in, v_sc_w_in)
    r_sc_out = finish("sc_out", 2, sc_w_out, m_sc_w_out, v_sc_w_out)
    r_ssd_in = [tr(a) for a in finish("ssd_in", 2, tr(ssd_w_in), tr(m_ssd_w_in), tr(v_ssd_w_in))]

    summed = small_reduce(small_gathered, "small_reduce")
    (s_mix_pre, s_mix_post, s_ffn_pre, s_ffn_post, s_ssd_cw, s_ssd_cb, s_ssd_dtb, s_ssd_alog, s_ssd_d, s_ssd_nw,
     s_sc_cw, s_ffn_cw, s_ffn_cb) = _unpack(summed, full_shapes)

    def my_cols(a, width):
        return lax.dynamic_slice_in_dim(a, me * width, width, axis=a.ndim - 1)

    s_ssd_cw = my_cols(s_ssd_cw, SSD_CONV // N_DEV)
    s_sc_cw = my_cols(s_sc_cw, D_MODEL // N_DEV)
    s_ffn_cw = my_cols(s_ffn_cw, FFN_F // N_DEV)

    small_g = [s_mix_pre, s_mix_post, s_ffn_pre, s_ffn_post, s_ssd_cw, s_ssd_cb, s_ssd_dtb, s_ssd_alog, s_ssd_d,
               s_ssd_nw, s_sc_cw, s_ffn_cw, s_ffn_cb]
    small_w = [mix_pre_g, mix_post_g, ffn_pre_g, ffn_post_g, ssd_conv_w, ssd_conv_b, ssd_dt_bias, ssd_A_log, ssd_D,
               ssd_norm_w, sc_conv_w, ffn_conv_w, ffn_conv_b]
    small_m = [m_mix_pre_g, m_mix_post_g, m_ffn_pre_g, m_ffn_post_g, m_ssd_conv_w, m_ssd_conv_b, m_ssd_dt_bias,
               m_ssd_A_log, m_ssd_D, m_ssd_norm_w, m_sc_conv_w, m_ffn_conv_w, m_ffn_conv_b]
    small_v = [v_mix_pre_g, v_mix_post_g, v_ffn_pre_g, v_ffn_post_g, v_ssd_conv_w, v_ssd_conv_b, v_ssd_dt_bias,
               v_ssd_A_log, v_ssd_D, v_ssd_norm_w, v_sc_conv_w, v_ffn_conv_w, v_ffn_conv_b]
    local_shapes = [a.shape for a in small_w]
    pd, pm, pv = small_adamw(_pack(small_g), _pack(small_w), _pack(small_m), _pack(small_v), "small_adamw")
    sd = _unpack(pd, local_shapes)
    sm = _unpack(pm, local_shapes)
    sv = _unpack(pv, local_shapes)

    def ordered(small, big):
        (mix_pre, mix_post, ffn_pre, ffn_post, ssd_cw, ssd_cb, dtb, alog, dsk, nw, sc_cw, ffn_cw, ffn_cb) = small
        (b_ssd_in, b_ssd_out, b_sc_in, b_sc_out, b_up, b_down) = big
        return [mix_pre, mix_post, ffn_pre, ffn_post, b_ssd_in, ssd_cw, ssd_cb, dtb, alog, dsk, nw, b_ssd_out,
                b_sc_in, sc_cw, b_sc_out, b_up, ffn_cw, ffn_cb, b_down]

    bigs = [r_ssd_in, r_ssd_out, r_sc_in, r_sc_out, r_up, r_down]
    grads = ordered(small_g, [r[0] for r in bigs])
    deltas = ordered(sd, [r[1] for r in bigs])
    new_m = ordered(sm, [r[2] for r in bigs])
    new_v = ordered(sv, [r[3] for r in bigs])
    return (loss, grad_x, *grads, *deltas, *new_m, *new_v)
```

```python
import functools

import jax
import jax.numpy as jnp
from jax import lax
from jax.experimental import pallas as pl
from jax.experimental.pallas import tpu as pltpu

F32 = jnp.float32
BF16 = jnp.bfloat16

EPS = 1e-6
D_MODEL = 1024
DEPTH = 4
N_DEV = 8
CHUNK = 64
SSD_DI = 2048
SSD_H = 32
SSD_P = 64
SSD_G = 8
SSD_R = SSD_H // SSD_G
SSD_N = 128
SSD_CONV = SSD_DI + 2 * SSD_G * SSD_N
SSD_IN = SSD_DI + SSD_CONV + SSD_H
LANES = 128
SSD_IN_PAD = -(-SSD_IN // LANES) * LANES
SSD_KW = 4
SC_KW = 3
FFN_F = 2816
FFN_KW = 3
SCAN_CPS = 2
TL = 256
HALO = 8
CONV_RC = 128
CONV_LC = 256
VMEM_LIMIT = 60 * 1024 * 1024

ADAM_LR = 0.001
ADAM_B1 = 0.9
ADAM_B2 = 0.999
ADAM_EPS = 1e-08
ADAM_WD = 0.01
ADAM_STEP = 10

MESH = pl.DeviceIdType.MESH


def _rms(x, g):
    r = lax.rsqrt(jnp.mean(x * x, axis=-1, keepdims=True) + EPS)
    return x * r * g


def _rms_bwd(x, g, dy):
    r = lax.rsqrt(jnp.mean(x * x, axis=-1, keepdims=True) + EPS)
    xh = x * r
    dg = jnp.sum(dy * xh, axis=0, keepdims=True)
    dxh = dy * g
    dx = r * (dxh - xh * jnp.mean(dxh * xh, axis=-1, keepdims=True))
    return dx, dg


def _mm(a, b):
    return jnp.dot(a, b, preferred_element_type=F32)


def _mm_nt(a, b):
    return lax.dot_general(a, b, (((1,), (1,)), ((), ())), preferred_element_type=F32)


def _mm_tn(a, b):
    return lax.dot_general(a, b, (((0,), (0,)), ((), ())), preferred_element_type=F32)


def _silu_parts(x):
    sg = jax.nn.sigmoid(x)
    return x * sg, sg * (1.0 + x * (1.0 - sg))


def _rows_from(a, s, tl):
    if s % HALO == 0:
        return a[s:s + tl]
    return pltpu.roll(a, a.shape[0] - s, 0)[0:tl]


def _taps(ext, kw, tl):
    base = HALO - (kw - 1)
    return [_rows_from(ext, base + j, tl) for j in range(kw)]


def _conv_fwd(taps, w):
    out = taps[0] * w[0:1]
    for j in range(1, len(taps)):
        out = out + taps[j] * w[j:j + 1]
    return out


def _conv_bwd_in(extd, w, kw, tl):
    out = _rows_from(extd, kw - 1, tl) * w[0:1]
    for j in range(1, kw):
        out = out + _rows_from(extd, kw - 1 - j, tl) * w[j:j + 1]
    return out


def _conv_bwd_w(taps, dy):
    return jnp.concatenate([jnp.sum(dy * t, axis=0, keepdims=True) for t in taps], axis=0)


def _emit_row_shards(acc_ref, out_ref, stage_ref):
    rows = out_ref.shape[1]
    for k in range(N_DEV):
        stage_ref[...] = acc_ref[k * rows:(k + 1) * rows, :].astype(BF16)
        pltpu.sync_copy(stage_ref, out_ref.at[k])


def _emit_col_shards(acc_ref, out_ref, stage_ref):
    cols = out_ref.shape[2]
    for k in range(N_DEV):
        stage_ref[...] = acc_ref[:, k * cols:(k + 1) * cols].astype(BF16)
        pltpu.sync_copy(stage_ref, out_ref.at[k])


def _res(shape):
    nd = len(shape)
    return pl.BlockSpec(shape, lambda i: (0,) * nd, pipeline_mode=pl.Buffered(1))


def _small(shape):
    nd = len(shape)
    return pl.BlockSpec(shape, lambda i: (0,) * nd)


def _tile(n):
    return pl.BlockSpec((TL, n), lambda i: (i, 0))


def _rtile(n, nt):
    return pl.BlockSpec((TL, n), lambda i: (nt - 1 - i, 0))


def _halo_before(n, nt, reverse):
    per = TL // HALO
    if reverse:
        return pl.BlockSpec((HALO, n), lambda i: (jnp.maximum((nt - 1 - i) * per - 1, 0), 0))
    return pl.BlockSpec((HALO, n), lambda i: (jnp.maximum(i * per - 1, 0), 0))


_ANY = pl.BlockSpec(memory_space=pl.ANY)


def _sds(shape, dtype=F32):
    return jax.ShapeDtypeStruct(shape, dtype)


def _peer(k):
    x, y, c = lax.axis_index("x"), lax.axis_index("y"), lax.axis_index("c")
    px = x ^ (k >> 2)
    py = y ^ ((k >> 1) & 1)
    pc = c ^ (k & 1)
    return (px, py, pc), 4 * px + 2 * py + pc


def _my_index():
    return 4 * lax.axis_index("x") + 2 * lax.axis_index("y") + lax.axis_index("c")


SIBLING = 1
SAME_CORE_CHIPS = (2, 4, 6)


def _job_copies(kind, src_ref, out_ref, send_sems, recv_sems, local_sems, j):
    me = _my_index()
    sends, recvs = [], []

    def pair(pattern, sem, src, put_slot, get_slot):
        dev, _ = _peer(pattern)
        sems = dict(send_sem=send_sems.at[j, sem], recv_sem=recv_sems.at[j, sem], device_id=dev, device_id_type=MESH)
        sends.append(pltpu.make_async_remote_copy(src_ref=src, dst_ref=out_ref.at[put_slot], **sems))
        recvs.append(pltpu.make_async_remote_copy(src_ref=src, dst_ref=out_ref.at[get_slot], **sems))

    if kind == "agB":
        for k in SAME_CORE_CHIPS:
            _, mine_from_k = _peer(k)
            _, sib_from_k = _peer(k | SIBLING)
            pair(SIBLING, k, out_ref.at[mine_from_k], mine_from_k, sib_from_k)
        return None, sends, recvs
    patterns = (SIBLING,) + SAME_CORE_CHIPS if kind == "agA" else range(1, N_DEV)
    mine = src_ref.at[me] if kind == "a2a" else src_ref
    local = pltpu.make_async_copy(mine, out_ref.at[me], local_sems.at[j])
    for k in patterns:
        _, idx = _peer(k)
        pair(k, k - 1, src_ref.at[idx] if kind == "a2a" else src_ref, me, idx)
    return local, sends, recvs


def _pcall(body, name, grid, in_specs, out_specs, out_shape, scratch_shapes, args, jobs=()):
    n_in, n_out, nj = len(in_specs), len(out_specs), len(jobs)
    last = grid[0] - 1
    kinds = [k for k, _ in jobs]

    def wrapped(*refs):
        ins = refs[:n_in]
        csrc = refs[n_in:n_in + nj]
        outs = refs[n_in + nj:n_in + nj + n_out]
        cout = refs[n_in + nj + n_out:n_in + 2 * nj + n_out]
        rest = refs[n_in + 2 * nj + n_out:]

        def copies(j, kind):
            return _job_copies(kind, csrc[j], cout[j], send_sems, recv_sems, local_sems, j)

        def start(j, kind):
            local, sends, _ = copies(j, kind)
            if local is not None:
                local.start()
            for cp in sends:
                cp.start()

        def finish(j, kind, arrivals_only=False):
            local, sends, recvs = copies(j, kind)
            for cp in recvs:
                cp.wait_recv()
            if not arrivals_only:
                for cp in sends:
                    cp.wait_send()
                if local is not None:
                    local.wait()

        if nj:
            scratch, (send_sems, recv_sems, local_sems) = rest[:-3], rest[-3:]
            i = pl.program_id(0)

            @pl.when(i == 0)
            def _():
                for j in range(nj):
                    start(j, "agA" if kinds[j] == "ag2" else kinds[j])
                for j in range(nj):
                    if kinds[j] == "ag2":
                        finish(j, "agA", arrivals_only=True)
                        start(j, "agB")
        else:
            scratch = rest
        body(*ins, *outs, *scratch)
        if nj:
            @pl.when(i == last)
            def _():
                for j in range(nj):
                    if kinds[j] == "ag2":
                        finish(j, "agB")
                        _, sends, _ = copies(j, "agA")
                        for cp in sends:
                            cp.wait_send()
                        copies(j, "agA")[0].wait()
                    else:
                        finish(j, kinds[j])

    job_shapes = []
    aliases = {}
    for j, (kind, s) in enumerate(jobs):
        shp = (N_DEV,) + tuple(s.shape) if kind in ("ag", "agA", "ag2") else tuple(s.shape)
        job_shapes.append(_sds(shp, s.dtype))
        if kind == "agB":
            aliases[n_in + j] = n_out + j
    sems = [pltpu.SemaphoreType.DMA((nj, N_DEV - 1)), pltpu.SemaphoreType.DMA((nj, N_DEV - 1)),
            pltpu.SemaphoreType.DMA((nj,))] if nj else []
    res = pl.pallas_call(
        wrapped, name=name, grid=grid,
        in_specs=list(in_specs) + [_ANY] * nj,
        out_specs=list(out_specs) + [_ANY] * nj,
        out_shape=list(out_shape) + job_shapes,
        scratch_shapes=list(scratch_shapes) + sems,
        input_output_aliases=aliases,
        compiler_params=pltpu.CompilerParams(dimension_semantics=("arbitrary",), vmem_limit_bytes=VMEM_LIMIT,
                                             has_side_effects=bool(nj)),
    )(*args, *[s for _, s in jobs])
    return list(res[:n_out]), list(res[n_out:])


def exchange(jobs, name):
    def body(o_ref):
        o_ref[...] = jnp.zeros_like(o_ref)

    _, outs = _pcall(body, name, (1,), [], [_small((8, LANES))], [_sds((8, LANES))], [], [], jobs)
    return outs


def ffn_fwd(x, g_pre, w_up, conv_w, conv_b, w_down, g_post, name, jobs=()):
    L = x.shape[0]
    nt = L // TL
    F = FFN_F

    def body(x_ref, gpre_ref, wup_ref, cw_ref, cb_ref, wdn_ref, gpost_ref, up_ref, f_ref, xn_ref, carry_ref):
        i = pl.program_id(0)

        @pl.when(i == 0)
        def _():
            carry_ref[...] = jnp.zeros_like(carry_ref)

        x = x_ref[...]
        h = _rms(x, gpre_ref[...]).astype(BF16)
        up = _mm_nt(h, wup_ref[...])
        up_ref[...] = up
        ug = up[:, :F]
        val = up[:, F:]
        ext = jnp.concatenate([carry_ref[...], ug], axis=0)
        gate = _conv_fwd(_taps(ext, FFN_KW, TL), cw_ref[...]) + cb_ref[...]
        carry_ref[...] = ug[TL - HALO:, :]
        a = (gate * jax.nn.sigmoid(gate) * val).astype(BF16)
        f = _mm(a, wdn_ref[...])
        f_ref[...] = f
        xn_ref[...] = x + _rms(f, gpost_ref[...])

    return _pcall(body, name, (nt,),
                  [_tile(D_MODEL), _small((1, D_MODEL)), _res((2 * F, D_MODEL)), _small((FFN_KW, F)), _small((1, F)),
                   _res((F, D_MODEL)), _small((1, D_MODEL))],
                  [_tile(2 * F), _tile(D_MODEL), _tile(D_MODEL)],
                  [_sds((L, 2 * F)), _sds((L, D_MODEL)), _sds((L, D_MODEL))],
                  [pltpu.VMEM((HALO, F), F32)],
                  [x, g_pre, w_up, conv_w, conv_b, w_down, g_post], jobs)


def ffn_bwd1(dxo, f, up, g_post, w_down, conv_w, conv_b, name, jobs=()):
    L = dxo.shape[0]
    nt = L // TL
    F = FFN_F
    rows = F // N_DEV

    def body(dxo_ref, f_ref, up_ref, halo_ref, gpost_ref, wdn_ref, cw_ref, cb_ref,
             dup_ref, dwdn_ref, dgp_ref, dcw_ref, dcb_ref, acc_ref, carry_ref, stage_ref):
        i = pl.program_id(0)
        t = nt - 1 - i

        @pl.when(i == 0)
        def _():
            acc_ref[...] = jnp.zeros_like(acc_ref)
            carry_ref[...] = jnp.zeros_like(carry_ref)
            dgp_ref[...] = jnp.zeros_like(dgp_ref)
            dcw_ref[...] = jnp.zeros_like(dcw_ref)
            dcb_ref[...] = jnp.zeros_like(dcb_ref)

        df, dgp = _rms_bwd(f_ref[...], gpost_ref[...], dxo_ref[...])
        dgp_ref[...] += dgp
        dfb = df.astype(BF16)
        da = _mm_nt(dfb, wdn_ref[...])
        up = up_ref[...]
        ug = up[:, :F]
        val = up[:, F:]
        halo = jnp.where(t == 0, 0.0, halo_ref[...])
        ext = jnp.concatenate([halo, ug], axis=0)
        w = cw_ref[...]
        taps = _taps(ext, FFN_KW, TL)
        gate = _conv_fwd(taps, w) + cb_ref[...]
        s, ds = _silu_parts(gate)
        acc_ref[...] += _mm_tn((s * val).astype(BF16), dfb)
        dval = da * s
        dgate = da * val * ds
        dcb_ref[...] += jnp.sum(dgate, axis=0, keepdims=True)
        dcw_ref[...] += _conv_bwd_w(taps, dgate)
        extd = jnp.concatenate([dgate, carry_ref[...]], axis=0)
        dug = _conv_bwd_in(extd, w, FFN_KW, TL)
        carry_ref[...] = dgate[:HALO, :]
        dup_ref[...] = jnp.concatenate([dug, dval], axis=1).astype(BF16)

        @pl.when(i == nt - 1)
        def _():
            _emit_row_shards(acc_ref, dwdn_ref, stage_ref)

    return _pcall(body, name, (nt,),
                  [_rtile(D_MODEL, nt), _rtile(D_MODEL, nt), _rtile(2 * F, nt), _halo_before(F, nt, True),
                   _small((1, D_MODEL)), _res((F, D_MODEL)), _small((FFN_KW, F)), _small((1, F))],
                  [_rtile(2 * F, nt), _ANY, _small((1, D_MODEL)), _small((FFN_KW, F)), _small((1, F))],
                  [_sds((L, 2 * F), BF16), _sds((N_DEV, rows, D_MODEL), BF16), _sds((1, D_MODEL)),
                   _sds((FFN_KW, F)), _sds((1, F))],
                  [pltpu.VMEM((F, D_MODEL), F32), pltpu.VMEM((HALO, F), F32), pltpu.VMEM((rows, D_MODEL), BF16)],
                  [dxo, f, up, up, g_post, w_down, conv_w, conv_b], jobs)


def inproj_bwd(x, g_pre, d, w, dxo, cols, name, jobs=(), transposed=False):
    L = x.shape[0]
    nt = L // TL
    N = d.shape[1]
    w_shape = (N, D_MODEL) if transposed else (D_MODEL, N)
    shard_shape = (cols, D_MODEL) if transposed else (D_MODEL, cols)

    def body(x_ref, g_ref, d_ref, w_ref, dxo_ref, dx_ref, dw_ref, dg_ref, acc_ref, stage_ref):
        i = pl.program_id(0)

        @pl.when(i == 0)
        def _():
            acc_ref[...] = jnp.zeros_like(acc_ref)
            dg_ref[...] = jnp.zeros_like(dg_ref)

        x = x_ref[...]
        g = g_ref[...]
        d = d_ref[...]
        h = _rms(x, g).astype(BF16)
        if transposed:
            dh = _mm(d, w_ref[...])
            acc_ref[...] += _mm_tn(d, h)
        else:
            dh = _mm_nt(d, w_ref[...])
            acc_ref[...] += _mm_tn(h, d)
        dxn, dg = _rms_bwd(x, g, dh)
        dx_ref[...] = dxo_ref[...] + dxn
        dg_ref[...] += dg

        @pl.when(i == nt - 1)
        def _():
            (_emit_row_shards if transposed else _emit_col_shards)(acc_ref, dw_ref, stage_ref)

    return _pcall(body, name, (nt,),
                  [_tile(D_MODEL), _small((1, D_MODEL)), _tile(N), _res(w_shape), _tile(D_MODEL)],
                  [_tile(D_MODEL), _ANY, _small((1, D_MODEL))],
                  [_sds((L, D_MODEL)), _sds((N_DEV,) + shard_shape, BF16), _sds((1, D_MODEL))],
                  [pltpu.VMEM(w_shape, F32), pltpu.VMEM(shard_shape, BF16)],
                  [x, g_pre, d, w, dxo], jobs)


def sc_fwd(x, g_pre, w_in, conv_w, w_out, g_post, name, jobs=()):
    L = x.shape[0]
    nt = L // TL
    W = D_MODEL

    def body(x_ref, gpre_ref, win_ref, cw_ref, wout_ref, gpost_ref, bcv_ref, m_ref, xn_ref, carry_ref):
        i = pl.program_id(0)

        @pl.when(i == 0)
        def _():
            carry_ref[...] = jnp.zeros_like(carry_ref)

        x = x_ref[...]
        h = _rms(x, gpre_ref[...]).astype(BF16)
        bcv = _mm(h, win_ref[...])
        bcv_ref[...] = bcv
        gb = bcv[:, :W]
        p = bcv[:, W:2 * W] * bcv[:, 2 * W:]
        ext = jnp.concatenate([carry_ref[...], p], axis=0)
        u = _conv_fwd(_taps(ext, SC_KW, TL), cw_ref[...])
        carry_ref[...] = p[TL - HALO:, :]
        m = _mm((gb * u).astype(BF16), wout_ref[...])
        m_ref[...] = m
        xn_ref[...] = x + _rms(m, gpost_ref[...])

    return _pcall(body, name, (nt,),
                  [_tile(W), _small((1, W)), _res((W, 3 * W)), _small((SC_KW, W)), _res((W, W)), _small((1, W))],
                  [_tile(3 * W), _tile(W), _tile(W)],
                  [_sds((L, 3 * W)), _sds((L, W)), _sds((L, W))],
                  [pltpu.VMEM((HALO, W), F32)],
                  [x, g_pre, w_in, conv_w, w_out, g_post], jobs)


def sc_bwd1(dxo, m, bcv, g_post, w_out, conv_w, name, jobs=()):
    L = dxo.shape[0]
    nt = L // TL
    W = D_MODEL
    rows = W // N_DEV

    def body(dxo_ref, m_ref, bcv_ref, halo_ref, gpost_ref, wout_ref, cw_ref,
             dbcv_ref, dwout_ref, dgp_ref, dcw_ref, acc_ref, carry_ref, stage_ref):
        i = pl.program_id(0)
        t = nt - 1 - i

        @pl.when(i == 0)
        def _():
            acc_ref[...] = jnp.zeros_like(acc_ref)
            carry_ref[...] = jnp.zeros_like(carry_ref)
            dgp_ref[...] = jnp.zeros_like(dgp_ref)
            dcw_ref[...] = jnp.zeros_like(dcw_ref)

        dm, dgp = _rms_bwd(m_ref[...], gpost_ref[...], dxo_ref[...])
        dgp_ref[...] += dgp
        dmb = dm.astype(BF16)
        dq = _mm_nt(dmb, wout_ref[...])
        bcv = bcv_ref[...]
        gb = bcv[:, :W]
        gc = bcv[:, W:2 * W]
        v = bcv[:, 2 * W:]
        hb = halo_ref[...]
        halo = jnp.where(t == 0, 0.0, hb[:, W:2 * W] * hb[:, 2 * W:])
        ext = jnp.concatenate([halo, gc * v], axis=0)
        w = cw_ref[...]
        taps = _taps(ext, SC_KW, TL)
        u = _conv_fwd(taps, w)
        acc_ref[...] += _mm_tn((gb * u).astype(BF16), dmb)
        dgb = dq * u
        du = dq * gb
        dcw_ref[...] += _conv_bwd_w(taps, du)
        extd = jnp.concatenate([du, carry_ref[...]], axis=0)
        dp = _conv_bwd_in(extd, w, SC_KW, TL)
        carry_ref[...] = du[:HALO, :]
        dbcv_ref[...] = jnp.concatenate([dgb, dp * v, dp * gc], axis=1).astype(BF16)

        @pl.when(i == nt - 1)
        def _():
            _emit_row_shards(acc_ref, dwout_ref, stage_ref)

    return _pcall(body, name, (nt,),
                  [_rtile(W, nt), _rtile(W, nt), _rtile(3 * W, nt), _halo_before(3 * W, nt, True),
                   _small((1, W)), _res((W, W)), _small((SC_KW, W))],
                  [_rtile(3 * W, nt), _ANY, _small((1, W)), _small((SC_KW, W))],
                  [_sds((L, 3 * W), BF16), _sds((N_DEV, rows, W), BF16), _sds((1, W)), _sds((SC_KW, W))],
                  [pltpu.VMEM((W, W), F32), pltpu.VMEM((HALO, W), F32), pltpu.VMEM((rows, W), BF16)],
                  [dxo, m, bcv, bcv, g_post, w_out, conv_w], jobs)


def ssd_inproj(x, g_pre, w_in, conv_w, conv_b, name, jobs=()):
    L = x.shape[0]
    nt = L // TL

    def body(x_ref, gpre_ref, win_ref, cw_ref, cb_ref, z_ref, raw_ref, dt_ref, xh_ref, bm_ref, cm_ref, carry_ref):
        i = pl.program_id(0)

        @pl.when(i == 0)
        def _():
            carry_ref[...] = jnp.zeros_like(carry_ref)

        h = _rms(x_ref[...], gpre_ref[...]).astype(BF16)
        zx = _mm_nt(h, win_ref[...])
        z_ref[...] = zx[:, :SSD_DI]
        raw = zx[:, SSD_DI:SSD_DI + SSD_CONV]
        raw_ref[...] = raw
        dt_ref[...] = zx[:, SSD_DI + SSD_CONV:SSD_IN]
        ext = jnp.concatenate([carry_ref[...], raw], axis=0)
        pre = _conv_fwd(_taps(ext, SSD_KW, TL), cw_ref[...]) + cb_ref[...]
        carry_ref[...] = raw[TL - HALO:, :]
        act = pre * jax.nn.sigmoid(pre)
        for hh in range(SSD_H):
            xh_ref[hh] = act[:, hh * SSD_P:(hh + 1) * SSD_P]
        for g in range(SSD_G):
            bm_ref[g] = act[:, SSD_DI + g * SSD_N:SSD_DI + (g + 1) * SSD_N]
            cm_ref[g] = act[:, SSD_DI + (SSD_G + g) * SSD_N:SSD_DI + (SSD_G + g + 1) * SSD_N]

    return _pcall(body, name, (nt,),
                  [_tile(D_MODEL), _small((1, D_MODEL)), _res((SSD_IN_PAD, D_MODEL)), _small((SSD_KW, SSD_CONV)),
                   _small((1, SSD_CONV))],
                  [_tile(SSD_DI), _tile(SSD_CONV), _tile(SSD_H),
                   pl.BlockSpec((SSD_H, TL, SSD_P), lambda i: (0, i, 0)),
                   pl.BlockSpec((SSD_G, TL, SSD_N), lambda i: (0, i, 0)),
                   pl.BlockSpec((SSD_G, TL, SSD_N), lambda i: (0, i, 0))],
                  [_sds((L, SSD_DI)), _sds((L, SSD_CONV)), _sds((L, SSD_H)), _sds((SSD_H, L, SSD_P)),
                   _sds((SSD_G, L, SSD_N)), _sds((SSD_G, L, SSD_N))],
                  [pltpu.VMEM((HALO, SSD_CONV), F32)],
                  [x, g_pre, w_in, conv_w, conv_b], jobs)


def _per_head(v, heads):
    return jnp.stack([v[:, h:h + 1] for h in heads], axis=0)


def _heads_to_lanes(cols):
    return jnp.concatenate(cols, axis=1)


def _rep_heads(v):
    g, a, b = v.shape
    return jnp.broadcast_to(v[:, None], (g, SSD_R, a, b)).reshape(g * SSD_R, a, b)


def _sum_heads(v):
    h, a, b = v.shape
    return v.reshape(SSD_G, SSD_R, a, b).sum(axis=1)


def _chunk_terms(dtr, bias, a_log):
    T = CHUNK
    dt = jax.nn.softplus(dtr + bias)
    a_head = -jnp.exp(a_log)
    ii = lax.broadcasted_iota(jnp.int32, (T, T), 0)
    jj = lax.broadcasted_iota(jnp.int32, (T, T), 1)
    tri = ii >= jj
    cs = jnp.dot(tri.astype(F32), dt * a_head, precision=lax.Precision.HIGHEST, preferred_element_type=F32)
    return dict(dt=dt, a_head=a_head, tri=tri, cs=cs, cs_t=cs.T)


def _head_terms(ct, heads):
    T = CHUNK
    cs, cs_t, tri = ct["cs"], ct["cs_t"], ct["tri"]
    csc = _per_head(cs, heads)
    csr = jnp.stack([cs_t[h:h + 1, :] for h in heads], axis=0)
    cl = _per_head(cs[T - 1:T, :], heads)
    lmat = jnp.exp(jnp.where(tri[None], csc - csr, -jnp.inf))
    return dict(dtc=_per_head(ct["dt"], heads), lmat=lmat, ecs=jnp.exp(csc), dsc=jnp.exp(cl - csc), cdc=jnp.exp(cl))


def ssd_scan_fwd(xh, bm, cm, dt_raw, dt_bias, a_log, d_skip, name, jobs=()):
    L = xh.shape[1]
    nc = L // CHUNK
    T = CHUNK
    TS = SCAN_CPS * T

    def body(xh_ref, bm_ref, cm_ref, dt_ref, bias_ref, alog_ref, dsk_ref, y_ref, sp_ref, st_ref):
        c = pl.program_id(0)

        @pl.when(c == 0)
        def _():
            st_ref[...] = jnp.zeros_like(st_ref)

        heads = range(SSD_H)
        dh = _per_head(dsk_ref[...], heads)
        s = st_ref[...]
        for k in range(SCAN_CPS):
            rows = slice(k * T, (k + 1) * T)
            ht = _head_terms(_chunk_terms(dt_ref[rows, :], bias_ref[...], alog_ref[...]), heads)
            x = xh_ref[:, rows, :]
            bgb = bm_ref[:, rows, :].astype(BF16)
            cgb = cm_ref[:, rows, :].astype(BF16)
            bh = _rep_heads(bgb)
            ch = _rep_heads(cgb)
            xt = x * ht["dtc"]
            cb = jnp.einsum("gln,gsn->gls", cgb, bgb, preferred_element_type=F32)
            mb = (_rep_heads(cb) * ht["lmat"]).astype(BF16)
            yd = jnp.einsum("hls,hsp->hlp", mb, xt.astype(BF16), preferred_element_type=F32)
            sb = s.astype(BF16)
            yo = jnp.einsum("hln,hpn->hlp", ch, sb, preferred_element_type=F32) * ht["ecs"]
            y_ref[:, rows, :] = yd + yo + x * dh
            sp_ref[k] = sb
            xd = (xt * ht["dsc"]).astype(BF16)
            s = s * ht["cdc"] + jnp.einsum("htp,htn->hpn", xd, bh, preferred_element_type=F32)
        st_ref[...] = s

    hd = pl.BlockSpec((SSD_H, TS, SSD_P), lambda c: (0, c, 0))
    gr = pl.BlockSpec((SSD_G, TS, SSD_N), lambda c: (0, c, 0))
    return _pcall(body, name, (nc // SCAN_CPS,),
                  [hd, gr, gr, pl.BlockSpec((TS, SSD_H), lambda c: (c, 0)),
                   _small((1, SSD_H)), _small((1, SSD_H)), _small((1, SSD_H))],
                  [hd, pl.BlockSpec((SCAN_CPS, SSD_H, SSD_P, SSD_N), lambda c: (c, 0, 0, 0))],
                  [_sds((SSD_H, L, SSD_P)), _sds((nc, SSD_H, SSD_P, SSD_N), BF16)],
                  [pltpu.VMEM((SSD_H, SSD_P, SSD_N), F32)],
                  [xh, bm, cm, dt_raw, dt_bias, a_log, d_skip], jobs)


def ssd_scan_bwd(dy, xh, bm, cm, dt_raw, sprev, dt_bias, a_log, d_skip, name, jobs=()):
    L = xh.shape[1]
    nc = L // CHUNK
    T = CHUNK

    def body(dy_ref, xh_ref, bm_ref, cm_ref, dt_ref, sp_ref, bias_ref, alog_ref, dsk_ref,
             dxh_ref, dbm_ref, dcm_ref, ddt_ref, dbias_ref, dalog_ref, ddsk_ref, g_ref):
        i = pl.program_id(0)

        @pl.when(i == 0)
        def _():
            g_ref[...] = jnp.zeros_like(g_ref)
            dbias_ref[...] = jnp.zeros_like(dbias_ref)
            dalog_ref[...] = jnp.zeros_like(dalog_ref)
            ddsk_ref[...] = jnp.zeros_like(ddsk_ref)

        bias = bias_ref[...]
        heads = range(SSD_H)
        dh = _per_head(dsk_ref[...], heads)
        g = g_ref[...]
        for k in reversed(range(SCAN_CPS)):
            rows = slice(k * T, (k + 1) * T)
            dtr = dt_ref[rows, :]
            ct = _chunk_terms(dtr, bias, alog_ref[...])
            dt, a_head, tri = ct["dt"], ct["a_head"], ct["tri"]
            ht = _head_terms(ct, heads)
            dtc, lmat, ecs, dsc, cdc = ht["dtc"], ht["lmat"], ht["ecs"], ht["dsc"], ht["cdc"]
            x = xh_ref[:, rows, :]
            dyv = dy_ref[:, rows, :]
            dyb = dyv.astype(BF16)
            bgb = bm_ref[:, rows, :].astype(BF16)
            cgb = cm_ref[:, rows, :].astype(BF16)
            bh = _rep_heads(bgb)
            ch = _rep_heads(cgb)
            sb = sp_ref[k]
            gb = g.astype(BF16)
            xt = x * dtc
            xtb = xt.astype(BF16)
            mf = _rep_heads(jnp.einsum("gln,gsn->gls", cgb, bgb, preferred_element_type=F32)) * lmat
            mb = mf.astype(BF16)
            ddsk = jnp.sum(dyv * x, axis=(1, 2), keepdims=True)
            dx = dyv * dh
            yo_raw = jnp.einsum("hln,hpn->hlp", ch, sb, preferred_element_type=F32)
            w1 = dyv * ecs
            w1b = w1.astype(BF16)
            ds_off = jnp.einsum("hlp,hln->hpn", w1b, ch, preferred_element_type=F32)
            dch = jnp.einsum("hlp,hpn->hln", w1b, sb, preferred_element_type=F32)
            dcs_c = jnp.sum(w1 * yo_raw, axis=2, keepdims=True)
            dm = jnp.einsum("hlp,hsp->hls", dyb, xtb, preferred_element_type=F32)
            dxt = jnp.einsum("hls,hlp->hsp", mb, dyb, preferred_element_type=F32)
            dcbb = _sum_heads(dm * lmat).astype(BF16)
            dseg = dm * mf
            dcs_c = dcs_c + jnp.sum(dseg, axis=2, keepdims=True)
            dcs_r = -jnp.sum(dseg, axis=1, keepdims=True)
            dc = jnp.einsum("gls,gsn->gln", dcbb, bgb, preferred_element_type=F32) + _sum_heads(dch)
            db = jnp.einsum("gls,gln->gsn", dcbb, cgb, preferred_element_type=F32)
            xd = xt * dsc
            dxd = jnp.einsum("htn,hpn->htp", bh, gb, preferred_element_type=F32)
            db = db + _sum_heads(jnp.einsum("htp,hpn->htn", xd.astype(BF16), gb, preferred_element_type=F32))
            dxt = dxt + dxd * dsc
            d_ds = jnp.sum(dxd * xt, axis=2, keepdims=True)
            d_cd = jnp.sum(g * sb.astype(F32), axis=(1, 2), keepdims=True)
            g = g * cdc + ds_off
            t1 = d_ds * dsc
            dcs_c = dcs_c - t1
            dcl = jnp.sum(t1, axis=1, keepdims=True) + d_cd * cdc
            ddt_c = jnp.sum(dxt * x, axis=2, keepdims=True)
            dxh_ref[:, rows, :] = dx + dxt * dtc
            dbm_ref[:, rows, :] = db
            dcm_ref[:, rows, :] = dc
            lanes = lambda v: _heads_to_lanes([v[h] for h in heads])
            rows_t = jnp.concatenate([dcs_r[h] for h in heads], axis=0).T
            last = (lax.broadcasted_iota(jnp.int32, (T, 1), 0) == T - 1).astype(F32)
            dcs = lanes(dcs_c) + rows_t + last * lanes(dcl)
            da = lax.dot_general(tri.astype(F32), dcs, (((0,), (0,)), ((), ())),
                                 precision=lax.Precision.HIGHEST, preferred_element_type=F32)
            ddt = da * a_head + lanes(ddt_c)
            dalog_ref[...] += jnp.sum(da * dt, axis=0, keepdims=True)
            ddtr = ddt * jax.nn.sigmoid(dtr + bias)
            ddt_ref[rows, :] = ddtr
            dbias_ref[...] += jnp.sum(ddtr, axis=0, keepdims=True)
            ddsk_ref[...] += lanes(ddsk)
        g_ref[...] = g

        @pl.when(i == nb - 1)
        def _():
            dalog_ref[...] = dalog_ref[...] * (-jnp.exp(alog_ref[...]))

    nb = nc // SCAN_CPS
    TS = SCAN_CPS * T
    hd = pl.BlockSpec((SSD_H, TS, SSD_P), lambda i: (0, nb - 1 - i, 0))
    gr = pl.BlockSpec((SSD_G, TS, SSD_N), lambda i: (0, nb - 1 - i, 0))
    tk = pl.BlockSpec((TS, SSD_H), lambda i: (nb - 1 - i, 0))
    return _pcall(body, name, (nb,),
                  [hd, hd, gr, gr, tk, pl.BlockSpec((SCAN_CPS, SSD_H, SSD_P, SSD_N), lambda i: (nb - 1 - i, 0, 0, 0)),
                   _small((1, SSD_H)), _small((1, SSD_H)), _small((1, SSD_H))],
                  [hd, gr, gr, tk, _small((1, SSD_H)), _small((1, SSD_H)), _small((1, SSD_H))],
                  [_sds((SSD_H, L, SSD_P)), _sds((SSD_G, L, SSD_N)), _sds((SSD_G, L, SSD_N)), _sds((L, SSD_H)),
                   _sds((1, SSD_H)), _sds((1, SSD_H)), _sds((1, SSD_H))],
                  [pltpu.VMEM((SSD_H, SSD_P, SSD_N), F32)],
                  [dy, xh, bm, cm, dt_raw, sprev, dt_bias, a_log, d_skip], jobs)


def _heads_to_tokens(y_ref):
    return jnp.concatenate([y_ref[h] for h in range(SSD_H)], axis=1)


def ssd_out_fwd(x, y, z, norm_w, w_out, g_post, name, jobs=()):
    L = x.shape[0]
    nt = L // TL

    def body(x_ref, y_ref, z_ref, nw_ref, wout_ref, gpost_ref, m_ref, xn_ref):
        z = z_ref[...]
        yg = _heads_to_tokens(y_ref) * (z * jax.nn.sigmoid(z))
        yn = _rms(yg, nw_ref[...]).astype(BF16)
        m = _mm(yn, wout_ref[...])
        m_ref[...] = m
        xn_ref[...] = x_ref[...] + _rms(m, gpost_ref[...])

    return _pcall(body, name, (nt,),
                  [_tile(D_MODEL), pl.BlockSpec((SSD_H, TL, SSD_P), lambda i: (0, i, 0)), _tile(SSD_DI),
                   _small((1, SSD_DI)), _res((SSD_DI, D_MODEL)), _small((1, D_MODEL))],
                  [_tile(D_MODEL), _tile(D_MODEL)],
                  [_sds((L, D_MODEL)), _sds((L, D_MODEL))],
                  [],
                  [x, y, z, norm_w, w_out, g_post], jobs)


def ssd_out_bwd(dxo, m, y, z, norm_w, w_out, g_post, name, jobs=()):
    L = dxo.shape[0]
    nt = L // TL
    rows = SSD_DI // N_DEV

    def body(dxo_ref, m_ref, y_ref, z_ref, nw_ref, wout_ref, gpost_ref,
             dy_ref, dz_ref, dwout_ref, dgp_ref, dnw_ref, acc_ref, stage_ref):
        i = pl.program_id(0)

        @pl.when(i == 0)
        def _():
            acc_ref[...] = jnp.zeros_like(acc_ref)
            dgp_ref[...] = jnp.zeros_like(dgp_ref)
            dnw_ref[...] = jnp.zeros_like(dnw_ref)

        dm, dgp = _rms_bwd(m_ref[...], gpost_ref[...], dxo_ref[...])
        dgp_ref[...] += dgp
        dmb = dm.astype(BF16)
        dyn = _mm_nt(dmb, wout_ref[...])
        z = z_ref[...]
        y = _heads_to_tokens(y_ref)
        sil, dsil = _silu_parts(z)
        yg = y * sil
        nw = nw_ref[...]
        acc_ref[...] += _mm_tn(_rms(yg, nw).astype(BF16), dmb)
        dyg, dnw = _rms_bwd(yg, nw, dyn)
        dnw_ref[...] += dnw
        dyv = dyg * sil
        dz_ref[...] = dyg * y * dsil
        for h in range(SSD_H):
            dy_ref[h] = dyv[:, h * SSD_P:(h + 1) * SSD_P]

        @pl.when(i == nt - 1)
        def _():
            _emit_row_shards(acc_ref, dwout_ref, stage_ref)

    hd = pl.BlockSpec((SSD_H, TL, SSD_P), lambda i: (0, i, 0))
    return _pcall(body, name, (nt,),
                  [_tile(D_MODEL), _tile(D_MODEL), hd, _tile(SSD_DI), _small((1, SSD_DI)), _res((SSD_DI, D_MODEL)),
                   _small((1, D_MODEL))],
                  [hd, _tile(SSD_DI), _ANY, _small((1, D_MODEL)), _small((1, SSD_DI))],
                  [_sds((SSD_H, L, SSD_P)), _sds((L, SSD_DI)), _sds((N_DEV, rows, D_MODEL), BF16),
                   _sds((1, D_MODEL)), _sds((1, SSD_DI))],
                  [pltpu.VMEM((SSD_DI, D_MODEL), F32), pltpu.VMEM((rows, D_MODEL), BF16)],
                  [dxo, m, y, z, norm_w, w_out, g_post], jobs)


def ssd_conv_bwd(dxh, dbm, dcm, xbc_raw, dz, ddt_raw, conv_w, conv_b, name, jobs=()):
    L = xbc_raw.shape[0]
    nt = L // TL

    def body(dxh_ref, dbm_ref, dcm_ref, raw_ref, halo_ref, dz_ref, ddt_ref, cw_ref, cb_ref,
             d_ref, dcw_ref, dcb_ref, carry_ref):
        i = pl.program_id(0)
        t = nt - 1 - i

        @pl.when(i == 0)
        def _():
            carry_ref[...] = jnp.zeros_like(carry_ref)
            dcw_ref[...] = jnp.zeros_like(dcw_ref)
            dcb_ref[...] = jnp.zeros_like(dcb_ref)

        def dact_chunk(rows, c):
            lo = c * CONV_LC
            if lo < SSD_DI:
                per, ref, first = SSD_P, dxh_ref, lo // SSD_P
            elif lo < SSD_DI + SSD_G * SSD_N:
                per, ref, first = SSD_N, dbm_ref, (lo - SSD_DI) // SSD_N
            else:
                per, ref, first = SSD_N, dcm_ref, (lo - SSD_DI - SSD_G * SSD_N) // SSD_N
            return jnp.concatenate([ref[first + q, rows, :] for q in range(CONV_LC // per)], axis=1)

        for c in range(SSD_CONV // CONV_LC):
            cols = slice(c * CONV_LC, (c + 1) * CONV_LC)
            w = cw_ref[:, cols]
            b = cb_ref[:, cols]
            halo = jnp.where(t == 0, 0.0, halo_ref[:, cols])
            carry = carry_ref[:, cols]
            dcb = jnp.zeros((1, CONV_LC), F32)
            dcw = jnp.zeros((SSD_KW, CONV_LC), F32)
            for r in reversed(range(TL // CONV_RC)):
                r0 = r * CONV_RC
                rows = slice(r0, r0 + CONV_RC)
                if r0 == 0:
                    ext = jnp.concatenate([halo, raw_ref[rows, cols]], axis=0)
                else:
                    ext = raw_ref[r0 - HALO:r0 + CONV_RC, cols]
                taps = _taps(ext, SSD_KW, CONV_RC)
                pre = _conv_fwd(taps, w) + b
                _, dsil = _silu_parts(pre)
                dpre = dact_chunk(rows, c) * dsil
                dcb = dcb + jnp.sum(dpre, axis=0, keepdims=True)
                dcw = dcw + _conv_bwd_w(taps, dpre)
                extd = jnp.concatenate([dpre, carry], axis=0)
                draw = _conv_bwd_in(extd, w, SSD_KW, CONV_RC)
                carry = dpre[:HALO, :]
                d_ref[rows, SSD_DI + c * CONV_LC:SSD_DI + (c + 1) * CONV_LC] = draw.astype(BF16)
            carry_ref[:, cols] = carry
            dcb_ref[:, cols] += dcb
            dcw_ref[:, cols] += dcw
        d_ref[:, :SSD_DI] = dz_ref[...].astype(BF16)
        tail = jnp.concatenate([ddt_ref[...], jnp.zeros((TL, SSD_IN_PAD - SSD_IN), F32)], axis=1)
        d_ref[:, SSD_DI + SSD_CONV:] = tail.astype(BF16)

    hd = pl.BlockSpec((SSD_H, TL, SSD_P), lambda i: (0, nt - 1 - i, 0))
    gr = pl.BlockSpec((SSD_G, TL, SSD_N), lambda i: (0, nt - 1 - i, 0))
    return _pcall(body, name, (nt,),
                  [hd, gr, gr, _rtile(SSD_CONV, nt), _halo_before(SSD_CONV, nt, True), _rtile(SSD_DI, nt),
                   _rtile(SSD_H, nt), _small((SSD_KW, SSD_CONV)), _small((1, SSD_CONV))],
                  [_rtile(SSD_IN_PAD, nt), _small((SSD_KW, SSD_CONV)), _small((1, SSD_CONV))],
                  [_sds((L, SSD_IN_PAD), BF16), _sds((SSD_KW, SSD_CONV)), _sds((1, SSD_CONV))],
                  [pltpu.VMEM((HALO, SSD_CONV), F32)],
                  [dxh, dbm, dcm, xbc_raw, xbc_raw, dz, ddt_raw, conv_w, conv_b], jobs)


def loss_fwd_bwd(y, target, name, jobs=()):
    L = y.shape[0]
    nt = L // TL

    def body(y_ref, t_ref, loss_ref, dy_ref):
        i = pl.program_id(0)

        @pl.when(i == 0)
        def _():
            loss_ref[...] = jnp.zeros_like(loss_ref)

        err = y_ref[...] - t_ref[...]
        dy_ref[...] = err * (1.0 / D_MODEL)
        loss_ref[...] += 0.5 * jnp.sum(jnp.mean(err * err, axis=-1, keepdims=True), axis=0, keepdims=True)

    return _pcall(body, name, (nt,),
                  [_tile(D_MODEL), _tile(D_MODEL)],
                  [_small((1, 1)), _tile(D_MODEL)],
                  [_sds((1, 1)), _sds((L, D_MODEL))],
                  [],
                  [y, target], jobs)


def _adamw_math(w, g, m, v):
    m = ADAM_B1 * m + (1.0 - ADAM_B1) * g
    v = ADAM_B2 * v + (1.0 - ADAM_B2) * (g * g)
    m_hat = m / (1.0 - ADAM_B1 ** ADAM_STEP)
    v_hat = v / (1.0 - ADAM_B2 ** ADAM_STEP)
    delta = -ADAM_LR * (m_hat / (jnp.sqrt(v_hat) + ADAM_EPS) + ADAM_WD * w)
    return delta, m, v


def _row_tile(rows):
    for cand in (256, 176, 128, 64, 32, 16, 8):
        if rows % cand == 0:
            return cand
    return rows


def reduce_adamw(recvs, w, m, v, name, jobs=()):
    nl = len(recvs)
    _, R, C = recvs[0].shape
    if R % 16 == 0:
        tr, tc = _row_tile(R), C
    else:
        tr, tc = R, 2 * LANES
    nr = (R // tr) * (C // tc)

    def body(*refs):
        r_refs = refs[:nl]
        w_ref, m_ref, v_ref, g_out, d_out, m_out, v_out = refs[nl:]
        layer = pl.program_id(0) // nr
        for ll in range(nl):
            @pl.when(layer == ll)
            def _(ll=ll):
                g = r_refs[ll][0].astype(F32)
                for j in range(1, N_DEV):
                    g = g + r_refs[ll][j].astype(F32)
                delta, mn, vn = _adamw_math(w_ref[0], g, m_ref[0], v_ref[0])
                g_out[0] = g
                d_out[0] = delta
                m_out[0] = mn
                v_out[0] = vn

    def recv_spec(ll):
        def index(i):
            t = jnp.where(i // nr == ll, i % nr, 0)
            return (0, t, 0) if tc == C else (0, 0, t)
        return pl.BlockSpec((N_DEV, tr, tc), index)

    blk = pl.BlockSpec((1, tr, tc), lambda i: (i // nr, i % nr, 0) if tc == C else (i // nr, 0, i % nr))
    return _pcall(body, name, (nl * nr,),
                  [recv_spec(ll) for ll in range(nl)] + [blk, blk, blk],
                  [blk] * 4,
                  [_sds((nl, R, C))] * 4,
                  [],
                  [*recvs, w, m, v], jobs)


def small_reduce(gathered, name):
    _, R, C = gathered.shape

    def body(r_ref, o_ref):
        g = r_ref[0]
        for j in range(1, N_DEV):
            g = g + r_ref[j]
        o_ref[...] = g

    return pl.pallas_call(body, name=name, out_shape=_sds((R, C)))(gathered)


def small_adamw(g, w, m, v, name):
    def body(g_ref, w_ref, m_ref, v_ref, d_out, m_out, v_out):
        delta, mn, vn = _adamw_math(w_ref[...], g_ref[...], m_ref[...], v_ref[...])
        d_out[...] = delta
        m_out[...] = mn
        v_out[...] = vn

    return pl.pallas_call(body, name=name, out_shape=[_sds(g.shape)] * 3)(g, w, m, v)


def _pack(arrs):
    flat = jnp.concatenate([a.reshape(-1) for a in arrs])
    n = flat.shape[0]
    rows = -(-n // (8 * LANES)) * 8
    flat = jnp.pad(flat, (0, rows * LANES - n))
    return flat.reshape(rows, LANES)


def _unpack(packed, shapes):
    flat = packed.reshape(-1)
    out = []
    off = 0
    for s in shapes:
        n = 1
        for d in s:
            n *= d
        out.append(flat[off:off + n].reshape(s))
        off += n
    return out


def kernel(x, mix_pre_g, mix_post_g, ffn_pre_g, ffn_post_g, ssd_w_in, ssd_conv_w, ssd_conv_b, ssd_dt_bias, ssd_A_log, ssd_D, ssd_norm_w, ssd_w_out, sc_w_in, sc_conv_w, sc_w_out, ffn_w_up, ffn_conv_w, ffn_conv_b, ffn_w_down, loss_target, m_mix_pre_g, m_mix_post_g, m_ffn_pre_g, m_ffn_post_g, m_ssd_w_in, m_ssd_conv_w, m_ssd_conv_b, m_ssd_dt_bias, m_ssd_A_log, m_ssd_D, m_ssd_norm_w, m_ssd_w_out, m_sc_w_in, m_sc_conv_w, m_sc_w_out, m_ffn_w_up, m_ffn_conv_w, m_ffn_conv_b, m_ffn_w_down, v_mix_pre_g, v_mix_post_g, v_ffn_pre_g, v_ffn_post_g, v_ssd_w_in, v_ssd_conv_w, v_ssd_conv_b, v_ssd_dt_bias, v_ssd_A_log, v_ssd_D, v_ssd_norm_w, v_ssd_w_out, v_sc_w_in, v_sc_conv_w, v_sc_w_out, v_ffn_w_up, v_ffn_conv_w, v_ffn_conv_b, v_ffn_w_down):
    me = _my_index()
    x0 = x[0]
    target = loss_target[0]
    row = lambda a: a.reshape(1, -1)

    tr = lambda a: jnp.transpose(a, (0, 2, 1))
    shards = {"ssd_in": tr(ssd_w_in), "ssd_out": ssd_w_out, "sc_in": sc_w_in, "sc_out": sc_w_out,
              "up": tr(ffn_w_up), "down": ffn_w_down}
    col_sharded = ("sc_in",)
    padded_rows = {"ssd_in": SSD_IN_PAD}
    weights = {}

    def shard_bf16(key):
        n, l = key
        return shards[n][l].astype(BF16)

    def store_weights(keys, outs):
        for (n, l), g in zip(keys, outs):
            _, R, C = g.shape
            if n in col_sharded:
                full = jnp.transpose(g, (1, 0, 2)).reshape(R, N_DEV * C)
            else:
                full = g.reshape(N_DEV * R, C)
                if n in padded_rows:
                    full = jnp.pad(full, ((0, padded_rows[n] - N_DEV * R), (0, 0)))
            weights[(n, l)] = full

    fwd_first_half = {
        "ssd_inproj_0": [("up", 0), ("down", 0)],
        "ssd_scan_fwd_0": [("sc_in", 0), ("sc_out", 0), ("up", 1)],
        "ssd_out_fwd_0": [("down", 1)],
        "ffn_fwd_0": [("ssd_in", 1)],
        "sc_fwd_0": [("ssd_out", 1), ("down", 2)],
        "ffn_fwd_1": [("up", 2)],
        "ssd_inproj_1": [("sc_in", 1), ("sc_out", 1), ("down", 3)],
        "ssd_scan_fwd_1": [("up", 3)],
    }
    bwd_sched = {
        "ffn_bwd2_3": [("down", 3)], "sc_bwd2_1": [("sc_out", 1)], "ffn_bwd1_2": [("up", 3)], "ffn_bwd2_2": [("sc_in", 1)],
        "ssd_out_bwd_1": [("down", 2)], "ssd_scan_bwd_1": [("up", 2), ("ssd_out", 1)], "ffn_bwd1_1": [("ssd_in", 1)],
        "ffn_bwd2_1": [("down", 1)], "sc_bwd2_0": [("sc_out", 0)], "ffn_bwd1_0": [("up", 1)], "ffn_bwd2_0": [("sc_in", 0)],
        "ssd_out_bwd_0": [("down", 0)], "ssd_scan_bwd_0": [("up", 0), ("ssd_out", 0)],
    }

    first = [("ssd_in", 0), ("ssd_out", 0)]
    outs = exchange([("ag2", shard_bf16(k)) for k in first]
                    + [("ag", ssd_conv_w), ("ag", sc_conv_w), ("ag", ffn_conv_w)], "ag_first")
    store_weights(first, outs[:2])

    def taps(g):
        _, nl, K, C = g.shape
        return jnp.transpose(g, (1, 2, 0, 3)).reshape(nl, K, N_DEV * C)

    CW_ssd, CW_sc, CW_ffn = taps(outs[2]), taps(outs[3]), taps(outs[4])

    half_done = []

    def fwd(fn, name, *args):
        second = list(half_done)
        starting = fwd_first_half.get(name, [])
        jobs = [("agB", buf) for _, buf in second] + [("agA", shard_bf16(k)) for k in starting]
        res, got = fn(*args, name, jobs)
        store_weights([k for k, _ in second], got[:len(second)])
        half_done[:] = list(zip(starting, got[len(second):]))
        return res

    saved = []
    h = x0
    for i in range(DEPTH):
        j = i // 2
        blk = dict(x_mix=h)
        if i % 2 == 0:
            z, raw, dt_raw, xh, bm, cm = fwd(ssd_inproj, f"ssd_inproj_{j}", h, row(mix_pre_g[i]), weights[("ssd_in", j)],
                                             CW_ssd[j], row(ssd_conv_b[j]))
            y, sprev = fwd(ssd_scan_fwd, f"ssd_scan_fwd_{j}", xh, bm, cm, dt_raw, row(ssd_dt_bias[j]),
                           row(ssd_A_log[j]), row(ssd_D[j]))
            m, h = fwd(ssd_out_fwd, f"ssd_out_fwd_{j}", h, y, z, row(ssd_norm_w[j]), weights[("ssd_out", j)],
                       row(mix_post_g[i]))
            blk.update(z=z, raw=raw, dt_raw=dt_raw, xh=xh, bm=bm, cm=cm, y=y, sprev=sprev, m=m)
        else:
            bcv, m, h = fwd(sc_fwd, f"sc_fwd_{j}", h, row(mix_pre_g[i]), weights[("sc_in", j)], CW_sc[j],
                            weights[("sc_out", j)], row(mix_post_g[i]))
            blk.update(bcv=bcv, m=m)
        blk["x_ffn"] = h
        up, f, h = fwd(ffn_fwd, f"ffn_fwd_{i}", h, row(ffn_pre_g[i]), weights[("up", i)], CW_ffn[i],
                       row(ffn_conv_b[i]), weights[("down", i)], row(ffn_post_g[i]))
        blk.update(up=up, f=f)
        saved.append(blk)

    (loss_dev, dh), _ = loss_fwd_bwd(h, target, "loss")
    loss = lax.psum(loss_dev[0, 0], ("x", "y", "c"))

    parts, recvd = {}, {}

    def bwd(fn, name, *args):
        keys = bwd_sched.get(name, [])
        res, got = fn(*args, name, [("a2a", parts[k]) for k in keys])
        for k, g in zip(keys, got):
            recvd[k] = g
        return res

    g_mix_pre, g_mix_post, g_ffn_pre, g_ffn_post = [None] * DEPTH, [None] * DEPTH, [None] * DEPTH, [None] * DEPTH
    g_ffn_cw, g_ffn_cb = [None] * DEPTH, [None] * DEPTH
    g_ssd_cw, g_ssd_cb, g_ssd_dtb, g_ssd_alog, g_ssd_d, g_ssd_nw = ([None] * 2 for _ in range(6))
    g_sc_cw = [None] * 2
    for i in reversed(range(DEPTH)):
        j = i // 2
        blk = saved[i]
        dup, parts[("down", i)], g_ffn_post[i], g_ffn_cw[i], g_ffn_cb[i] = bwd(
            ffn_bwd1, f"ffn_bwd1_{i}", dh, blk["f"], blk["up"], row(ffn_post_g[i]), weights[("down", i)], CW_ffn[i],
            row(ffn_conv_b[i]))
        dh, parts[("up", i)], g_ffn_pre[i] = bwd(functools.partial(inproj_bwd, transposed=True), f"ffn_bwd2_{i}",
                                                  blk["x_ffn"], row(ffn_pre_g[i]), dup, weights[("up", i)], dh,
                                                  2 * FFN_F // N_DEV)
        if i % 2 == 0:
            dy, dz, parts[("ssd_out", j)], g_mix_post[i], g_ssd_nw[j] = bwd(
                ssd_out_bwd, f"ssd_out_bwd_{j}", dh, blk["m"], blk["y"], blk["z"], row(ssd_norm_w[j]),
                weights[("ssd_out", j)], row(mix_post_g[i]))
            dxh, dbm, dcm, ddt, g_ssd_dtb[j], g_ssd_alog[j], g_ssd_d[j] = bwd(
                ssd_scan_bwd, f"ssd_scan_bwd_{j}", dy, blk["xh"], blk["bm"], blk["cm"], blk["dt_raw"], blk["sprev"],
                row(ssd_dt_bias[j]), row(ssd_A_log[j]), row(ssd_D[j]))
            d_in, g_ssd_cw[j], g_ssd_cb[j] = bwd(ssd_conv_bwd, f"ssd_conv_bwd_{j}", dxh, dbm, dcm, blk["raw"], dz, ddt,
                                                 CW_ssd[j], row(ssd_conv_b[j]))
            dh, parts[("ssd_in", j)], g_mix_pre[i] = bwd(functools.partial(inproj_bwd, transposed=True),
                                                          f"ssd_bwd2_{j}", blk["x_mix"], row(mix_pre_g[i]), d_in,
                                                          weights[("ssd_in", j)], dh, SSD_IN // N_DEV)
        else:
            dbcv, parts[("sc_out", j)], g_mix_post[i], g_sc_cw[j] = bwd(
                sc_bwd1, f"sc_bwd1_{j}", dh, blk["m"], blk["bcv"], row(mix_post_g[i]), weights[("sc_out", j)], CW_sc[j])
            dh, parts[("sc_in", j)], g_mix_pre[i] = bwd(inproj_bwd, f"sc_bwd2_{j}", blk["x_mix"], row(mix_pre_g[i]),
                                                         dbcv, weights[("sc_in", j)], dh, 3 * D_MODEL // N_DEV)
    grad_x = dh[None]

    st = lambda lst: jnp.concatenate(lst, axis=0)
    small_full = [
        st(g_mix_pre), st(g_mix_post), st(g_ffn_pre), st(g_ffn_post),
        jnp.stack(g_ssd_cw), st(g_ssd_cb), st(g_ssd_dtb), st(g_ssd_alog), st(g_ssd_d), st(g_ssd_nw),
        jnp.stack(g_sc_cw), jnp.stack(g_ffn_cw), st(g_ffn_cb),
    ]
    full_shapes = [a.shape for a in small_full]

    recvd[("ssd_in", 0)], small_gathered = exchange([("a2a", parts[("ssd_in", 0)]), ("ag", _pack(small_full))],
                                                    "a2a_last")

    def finish(n, nl, w, m, v):
        return reduce_adamw([recvd[(n, l)] for l in range(nl)], w, m, v, "adamw_" + n)[0]

    r_up = [tr(a) for a in finish("up", DEPTH, tr(ffn_w_up), tr(m_ffn_w_up), tr(v_ffn_w_up))]
    r_down = finish("down", DEPTH, ffn_w_down, m_ffn_w_down, v_ffn_w_down)
    r_ssd_out = finish("ssd_out", 2, ssd_w_out, m_ssd_w_out, v_ssd_w_out)
    r_sc_in = finish("sc_in", 2, sc_w_in, m_sc_w_in, v_sc_w_in)
    r_sc_out = finish("sc_out", 2, sc_w_out, m_sc_w_out, v_sc_w_out)
    r_ssd_in = [tr(a) for a in finish("ssd_in", 2, tr(ssd_w_in), tr(m_ssd_w_in), tr(v_ssd_w_in))]

    summed = small_reduce(small_gathered, "small_reduce")
    (s_mix_pre, s_mix_post, s_ffn_pre, s_ffn_post, s_ssd_cw, s_ssd_cb, s_ssd_dtb, s_ssd_alog, s_ssd_d, s_ssd_nw,
     s_sc_cw, s_ffn_cw, s_ffn_cb) = _unpack(summed, full_shapes)

    def my_cols(a, width):
        return lax.dynamic_slice_in_dim(a, me * width, width, axis=a.ndim - 1)

    s_ssd_cw = my_cols(s_ssd_cw, SSD_CONV // N_DEV)
    s_sc_cw = my_cols(s_sc_cw, D_MODEL // N_DEV)
    s_ffn_cw = my_cols(s_ffn_cw, FFN_F // N_DEV)

    small_g = [s_mix_pre, s_mix_post, s_ffn_pre, s_ffn_post, s_ssd_cw, s_ssd_cb, s_ssd_dtb, s_ssd_alog, s_ssd_d,
               s_ssd_nw, s_sc_cw, s_ffn_cw, s_ffn_cb]
    small_w = [mix_pre_g, mix_post_g, ffn_pre_g, ffn_post_g, ssd_conv_w, ssd_conv_b, ssd_dt_bias, ssd_A_log, ssd_D,
               ssd_norm_w, sc_conv_w, ffn_conv_w, ffn_conv_b]
    small_m = [m_mix_pre_g, m_mix_post_g, m_ffn_pre_g, m_ffn_post_g, m_ssd_conv_w, m_ssd_conv_b, m_ssd_dt_bias,
               m_ssd_A_log, m_ssd_D, m_ssd_norm_w, m_sc_conv_w, m_ffn_conv_w, m_ffn_conv_b]
    small_v = [v_mix_pre_g, v_mix_post_g, v_ffn_pre_g, v_ffn_post_g, v_ssd_conv_w, v_ssd_conv_b, v_ssd_dt_bias,
               v_ssd_A_log, v_ssd_D, v_ssd_norm_w, v_sc_conv_w, v_ffn_conv_w, v_ffn_conv_b]
    local_shapes = [a.shape for a in small_w]
    pd, pm, pv = small_adamw(_pack(small_g), _pack(small_w), _pack(small_m), _pack(small_v), "small_adamw")
    sd = _unpack(pd, local_shapes)
    sm = _unpack(pm, local_shapes)
    sv = _unpack(pv, local_shapes)

    def ordered(small, big):
        (mix_pre, mix_post, ffn_pre, ffn_post, ssd_cw, ssd_cb, dtb, alog, dsk, nw, sc_cw, ffn_cw, ffn_cb) = small
        (b_ssd_in, b_ssd_out, b_sc_in, b_sc_out, b_up, b_down) = big
        return [mix_pre, mix_post, ffn_pre, ffn_post, b_ssd_in, ssd_cw, ssd_cb, dtb, alog, dsk, nw, b_ssd_out,
                b_sc_in, sc_cw, b_sc_out, b_up, ffn_cw, ffn_cb, b_down]

    bigs = [r_ssd_in, r_ssd_out, r_sc_in, r_sc_out, r_up, r_down]
    grads = ordered(small_g, [r[0] for r in bigs])
    deltas = ordered(sd, [r[1] for r in bigs])
    new_m = ordered(sm, [r[2] for r in bigs])
    new_v = ordered(sv, [r[3] for r in bigs])
    return (loss, grad_x, *grads, *deltas, *new_m, *new_v)
```

```python
import functools

import jax
import jax.numpy as jnp
from jax import lax
from jax.experimental import pallas as pl
from jax.experimental.pallas import tpu as pltpu

F32 = jnp.float32
BF16 = jnp.bfloat16

EPS = 1e-6
D_MODEL = 1024
DEPTH = 4
N_DEV = 8
CHUNK = 64
SSD_DI = 2048
SSD_H = 32
SSD_P = 64
SSD_G = 8
SSD_R = SSD_H // SSD_G
SSD_N = 128
SSD_CONV = SSD_DI + 2 * SSD_G * SSD_N
SSD_IN = SSD_DI + SSD_CONV + SSD_H
LANES = 128
SSD_IN_PAD = -(-SSD_IN // LANES) * LANES
SSD_KW = 4
SC_KW = 3
FFN_F = 2816
FFN_KW = 3
SCAN_CPS = 2
TL = 256
HALO = 8
CONV_RC = 128
CONV_LC = 256
VMEM_LIMIT = 60 * 1024 * 1024

ADAM_LR = 0.001
ADAM_B1 = 0.9
ADAM_B2 = 0.999
ADAM_EPS = 1e-08
ADAM_WD = 0.01
ADAM_STEP = 10

MESH = pl.DeviceIdType.MESH


def _rms(x, g):
    r = lax.rsqrt(jnp.mean(x * x, axis=-1, keepdims=True) + EPS)
    return x * r * g


def _rms_bwd(x, g, dy):
    r = lax.rsqrt(jnp.mean(x * x, axis=-1, keepdims=True) + EPS)
    xh = x * r
    dg = jnp.sum(dy * xh, axis=0, keepdims=True)
    dxh = dy * g
    dx = r * (dxh - xh * jnp.mean(dxh * xh, axis=-1, keepdims=True))
    return dx, dg


def _mm(a, b):
    return jnp.dot(a, b, preferred_element_type=F32)


def _mm_nt(a, b):
    return lax.dot_general(a, b, (((1,), (1,)), ((), ())), preferred_element_type=F32)


def _mm_tn(a, b):
    return lax.dot_general(a, b, (((0,), (0,)), ((), ())), preferred_element_type=F32)


def _silu_parts(x):
    sg = jax.nn.sigmoid(x)
    return x * sg, sg * (1.0 + x * (1.0 - sg))


def _rows_from(a, s, tl):
    if s % HALO == 0:
        return a[s:s + tl]
    return pltpu.roll(a, a.shape[0] - s, 0)[0:tl]


def _taps(ext, kw, tl):
    base = HALO - (kw - 1)
    return [_rows_from(ext, base + j, tl) for j in range(kw)]


def _conv_fwd(taps, w):
    out = taps[0] * w[0:1]
    for j in range(1, len(taps)):
        out = out + taps[j] * w[j:j + 1]
    return out


def _conv_bwd_in(extd, w, kw, tl):
    out = _rows_from(extd, kw - 1, tl) * w[0:1]
    for j in range(1, kw):
        out = out + _rows_from(extd, kw - 1 - j, tl) * w[j:j + 1]
    return out


def _conv_bwd_w(taps, dy):
    return jnp.concatenate([jnp.sum(dy * t, axis=0, keepdims=True) for t in taps], axis=0)


def _emit_row_shards(acc_ref, out_ref, stage_ref):
    rows = out_ref.shape[1]
    for k in range(N_DEV):
        stage_ref[...] = acc_ref[k * rows:(k + 1) * rows, :].astype(BF16)
        pltpu.sync_copy(stage_ref, out_ref.at[k])


def _emit_col_shards(acc_ref, out_ref, stage_ref):
    cols = out_ref.shape[2]
    for k in range(N_DEV):
        stage_ref[...] = acc_ref[:, k * cols:(k + 1) * cols].astype(BF16)
        pltpu.sync_copy(stage_ref, out_ref.at[k])


def _res(shape):
    nd = len(shape)
    return pl.BlockSpec(shape, lambda i: (0,) * nd, pipeline_mode=pl.Buffered(1))


def _small(shape):
    nd = len(shape)
    return pl.BlockSpec(shape, lambda i: (0,) * nd)


def _tile(n):
    return pl.BlockSpec((TL, n), lambda i: (i, 0))


def _rtile(n, nt):
    return pl.BlockSpec((TL, n), lambda i: (nt - 1 - i, 0))


def _halo_before(n, nt, reverse):
    per = TL // HALO
    if reverse:
        return pl.BlockSpec((HALO, n), lambda i: (jnp.maximum((nt - 1 - i) * per - 1, 0), 0))
    return pl.BlockSpec((HALO, n), lambda i: (jnp.maximum(i * per - 1, 0), 0))


_ANY = pl.BlockSpec(memory_space=pl.ANY)


def _sds(shape, dtype=F32):
    return jax.ShapeDtypeStruct(shape, dtype)


def _peer(k):
    x, y, c = lax.axis_index("x"), lax.axis_index("y"), lax.axis_index("c")
    px = x ^ (k >> 2)
    py = y ^ ((k >> 1) & 1)
    pc = c ^ (k & 1)
    return (px, py, pc), 4 * px + 2 * py + pc


def _my_index():
    return 4 * lax.axis_index("x") + 2 * lax.axis_index("y") + lax.axis_index("c")


SIBLING = 1
SAME_CORE_CHIPS = (2, 4, 6)


def _job_copies(kind, src_ref, out_ref, send_sems, recv_sems, local_sems, j):
    me = _my_index()
    sends, recvs = [], []

    def pair(pattern, sem, src, put_slot, get_slot):
        dev, _ = _peer(pattern)
        sems = dict(send_sem=send_sems.at[j, sem], recv_sem=recv_sems.at[j, sem], device_id=dev, device_id_type=MESH)
        sends.append(pltpu.make_async_remote_copy(src_ref=src, dst_ref=out_ref.at[put_slot], **sems))
        recvs.append(pltpu.make_async_remote_copy(src_ref=src, dst_ref=out_ref.at[get_slot], **sems))

    if kind == "agB":
        for k in SAME_CORE_CHIPS:
            _, mine_from_k = _peer(k)
            _, sib_from_k = _peer(k | SIBLING)
            pair(SIBLING, k, out_ref.at[mine_from_k], mine_from_k, sib_from_k)
        return None, sends, recvs
    patterns = (SIBLING,) + SAME_CORE_CHIPS if kind == "agA" else range(1, N_DEV)
    mine = src_ref.at[me] if kind == "a2a" else src_ref
    local = pltpu.make_async_copy(mine, out_ref.at[me], local_sems.at[j])
    for k in patterns:
        _, idx = _peer(k)
        pair(k, k - 1, src_ref.at[idx] if kind == "a2a" else src_ref, me, idx)
    return local, sends, recvs


def _pcall(body, name, grid, in_specs, out_specs, out_shape, scratch_shapes, args, jobs=()):
    n_in, n_out, nj = len(in_specs), len(out_specs), len(jobs)
    last = grid[0] - 1
    kinds = [k for k, _ in jobs]

    def wrapped(*refs):
        ins = refs[:n_in]
        csrc = refs[n_in:n_in + nj]
        outs = refs[n_in + nj:n_in + nj + n_out]
        cout = refs[n_in + nj + n_out:n_in + 2 * nj + n_out]
        rest = refs[n_in + 2 * nj + n_out:]

        def copies(j, kind):
            return _job_copies(kind, csrc[j], cout[j], send_sems, recv_sems, local_sems, j)

        def start(j, kind):
            local, sends, _ = copies(j, kind)
            if local is not None:
                local.start()
            for cp in sends:
                cp.start()

        def finish(j, kind, arrivals_only=False):
            local, sends, recvs = copies(j, kind)
            for cp in recvs:
                cp.wait_recv()
            if not arrivals_only:
                for cp in sends:
                    cp.wait_send()
                if local is not None:
                    local.wait()

        if nj:
            scratch, (send_sems, recv_sems, local_sems) = rest[:-3], rest[-3:]
            i = pl.program_id(0)

            @pl.when(i == 0)
            def _():
                for j in range(nj):
                    start(j, "agA" if kinds[j] == "ag2" else kinds[j])
                for j in range(nj):
                    if kinds[j] == "ag2":
                        finish(j, "agA", arrivals_only=True)
                        start(j, "agB")
        else:
            scratch = rest
        body(*ins, *outs, *scratch)
        if nj:
            @pl.when(i == last)
            def _():
                for j in range(nj):
                    if kinds[j] == "ag2":
                        finish(j, "agB")
                        _, sends, _ = copies(j, "agA")
                        for cp in sends:
                            cp.wait_send()
                        copies(j, "agA")[0].wait()
                    else:
                        finish(j, kinds[j])

    job_shapes = []
    aliases = {}
    for j, (kind, s) in enumerate(jobs):
        shp = (N_DEV,) + tuple(s.shape) if kind in ("ag", "agA", "ag2") else tuple(s.shape)
        job_shapes.append(_sds(shp, s.dtype))
        if kind == "agB":
            aliases[n_in + j] = n_out + j
    sems = [pltpu.SemaphoreType.DMA((nj, N_DEV - 1)), pltpu.SemaphoreType.DMA((nj, N_DEV - 1)),
            pltpu.SemaphoreType.DMA((nj,))] if nj else []
    res = pl.pallas_call(
        wrapped, name=name, grid=grid,
        in_specs=list(in_specs) + [_ANY] * nj,
        out_specs=list(out_specs) + [_ANY] * nj,
        out_shape=list(out_shape) + job_shapes,
        scratch_shapes=list(scratch_shapes) + sems,
        input_output_aliases=aliases,
        compiler_params=pltpu.CompilerParams(dimension_semantics=("arbitrary",), vmem_limit_bytes=VMEM_LIMIT,
                                             has_side_effects=bool(nj)),
    )(*args, *[s for _, s in jobs])
    return list(res[:n_out]), list(res[n_out:])


def exchange(jobs, name):
    def body(o_ref):
        o_ref[...] = jnp.zeros_like(o_ref)

    _, outs = _pcall(body, name, (1,), [], [_small((8, LANES))], [_sds((8, LANES))], [], [], jobs)
    return outs


def ffn_fwd(x, g_pre, w_up, conv_w, conv_b, w_down, g_post, name, jobs=()):
    L = x.shape[0]
    nt = L // TL
    F = FFN_F

    def body(x_ref, gpre_ref, wup_ref, cw_ref, cb_ref, wdn_ref, gpost_ref, up_ref, f_ref, xn_ref, carry_ref):
        i = pl.program_id(0)

        @pl.when(i == 0)
        def _():
            carry_ref[...] = jnp.zeros_like(carry_ref)

        x = x_ref[...]
        h = _rms(x, gpre_ref[...]).astype(BF16)
        up = _mm_nt(h, wup_ref[...])
        up_ref[...] = up
        ug = up[:, :F]
        val = up[:, F:]
        ext = jnp.concatenate([carry_ref[...], ug], axis=0)
        gate = _conv_fwd(_taps(ext, FFN_KW, TL), cw_ref[...]) + cb_ref[...]
        carry_ref[...] = ug[TL - HALO:, :]
        a = (gate * jax.nn.sigmoid(gate) * val).astype(BF16)
        f = _mm(a, wdn_ref[...])
        f_ref[...] = f
        xn_ref[...] = x + _rms(f, gpost_ref[...])

    return _pcall(body, name, (nt,),
                  [_tile(D_MODEL), _small((1, D_MODEL)), _res((2 * F, D_MODEL)), _small((FFN_KW, F)), _small((1, F)),
                   _res((F, D_MODEL)), _small((1, D_MODEL))],
                  [_tile(2 * F), _tile(D_MODEL), _tile(D_MODEL)],
                  [_sds((L, 2 * F)), _sds((L, D_MODEL)), _sds((L, D_MODEL))],
                  [pltpu.VMEM((HALO, F), F32)],
                  [x, g_pre, w_up, conv_w, conv_b, w_down, g_post], jobs)


def ffn_bwd1(dxo, f, up, g_post, w_down, conv_w, conv_b, name, jobs=()):
    L = dxo.shape[0]
    nt = L // TL
    F = FFN_F
    rows = F // N_DEV

    def body(dxo_ref, f_ref, up_ref, halo_ref, gpost_ref, wdn_ref, cw_ref, cb_ref,
             dup_ref, dwdn_ref, dgp_ref, dcw_ref, dcb_ref, acc_ref, carry_ref, stage_ref):
        i = pl.program_id(0)
        t = nt - 1 - i

        @pl.when(i == 0)
        def _():
            acc_ref[...] = jnp.zeros_like(acc_ref)
            carry_ref[...] = jnp.zeros_like(carry_ref)
            dgp_ref[...] = jnp.zeros_like(dgp_ref)
            dcw_ref[...] = jnp.zeros_like(dcw_ref)
            dcb_ref[...] = jnp.zeros_like(dcb_ref)

        df, dgp = _rms_bwd(f_ref[...], gpost_ref[...], dxo_ref[...])
        dgp_ref[...] += dgp
        dfb = df.astype(BF16)
        for c in range(F // CONV_LC):
            cols = slice(c * CONV_LC, (c + 1) * CONV_LC)
            vcols = slice(F + c * CONV_LC, F + (c + 1) * CONV_LC)
            da = _mm_nt(dfb, wdn_ref[cols, :])
            w = cw_ref[:, cols]
            b = cb_ref[:, cols]
            halo = jnp.where(t == 0, 0.0, halo_ref[:, cols])
            carry = carry_ref[:, cols]
            dcb = jnp.zeros((1, CONV_LC), F32)
            dcw = jnp.zeros((FFN_KW, CONV_LC), F32)
            a_rows = [None] * (TL // CONV_RC)
            for r in reversed(range(TL // CONV_RC)):
                r0 = r * CONV_RC
                rows = slice(r0, r0 + CONV_RC)
                if r0 == 0:
                    ext = jnp.concatenate([halo, up_ref[rows, cols]], axis=0)
                else:
                    ext = up_ref[r0 - HALO:r0 + CONV_RC, cols]
                val = up_ref[rows, vcols]
                taps = _taps(ext, FFN_KW, CONV_RC)
                gate = _conv_fwd(taps, w) + b
                s, ds = _silu_parts(gate)
                a_rows[r] = (s * val).astype(BF16)
                da_r = da[rows, :]
                dgate = da_r * val * ds
                dcb = dcb + jnp.sum(dgate, axis=0, keepdims=True)
                dcw = dcw + _conv_bwd_w(taps, dgate)
                extd = jnp.concatenate([dgate, carry], axis=0)
                dup_ref[rows, cols] = _conv_bwd_in(extd, w, FFN_KW, CONV_RC).astype(BF16)
                dup_ref[rows, vcols] = (da_r * s).astype(BF16)
                carry = dgate[:HALO, :]
            carry_ref[:, cols] = carry
            dcb_ref[:, cols] += dcb
            dcw_ref[:, cols] += dcw
            acc_ref[cols, :] += _mm_tn(jnp.concatenate(a_rows, axis=0), dfb)

        @pl.when(i == nt - 1)
        def _():
            _emit_row_shards(acc_ref, dwdn_ref, stage_ref)

    return _pcall(body, name, (nt,),
                  [_rtile(D_MODEL, nt), _rtile(D_MODEL, nt), _rtile(2 * F, nt), _halo_before(F, nt, True),
                   _small((1, D_MODEL)), _res((F, D_MODEL)), _small((FFN_KW, F)), _small((1, F))],
                  [_rtile(2 * F, nt), _ANY, _small((1, D_MODEL)), _small((FFN_KW, F)), _small((1, F))],
                  [_sds((L, 2 * F), BF16), _sds((N_DEV, rows, D_MODEL), BF16), _sds((1, D_MODEL)),
                   _sds((FFN_KW, F)), _sds((1, F))],
                  [pltpu.VMEM((F, D_MODEL), F32), pltpu.VMEM((HALO, F), F32), pltpu.VMEM((rows, D_MODEL), BF16)],
                  [dxo, f, up, up, g_post, w_down, conv_w, conv_b], jobs)


def inproj_bwd(x, g_pre, d, w, dxo, cols, name, jobs=(), transposed=False):
    L = x.shape[0]
    nt = L // TL
    N = d.shape[1]
    w_shape = (N, D_MODEL) if transposed else (D_MODEL, N)
    shard_shape = (cols, D_MODEL) if transposed else (D_MODEL, cols)

    def body(x_ref, g_ref, d_ref, w_ref, dxo_ref, dx_ref, dw_ref, dg_ref, acc_ref, stage_ref):
        i = pl.program_id(0)

        @pl.when(i == 0)
        def _():
            acc_ref[...] = jnp.zeros_like(acc_ref)
            dg_ref[...] = jnp.zeros_like(dg_ref)

        x = x_ref[...]
        g = g_ref[...]
        d = d_ref[...]
        h = _rms(x, g).astype(BF16)
        if transposed:
            dh = _mm(d, w_ref[...])
            acc_ref[...] += _mm_tn(d, h)
        else:
            dh = _mm_nt(d, w_ref[...])
            acc_ref[...] += _mm_tn(h, d)
        dxn, dg = _rms_bwd(x, g, dh)
        dx_ref[...] = dxo_ref[...] + dxn
        dg_ref[...] += dg

        @pl.when(i == nt - 1)
        def _():
            (_emit_row_shards if transposed else _emit_col_shards)(acc_ref, dw_ref, stage_ref)

    return _pcall(body, name, (nt,),
                  [_tile(D_MODEL), _small((1, D_MODEL)), _tile(N), _res(w_shape), _tile(D_MODEL)],
                  [_tile(D_MODEL), _ANY, _small((1, D_MODEL))],
                  [_sds((L, D_MODEL)), _sds((N_DEV,) + shard_shape, BF16), _sds((1, D_MODEL))],
                  [pltpu.VMEM(w_shape, F32), pltpu.VMEM(shard_shape, BF16)],
                  [x, g_pre, d, w, dxo], jobs)


def sc_fwd(x, g_pre, w_in, conv_w, w_out, g_post, name, jobs=()):
    L = x.shape[0]
    nt = L // TL
    W = D_MODEL

    def body(x_ref, gpre_ref, win_ref, cw_ref, wout_ref, gpost_ref, bcv_ref, m_ref, xn_ref, carry_ref):
        i = pl.program_id(0)

        @pl.when(i == 0)
        def _():
            carry_ref[...] = jnp.zeros_like(carry_ref)

        x = x_ref[...]
        h = _rms(x, gpre_ref[...]).astype(BF16)
        bcv = _mm(h, win_ref[...])
        bcv_ref[...] = bcv
        gb = bcv[:, :W]
        p = bcv[:, W:2 * W] * bcv[:, 2 * W:]
        ext = jnp.concatenate([carry_ref[...], p], axis=0)
        u = _conv_fwd(_taps(ext, SC_KW, TL), cw_ref[...])
        carry_ref[...] = p[TL - HALO:, :]
        m = _mm((gb * u).astype(BF16), wout_ref[...])
        m_ref[...] = m
        xn_ref[...] = x + _rms(m, gpost_ref[...])

    return _pcall(body, name, (nt,),
                  [_tile(W), _small((1, W)), _res((W, 3 * W)), _small((SC_KW, W)), _res((W, W)), _small((1, W))],
                  [_tile(3 * W), _tile(W), _tile(W)],
                  [_sds((L, 3 * W)), _sds((L, W)), _sds((L, W))],
                  [pltpu.VMEM((HALO, W), F32)],
                  [x, g_pre, w_in, conv_w, w_out, g_post], jobs)


def sc_bwd1(dxo, m, bcv, g_post, w_out, conv_w, name, jobs=()):
    L = dxo.shape[0]
    nt = L // TL
    W = D_MODEL
    rows = W // N_DEV

    def body(dxo_ref, m_ref, bcv_ref, halo_ref, gpost_ref, wout_ref, cw_ref,
             dbcv_ref, dwout_ref, dgp_ref, dcw_ref, acc_ref, carry_ref, stage_ref):
        i = pl.program_id(0)
        t = nt - 1 - i

        @pl.when(i == 0)
        def _():
            acc_ref[...] = jnp.zeros_like(acc_ref)
            carry_ref[...] = jnp.zeros_like(carry_ref)
            dgp_ref[...] = jnp.zeros_like(dgp_ref)
            dcw_ref[...] = jnp.zeros_like(dcw_ref)

        dm, dgp = _rms_bwd(m_ref[...], gpost_ref[...], dxo_ref[...])
        dgp_ref[...] += dgp
        dmb = dm.astype(BF16)
        dq = _mm_nt(dmb, wout_ref[...])
        bcv = bcv_ref[...]
        gb = bcv[:, :W]
        gc = bcv[:, W:2 * W]
        v = bcv[:, 2 * W:]
        hb = halo_ref[...]
        halo = jnp.where(t == 0, 0.0, hb[:, W:2 * W] * hb[:, 2 * W:])
        ext = jnp.concatenate([halo, gc * v], axis=0)
        w = cw_ref[...]
        taps = _taps(ext, SC_KW, TL)
        u = _conv_fwd(taps, w)
        acc_ref[...] += _mm_tn((gb * u).astype(BF16), dmb)
        dgb = dq * u
        du = dq * gb
        dcw_ref[...] += _conv_bwd_w(taps, du)
        extd = jnp.concatenate([du, carry_ref[...]], axis=0)
        dp = _conv_bwd_in(extd, w, SC_KW, TL)
        carry_ref[...] = du[:HALO, :]
        dbcv_ref[...] = jnp.concatenate([dgb, dp * v, dp * gc], axis=1).astype(BF16)

        @pl.when(i == nt - 1)
        def _():
            _emit_row_shards(acc_ref, dwout_ref, stage_ref)

    return _pcall(body, name, (nt,),
                  [_rtile(W, nt), _rtile(W, nt), _rtile(3 * W, nt), _halo_before(3 * W, nt, True),
                   _small((1, W)), _res((W, W)), _small((SC_KW, W))],
                  [_rtile(3 * W, nt), _ANY, _small((1, W)), _small((SC_KW, W))],
                  [_sds((L, 3 * W), BF16), _sds((N_DEV, rows, W), BF16), _sds((1, W)), _sds((SC_KW, W))],
                  [pltpu.VMEM((W, W), F32), pltpu.VMEM((HALO, W), F32), pltpu.VMEM((rows, W), BF16)],
                  [dxo, m, bcv, bcv, g_post, w_out, conv_w], jobs)


def ssd_inproj(x, g_pre, w_in, conv_w, conv_b, name, jobs=()):
    L = x.shape[0]
    nt = L // TL

    def body(x_ref, gpre_ref, win_ref, cw_ref, cb_ref, z_ref, raw_ref, dt_ref, xh_ref, bm_ref, cm_ref, carry_ref):
        i = pl.program_id(0)

        @pl.when(i == 0)
        def _():
            carry_ref[...] = jnp.zeros_like(carry_ref)

        h = _rms(x_ref[...], gpre_ref[...]).astype(BF16)
        zx = _mm_nt(h, win_ref[...])
        z_ref[...] = zx[:, :SSD_DI]
        raw = zx[:, SSD_DI:SSD_DI + SSD_CONV]
        raw_ref[...] = raw
        dt_ref[...] = zx[:, SSD_DI + SSD_CONV:SSD_IN]
        ext = jnp.concatenate([carry_ref[...], raw], axis=0)
        pre = _conv_fwd(_taps(ext, SSD_KW, TL), cw_ref[...]) + cb_ref[...]
        carry_ref[...] = raw[TL - HALO:, :]
        act = pre * jax.nn.sigmoid(pre)
        for hh in range(SSD_H):
            xh_ref[hh] = act[:, hh * SSD_P:(hh + 1) * SSD_P]
        for g in range(SSD_G):
            bm_ref[g] = act[:, SSD_DI + g * SSD_N:SSD_DI + (g + 1) * SSD_N]
            cm_ref[g] = act[:, SSD_DI + (SSD_G + g) * SSD_N:SSD_DI + (SSD_G + g + 1) * SSD_N]

    return _pcall(body, name, (nt,),
                  [_tile(D_MODEL), _small((1, D_MODEL)), _res((SSD_IN_PAD, D_MODEL)), _small((SSD_KW, SSD_CONV)),
                   _small((1, SSD_CONV))],
                  [_tile(SSD_DI), _tile(SSD_CONV), _tile(SSD_H),
                   pl.BlockSpec((SSD_H, TL, SSD_P), lambda i: (0, i, 0)),
                   pl.BlockSpec((SSD_G, TL, SSD_N), lambda i: (0, i, 0)),
                   pl.BlockSpec((SSD_G, TL, SSD_N), lambda i: (0, i, 0))],
                  [_sds((L, SSD_DI)), _sds((L, SSD_CONV)), _sds((L, SSD_H)), _sds((SSD_H, L, SSD_P)),
                   _sds((SSD_G, L, SSD_N)), _sds((SSD_G, L, SSD_N))],
                  [pltpu.VMEM((HALO, SSD_CONV), F32)],
                  [x, g_pre, w_in, conv_w, conv_b], jobs)


def _per_head(v, heads):
    return jnp.stack([v[:, h:h + 1] for h in heads], axis=0)


def _heads_to_lanes(cols):
    return jnp.concatenate(cols, axis=1)


def _rep_heads(v):
    g, a, b = v.shape
    return jnp.broadcast_to(v[:, None], (g, SSD_R, a, b)).reshape(g * SSD_R, a, b)


def _sum_heads(v):
    h, a, b = v.shape
    return v.reshape(SSD_G, SSD_R, a, b).sum(axis=1)


def _chunk_terms(dtr, bias, a_log):
    T = CHUNK
    dt = jax.nn.softplus(dtr + bias)
    a_head = -jnp.exp(a_log)
    ii = lax.broadcasted_iota(jnp.int32, (T, T), 0)
    jj = lax.broadcasted_iota(jnp.int32, (T, T), 1)
    tri = ii >= jj
    cs = jnp.dot(tri.astype(F32), dt * a_head, precision=lax.Precision.HIGHEST, preferred_element_type=F32)
    return dict(dt=dt, a_head=a_head, tri=tri, cs=cs, cs_t=cs.T)


def _head_terms(ct, heads):
    T = CHUNK
    cs, cs_t, tri = ct["cs"], ct["cs_t"], ct["tri"]
    csc = _per_head(cs, heads)
    csr = jnp.stack([cs_t[h:h + 1, :] for h in heads], axis=0)
    cl = _per_head(cs[T - 1:T, :], heads)
    lmat = jnp.exp(jnp.where(tri[None], csc - csr, -jnp.inf))
    return dict(dtc=_per_head(ct["dt"], heads), lmat=lmat, ecs=jnp.exp(csc), dsc=jnp.exp(cl - csc), cdc=jnp.exp(cl))


def ssd_scan_fwd(xh, bm, cm, dt_raw, dt_bias, a_log, d_skip, name, jobs=()):
    L = xh.shape[1]
    nc = L // CHUNK
    T = CHUNK
    TS = SCAN_CPS * T

    def body(xh_ref, bm_ref, cm_ref, dt_ref, bias_ref, alog_ref, dsk_ref, y_ref, sp_ref, st_ref):
        c = pl.program_id(0)

        @pl.when(c == 0)
        def _():
            st_ref[...] = jnp.zeros_like(st_ref)

        heads = range(SSD_H)
        dh = _per_head(dsk_ref[...], heads)
        s = st_ref[...]
        for k in range(SCAN_CPS):
            rows = slice(k * T, (k + 1) * T)
            ht = _head_terms(_chunk_terms(dt_ref[rows, :], bias_ref[...], alog_ref[...]), heads)
            x = xh_ref[:, rows, :]
            bgb = bm_ref[:, rows, :].astype(BF16)
            cgb = cm_ref[:, rows, :].astype(BF16)
            bh = _rep_heads(bgb)
            ch = _rep_heads(cgb)
            xt = x * ht["dtc"]
            cb = jnp.einsum("gln,gsn->gls", cgb, bgb, preferred_element_type=F32)
            mb = (_rep_heads(cb) * ht["lmat"]).astype(BF16)
            yd = jnp.einsum("hls,hsp->hlp", mb, xt.astype(BF16), preferred_element_type=F32)
            sb = s.astype(BF16)
            yo = jnp.einsum("hln,hpn->hlp", ch, sb, preferred_element_type=F32) * ht["ecs"]
            y_ref[:, rows, :] = yd + yo + x * dh
            sp_ref[k] = sb
            xd = (xt * ht["dsc"]).astype(BF16)
            s = s * ht["cdc"] + jnp.einsum("htp,htn->hpn", xd, bh, preferred_element_type=F32)
        st_ref[...] = s

    hd = pl.BlockSpec((SSD_H, TS, SSD_P), lambda c: (0, c, 0))
    gr = pl.BlockSpec((SSD_G, TS, SSD_N), lambda c: (0, c, 0))
    return _pcall(body, name, (nc // SCAN_CPS,),
                  [hd, gr, gr, pl.BlockSpec((TS, SSD_H), lambda c: (c, 0)),
                   _small((1, SSD_H)), _small((1, SSD_H)), _small((1, SSD_H))],
                  [hd, pl.BlockSpec((SCAN_CPS, SSD_H, SSD_P, SSD_N), lambda c: (c, 0, 0, 0))],
                  [_sds((SSD_H, L, SSD_P)), _sds((nc, SSD_H, SSD_P, SSD_N), BF16)],
                  [pltpu.VMEM((SSD_H, SSD_P, SSD_N), F32)],
                  [xh, bm, cm, dt_raw, dt_bias, a_log, d_skip], jobs)


def ssd_scan_bwd(dy, xh, bm, cm, dt_raw, sprev, dt_bias, a_log, d_skip, name, jobs=()):
    L = xh.shape[1]
    nc = L // CHUNK
    T = CHUNK

    def body(dy_ref, xh_ref, bm_ref, cm_ref, dt_ref, sp_ref, bias_ref, alog_ref, dsk_ref,
             dxh_ref, dbm_ref, dcm_ref, ddt_ref, dbias_ref, dalog_ref, ddsk_ref, g_ref):
        i = pl.program_id(0)

        @pl.when(i == 0)
        def _():
            g_ref[...] = jnp.zeros_like(g_ref)
            dbias_ref[...] = jnp.zeros_like(dbias_ref)
            dalog_ref[...] = jnp.zeros_like(dalog_ref)
            ddsk_ref[...] = jnp.zeros_like(ddsk_ref)

        bias = bias_ref[...]
        heads = range(SSD_H)
        dh = _per_head(dsk_ref[...], heads)
        g = g_ref[...]
        for k in reversed(range(SCAN_CPS)):
            rows = slice(k * T, (k + 1) * T)
            dtr = dt_ref[rows, :]
            ct = _chunk_terms(dtr, bias, alog_ref[...])
            dt, a_head, tri = ct["dt"], ct["a_head"], ct["tri"]
            ht = _head_terms(ct, heads)
            dtc, lmat, ecs, dsc, cdc = ht["dtc"], ht["lmat"], ht["ecs"], ht["dsc"], ht["cdc"]
            x = xh_ref[:, rows, :]
            dyv = dy_ref[:, rows, :]
            dyb = dyv.astype(BF16)
            bgb = bm_ref[:, rows, :].astype(BF16)
            cgb = cm_ref[:, rows, :].astype(BF16)
            bh = _rep_heads(bgb)
            ch = _rep_heads(cgb)
            sb = sp_ref[k]
            gb = g.astype(BF16)
            xt = x * dtc
            xtb = xt.astype(BF16)
            mf = _rep_heads(jnp.einsum("gln,gsn->gls", cgb, bgb, preferred_element_type=F32)) * lmat
            mb = mf.astype(BF16)
            ddsk = jnp.sum(dyv * x, axis=(1, 2), keepdims=True)
            dx = dyv * dh
            yo_raw = jnp.einsum("hln,hpn->hlp", ch, sb, preferred_element_type=F32)
            w1 = dyv * ecs
            w1b = w1.astype(BF16)
            ds_off = jnp.einsum("hlp,hln->hpn", w1b, ch, preferred_element_type=F32)
            dch = jnp.einsum("hlp,hpn->hln", w1b, sb, preferred_element_type=F32)
            dcs_c = jnp.sum(w1 * yo_raw, axis=2, keepdims=True)
            dm = jnp.einsum("hlp,hsp->hls", dyb, xtb, preferred_element_type=F32)
            dxt = jnp.einsum("hls,hlp->hsp", mb, dyb, preferred_element_type=F32)
            dcbb = _sum_heads(dm * lmat).astype(BF16)
            dseg = dm * mf
            dcs_c = dcs_c + jnp.sum(dseg, axis=2, keepdims=True)
            dcs_r = -jnp.sum(dseg, axis=1, keepdims=True)
            dc = jnp.einsum("gls,gsn->gln", dcbb, bgb, preferred_element_type=F32) + _sum_heads(dch)
            db = jnp.einsum("gls,gln->gsn", dcbb, cgb, preferred_element_type=F32)
            xd = xt * dsc
            dxd = jnp.einsum("htn,hpn->htp", bh, gb, preferred_element_type=F32)
            db = db + _sum_heads(jnp.einsum("htp,hpn->htn", xd.astype(BF16), gb, preferred_element_type=F32))
            dxt = dxt + dxd * dsc
            d_ds = jnp.sum(dxd * xt, axis=2, keepdims=True)
            d_cd = jnp.sum(g * sb.astype(F32), axis=(1, 2), keepdims=True)
            g = g * cdc + ds_off
            t1 = d_ds * dsc
            dcs_c = dcs_c - t1
            dcl = jnp.sum(t1, axis=1, keepdims=True) + d_cd * cdc
            ddt_c = jnp.sum(dxt * x, axis=2, keepdims=True)
            dxh_ref[:, rows, :] = dx + dxt * dtc
            dbm_ref[:, rows, :] = db
            dcm_ref[:, rows, :] = dc
            lanes = lambda v: _heads_to_lanes([v[h] for h in heads])
            rows_t = jnp.concatenate([dcs_r[h] for h in heads], axis=0).T
            last = (lax.broadcasted_iota(jnp.int32, (T, 1), 0) == T - 1).astype(F32)
            dcs = lanes(dcs_c) + rows_t + last * lanes(dcl)
            da = lax.dot_general(tri.astype(F32), dcs, (((0,), (0,)), ((), ())),
                                 precision=lax.Precision.HIGHEST, preferred_element_type=F32)
            ddt = da * a_head + lanes(ddt_c)
            dalog_ref[...] += jnp.sum(da * dt, axis=0, keepdims=True)
            ddtr = ddt * jax.nn.sigmoid(dtr + bias)
            ddt_ref[rows, :] = ddtr
            dbias_ref[...] += jnp.sum(ddtr, axis=0, keepdims=True)
            ddsk_ref[...] += lanes(ddsk)
        g_ref[...] = g

        @pl.when(i == nb - 1)
        def _():
            dalog_ref[...] = dalog_ref[...] * (-jnp.exp(alog_ref[...]))

    nb = nc // SCAN_CPS
    TS = SCAN_CPS * T
    hd = pl.BlockSpec((SSD_H, TS, SSD_P), lambda i: (0, nb - 1 - i, 0))
    gr = pl.BlockSpec((SSD_G, TS, SSD_N), lambda i: (0, nb - 1 - i, 0))
    tk = pl.BlockSpec((TS, SSD_H), lambda i: (nb - 1 - i, 0))
    return _pcall(body, name, (nb,),
                  [hd, hd, gr, gr, tk, pl.BlockSpec((SCAN_CPS, SSD_H, SSD_P, SSD_N), lambda i: (nb - 1 - i, 0, 0, 0)),
                   _small((1, SSD_H)), _small((1, SSD_H)), _small((1, SSD_H))],
                  [hd, gr, gr, tk, _small((1, SSD_H)), _small((1, SSD_H)), _small((1, SSD_H))],
                  [_sds((SSD_H, L, SSD_P)), _sds((SSD_G, L, SSD_N)), _sds((SSD_G, L, SSD_N)), _sds((L, SSD_H)),
                   _sds((1, SSD_H)), _sds((1, SSD_H)), _sds((1, SSD_H))],
                  [pltpu.VMEM((SSD_H, SSD_P, SSD_N), F32)],
                  [dy, xh, bm, cm, dt_raw, sprev, dt_bias, a_log, d_skip], jobs)


def _heads_to_tokens(y_ref):
    return jnp.concatenate([y_ref[h] for h in range(SSD_H)], axis=1)


def ssd_out_fwd(x, y, z, norm_w, w_out, g_post, name, jobs=()):
    L = x.shape[0]
    nt = L // TL

    def body(x_ref, y_ref, z_ref, nw_ref, wout_ref, gpost_ref, m_ref, xn_ref):
        z = z_ref[...]
        yg = _heads_to_tokens(y_ref) * (z * jax.nn.sigmoid(z))
        yn = _rms(yg, nw_ref[...]).astype(BF16)
        m = _mm(yn, wout_ref[...])
        m_ref[...] = m
        xn_ref[...] = x_ref[...] + _rms(m, gpost_ref[...])

    return _pcall(body, name, (nt,),
                  [_tile(D_MODEL), pl.BlockSpec((SSD_H, TL, SSD_P), lambda i: (0, i, 0)), _tile(SSD_DI),
                   _small((1, SSD_DI)), _res((SSD_DI, D_MODEL)), _small((1, D_MODEL))],
                  [_tile(D_MODEL), _tile(D_MODEL)],
                  [_sds((L, D_MODEL)), _sds((L, D_MODEL))],
                  [],
                  [x, y, z, norm_w, w_out, g_post], jobs)


def ssd_out_bwd(dxo, m, y, z, norm_w, w_out, g_post, name, jobs=()):
    L = dxo.shape[0]
    nt = L // TL
    rows = SSD_DI // N_DEV

    def body(dxo_ref, m_ref, y_ref, z_ref, nw_ref, wout_ref, gpost_ref,
             dy_ref, dz_ref, dwout_ref, dgp_ref, dnw_ref, acc_ref, stage_ref):
        i = pl.program_id(0)

        @pl.when(i == 0)
        def _():
            acc_ref[...] = jnp.zeros_like(acc_ref)
            dgp_ref[...] = jnp.zeros_like(dgp_ref)
            dnw_ref[...] = jnp.zeros_like(dnw_ref)

        dm, dgp = _rms_bwd(m_ref[...], gpost_ref[...], dxo_ref[...])
        dgp_ref[...] += dgp
        dmb = dm.astype(BF16)
        dyn = _mm_nt(dmb, wout_ref[...])
        z = z_ref[...]
        y = _heads_to_tokens(y_ref)
        sil, dsil = _silu_parts(z)
        yg = y * sil
        nw = nw_ref[...]
        acc_ref[...] += _mm_tn(_rms(yg, nw).astype(BF16), dmb)
        dyg, dnw = _rms_bwd(yg, nw, dyn)
        dnw_ref[...] += dnw
        dyv = dyg * sil
        dz_ref[...] = dyg * y * dsil
        for h in range(SSD_H):
            dy_ref[h] = dyv[:, h * SSD_P:(h + 1) * SSD_P]

        @pl.when(i == nt - 1)
        def _():
            _emit_row_shards(acc_ref, dwout_ref, stage_ref)

    hd = pl.BlockSpec((SSD_H, TL, SSD_P), lambda i: (0, i, 0))
    return _pcall(body, name, (nt,),
                  [_tile(D_MODEL), _tile(D_MODEL), hd, _tile(SSD_DI), _small((1, SSD_DI)), _res((SSD_DI, D_MODEL)),
                   _small((1, D_MODEL))],
                  [hd, _tile(SSD_DI), _ANY, _small((1, D_MODEL)), _small((1, SSD_DI))],
                  [_sds((SSD_H, L, SSD_P)), _sds((L, SSD_DI)), _sds((N_DEV, rows, D_MODEL), BF16),
                   _sds((1, D_MODEL)), _sds((1, SSD_DI))],
                  [pltpu.VMEM((SSD_DI, D_MODEL), F32), pltpu.VMEM((rows, D_MODEL), BF16)],
                  [dxo, m, y, z, norm_w, w_out, g_post], jobs)


def ssd_conv_bwd(dxh, dbm, dcm, xbc_raw, dz, ddt_raw, conv_w, conv_b, name, jobs=()):
    L = xbc_raw.shape[0]
    nt = L // TL

    def body(dxh_ref, dbm_ref, dcm_ref, raw_ref, halo_ref, dz_ref, ddt_ref, cw_ref, cb_ref,
             d_ref, dcw_ref, dcb_ref, carry_ref):
        i = pl.program_id(0)
        t = nt - 1 - i

        @pl.when(i == 0)
        def _():
            carry_ref[...] = jnp.zeros_like(carry_ref)
            dcw_ref[...] = jnp.zeros_like(dcw_ref)
            dcb_ref[...] = jnp.zeros_like(dcb_ref)

        def dact_chunk(rows, c):
            lo = c * CONV_LC
            if lo < SSD_DI:
                per, ref, first = SSD_P, dxh_ref, lo // SSD_P
            elif lo < SSD_DI + SSD_G * SSD_N:
                per, ref, first = SSD_N, dbm_ref, (lo - SSD_DI) // SSD_N
            else:
                per, ref, first = SSD_N, dcm_ref, (lo - SSD_DI - SSD_G * SSD_N) // SSD_N
            return jnp.concatenate([ref[first + q, rows, :] for q in range(CONV_LC // per)], axis=1)

        for c in range(SSD_CONV // CONV_LC):
            cols = slice(c * CONV_LC, (c + 1) * CONV_LC)
            w = cw_ref[:, cols]
            b = cb_ref[:, cols]
            halo = jnp.where(t == 0, 0.0, halo_ref[:, cols])
            carry = carry_ref[:, cols]
            dcb = jnp.zeros((1, CONV_LC), F32)
            dcw = jnp.zeros((SSD_KW, CONV_LC), F32)
            for r in reversed(range(TL // CONV_RC)):
                r0 = r * CONV_RC
                rows = slice(r0, r0 + CONV_RC)
                if r0 == 0:
                    ext = jnp.concatenate([halo, raw_ref[rows, cols]], axis=0)
                else:
                    ext = raw_ref[r0 - HALO:r0 + CONV_RC, cols]
                taps = _taps(ext, SSD_KW, CONV_RC)
                pre = _conv_fwd(taps, w) + b
                _, dsil = _silu_parts(pre)
                dpre = dact_chunk(rows, c) * dsil
                dcb = dcb + jnp.sum(dpre, axis=0, keepdims=True)
                dcw = dcw + _conv_bwd_w(taps, dpre)
                extd = jnp.concatenate([dpre, carry], axis=0)
                draw = _conv_bwd_in(extd, w, SSD_KW, CONV_RC)
                carry = dpre[:HALO, :]
                d_ref[rows, SSD_DI + c * CONV_LC:SSD_DI + (c + 1) * CONV_LC] = draw.astype(BF16)
            carry_ref[:, cols] = carry
            dcb_ref[:, cols] += dcb
            dcw_ref[:, cols] += dcw
        d_ref[:, :SSD_DI] = dz_ref[...].astype(BF16)
        tail = jnp.concatenate([ddt_ref[...], jnp.zeros((TL, SSD_IN_PAD - SSD_IN), F32)], axis=1)
        d_ref[:, SSD_DI + SSD_CONV:] = tail.astype(BF16)

    hd = pl.BlockSpec((SSD_H, TL, SSD_P), lambda i: (0, nt - 1 - i, 0))
    gr = pl.BlockSpec((SSD_G, TL, SSD_N), lambda i: (0, nt - 1 - i, 0))
    return _pcall(body, name, (nt,),
                  [hd, gr, gr, _rtile(SSD_CONV, nt), _halo_before(SSD_CONV, nt, True), _rtile(SSD_DI, nt),
                   _rtile(SSD_H, nt), _small((SSD_KW, SSD_CONV)), _small((1, SSD_CONV))],
                  [_rtile(SSD_IN_PAD, nt), _small((SSD_KW, SSD_CONV)), _small((1, SSD_CONV))],
                  [_sds((L, SSD_IN_PAD), BF16), _sds((SSD_KW, SSD_CONV)), _sds((1, SSD_CONV))],
                  [pltpu.VMEM((HALO, SSD_CONV), F32)],
                  [dxh, dbm, dcm, xbc_raw, xbc_raw, dz, ddt_raw, conv_w, conv_b], jobs)


def loss_fwd_bwd(y, target, name, jobs=()):
    L = y.shape[0]
    nt = L // TL

    def body(y_ref, t_ref, loss_ref, dy_ref):
        i = pl.program_id(0)

        @pl.when(i == 0)
        def _():
            loss_ref[...] = jnp.zeros_like(loss_ref)

        err = y_ref[...] - t_ref[...]
        dy_ref[...] = err * (1.0 / D_MODEL)
        loss_ref[...] += 0.5 * jnp.sum(jnp.mean(err * err, axis=-1, keepdims=True), axis=0, keepdims=True)

    return _pcall(body, name, (nt,),
                  [_tile(D_MODEL), _tile(D_MODEL)],
                  [_small((1, 1)), _tile(D_MODEL)],
                  [_sds((1, 1)), _sds((L, D_MODEL))],
                  [],
                  [y, target], jobs)


def _adamw_math(w, g, m, v):
    m = ADAM_B1 * m + (1.0 - ADAM_B1) * g
    v = ADAM_B2 * v + (1.0 - ADAM_B2) * (g * g)
    m_hat = m / (1.0 - ADAM_B1 ** ADAM_STEP)
    v_hat = v / (1.0 - ADAM_B2 ** ADAM_STEP)
    delta = -ADAM_LR * (m_hat / (jnp.sqrt(v_hat) + ADAM_EPS) + ADAM_WD * w)
    return delta, m, v


def _row_tile(rows):
    for cand in (256, 176, 128, 64, 32, 16, 8):
        if rows % cand == 0:
            return cand
    return rows


def reduce_adamw(recvs, w, m, v, name, jobs=()):
    nl = len(recvs)
    _, R, C = recvs[0].shape
    if R % 16 == 0:
        tr, tc = _row_tile(R), C
    else:
        tr, tc = R, 2 * LANES
    nr = (R // tr) * (C // tc)

    def body(*refs):
        r_refs = refs[:nl]
        w_ref, m_ref, v_ref, g_out, d_out, m_out, v_out = refs[nl:]
        layer = pl.program_id(0) // nr
        for ll in range(nl):
            @pl.when(layer == ll)
            def _(ll=ll):
                g = r_refs[ll][0].astype(F32)
                for j in range(1, N_DEV):
                    g = g + r_refs[ll][j].astype(F32)
                delta, mn, vn = _adamw_math(w_ref[0], g, m_ref[0], v_ref[0])
                g_out[0] = g
                d_out[0] = delta
                m_out[0] = mn
                v_out[0] = vn

    def recv_spec(ll):
        def index(i):
            t = jnp.where(i // nr == ll, i % nr, 0)
            return (0, t, 0) if tc == C else (0, 0, t)
        return pl.BlockSpec((N_DEV, tr, tc), index)

    blk = pl.BlockSpec((1, tr, tc), lambda i: (i // nr, i % nr, 0) if tc == C else (i // nr, 0, i % nr))
    return _pcall(body, name, (nl * nr,),
                  [recv_spec(ll) for ll in range(nl)] + [blk, blk, blk],
                  [blk] * 4,
                  [_sds((nl, R, C))] * 4,
                  [],
                  [*recvs, w, m, v], jobs)


def small_reduce(gathered, name):
    _, R, C = gathered.shape

    def body(r_ref, o_ref):
        g = r_ref[0]
        for j in range(1, N_DEV):
            g = g + r_ref[j]
        o_ref[...] = g

    return pl.pallas_call(body, name=name, out_shape=_sds((R, C)))(gathered)


def small_adamw(g, w, m, v, name):
    def body(g_ref, w_ref, m_ref, v_ref, d_out, m_out, v_out):
        delta, mn, vn = _adamw_math(w_ref[...], g_ref[...], m_ref[...], v_ref[...])
        d_out[...] = delta
        m_out[...] = mn
        v_out[...] = vn

    return pl.pallas_call(body, name=name, out_shape=[_sds(g.shape)] * 3)(g, w, m, v)


def _pack(arrs):
    flat = jnp.concatenate([a.reshape(-1) for a in arrs])
    n = flat.shape[0]
    rows = -(-n // (8 * LANES)) * 8
    flat = jnp.pad(flat, (0, rows * LANES - n))
    return flat.reshape(rows, LANES)


def _unpack(packed, shapes):
    flat = packed.reshape(-1)
    out = []
    off = 0
    for s in shapes:
        n = 1
        for d in s:
            n *= d
        out.append(flat[off:off + n].reshape(s))
        off += n
    return out


def kernel(x, mix_pre_g, mix_post_g, ffn_pre_g, ffn_post_g, ssd_w_in, ssd_conv_w, ssd_conv_b, ssd_dt_bias, ssd_A_log, ssd_D, ssd_norm_w, ssd_w_out, sc_w_in, sc_conv_w, sc_w_out, ffn_w_up, ffn_conv_w, ffn_conv_b, ffn_w_down, loss_target, m_mix_pre_g, m_mix_post_g, m_ffn_pre_g, m_ffn_post_g, m_ssd_w_in, m_ssd_conv_w, m_ssd_conv_b, m_ssd_dt_bias, m_ssd_A_log, m_ssd_D, m_ssd_norm_w, m_ssd_w_out, m_sc_w_in, m_sc_conv_w, m_sc_w_out, m_ffn_w_up, m_ffn_conv_w, m_ffn_conv_b, m_ffn_w_down, v_mix_pre_g, v_mix_post_g, v_ffn_pre_g, v_ffn_post_g, v_ssd_w_in, v_ssd_conv_w, v_ssd_conv_b, v_ssd_dt_bias, v_ssd_A_log, v_ssd_D, v_ssd_norm_w, v_ssd_w_out, v_sc_w_in, v_sc_conv_w, v_sc_w_out, v_ffn_w_up, v_ffn_conv_w, v_ffn_conv_b, v_ffn_w_down):
    me = _my_index()
    x0 = x[0]
    target = loss_target[0]
    row = lambda a: a.reshape(1, -1)

    tr = lambda a: jnp.transpose(a, (0, 2, 1))
    shards = {"ssd_in": tr(ssd_w_in), "ssd_out": ssd_w_out, "sc_in": sc_w_in, "sc_out": sc_w_out,
              "up": tr(ffn_w_up), "down": ffn_w_down}
    col_sharded = ("sc_in",)
    padded_rows = {"ssd_in": SSD_IN_PAD}
    weights = {}

    def shard_bf16(key):
        n, l = key
        return shards[n][l].astype(BF16)

    def store_weights(keys, outs):
        for (n, l), g in zip(keys, outs):
            _, R, C = g.shape
            if n in col_sharded:
                full = jnp.transpose(g, (1, 0, 2)).reshape(R, N_DEV * C)
            else:
                full = g.reshape(N_DEV * R, C)
                if n in padded_rows:
                    full = jnp.pad(full, ((0, padded_rows[n] - N_DEV * R), (0, 0)))
            weights[(n, l)] = full

    fwd_first_half = {
        "ssd_inproj_0": [("ssd_out", 0), ("up", 0), ("down", 0)],
        "ssd_scan_fwd_0": [("sc_in", 0), ("sc_out", 0), ("up", 1)],
        "ssd_out_fwd_0": [("down", 1)],
        "ffn_fwd_0": [("ssd_in", 1)],
        "sc_fwd_0": [("ssd_out", 1), ("down", 2)],
        "ffn_fwd_1": [("up", 2)],
        "ssd_inproj_1": [("sc_in", 1), ("sc_out", 1), ("down", 3)],
        "ssd_scan_fwd_1": [("up", 3)],
    }
    bwd_sched = {
        "ffn_bwd2_3": [("down", 3)], "sc_bwd2_1": [("sc_out", 1)], "ffn_bwd1_2": [("up", 3)], "ffn_bwd2_2": [("sc_in", 1)],
        "ssd_out_bwd_1": [("down", 2)], "ssd_scan_bwd_1": [("up", 2), ("ssd_out", 1)], "ffn_bwd1_1": [("ssd_in", 1)],
        "ffn_bwd2_1": [("down", 1)], "sc_bwd2_0": [("sc_out", 0)], "ffn_bwd1_0": [("up", 1)], "ffn_bwd2_0": [("sc_in", 0)],
        "ssd_out_bwd_0": [("down", 0)], "ssd_scan_bwd_0": [("up", 0), ("ssd_out", 0)],
    }

    first = [("ssd_in", 0)]
    outs = exchange([("ag2", shard_bf16(k)) for k in first]
                    + [("ag", ssd_conv_w), ("ag", sc_conv_w), ("ag", ffn_conv_w)], "ag_first")
    store_weights(first, outs[:1])

    def taps(g):
        _, nl, K, C = g.shape
        return jnp.transpose(g, (1, 2, 0, 3)).reshape(nl, K, N_DEV * C)

    CW_ssd, CW_sc, CW_ffn = taps(outs[1]), taps(outs[2]), taps(outs[3])

    half_done = []

    def fwd(fn, name, *args):
        second = list(half_done)
        starting = fwd_first_half.get(name, [])
        jobs = [("agB", buf) for _, buf in second] + [("agA", shard_bf16(k)) for k in starting]
        res, got = fn(*args, name, jobs)
        store_weights([k for k, _ in second], got[:len(second)])
        half_done[:] = list(zip(starting, got[len(second):]))
        return res

    saved = []
    h = x0
    for i in range(DEPTH):
        j = i // 2
        blk = dict(x_mix=h)
        if i % 2 == 0:
            z, raw, dt_raw, xh, bm, cm = fwd(ssd_inproj, f"ssd_inproj_{j}", h, row(mix_pre_g[i]), weights[("ssd_in", j)],
                                             CW_ssd[j], row(ssd_conv_b[j]))
            y, sprev = fwd(ssd_scan_fwd, f"ssd_scan_fwd_{j}", xh, bm, cm, dt_raw, row(ssd_dt_bias[j]),
                           row(ssd_A_log[j]), row(ssd_D[j]))
            m, h = fwd(ssd_out_fwd, f"ssd_out_fwd_{j}", h, y, z, row(ssd_norm_w[j]), weights[("ssd_out", j)],
                       row(mix_post_g[i]))
            blk.update(z=z, raw=raw, dt_raw=dt_raw, xh=xh, bm=bm, cm=cm, y=y, sprev=sprev, m=m)
        else:
            bcv, m, h = fwd(sc_fwd, f"sc_fwd_{j}", h, row(mix_pre_g[i]), weights[("sc_in", j)], CW_sc[j],
                            weights[("sc_out", j)], row(mix_post_g[i]))
            blk.update(bcv=bcv, m=m)
        blk["x_ffn"] = h
        up, f, h = fwd(ffn_fwd, f"ffn_fwd_{i}", h, row(ffn_pre_g[i]), weights[("up", i)], CW_ffn[i],
                       row(ffn_conv_b[i]), weights[("down", i)], row(ffn_post_g[i]))
        blk.update(up=up, f=f)
        saved.append(blk)

    (loss_dev, dh), _ = loss_fwd_bwd(h, target, "loss")
    loss = lax.psum(loss_dev[0, 0], ("x", "y", "c"))

    parts, recvd = {}, {}

    def bwd(fn, name, *args):
        keys = bwd_sched.get(name, [])
        res, got = fn(*args, name, [("a2a", parts[k]) for k in keys])
        for k, g in zip(keys, got):
            recvd[k] = g
        return res

    g_mix_pre, g_mix_post, g_ffn_pre, g_ffn_post = [None] * DEPTH, [None] * DEPTH, [None] * DEPTH, [None] * DEPTH
    g_ffn_cw, g_ffn_cb = [None] * DEPTH, [None] * DEPTH
    g_ssd_cw, g_ssd_cb, g_ssd_dtb, g_ssd_alog, g_ssd_d, g_ssd_nw = ([None] * 2 for _ in range(6))
    g_sc_cw = [None] * 2
    for i in reversed(range(DEPTH)):
        j = i // 2
        blk = saved[i]
        dup, parts[("down", i)], g_ffn_post[i], g_ffn_cw[i], g_ffn_cb[i] = bwd(
            ffn_bwd1, f"ffn_bwd1_{i}", dh, blk["f"], blk["up"], row(ffn_post_g[i]), weights[("down", i)], CW_ffn[i],
            row(ffn_conv_b[i]))
        dh, parts[("up", i)], g_ffn_pre[i] = bwd(functools.partial(inproj_bwd, transposed=True), f"ffn_bwd2_{i}",
                                                  blk["x_ffn"], row(ffn_pre_g[i]), dup, weights[("up", i)], dh,
                                                  2 * FFN_F // N_DEV)
        if i % 2 == 0:
            dy, dz, parts[("ssd_out", j)], g_mix_post[i], g_ssd_nw[j] = bwd(
                ssd_out_bwd, f"ssd_out_bwd_{j}", dh, blk["m"], blk["y"], blk["z"], row(ssd_norm_w[j]),
                weights[("ssd_out", j)], row(mix_post_g[i]))
            dxh, dbm, dcm, ddt, g_ssd_dtb[j], g_ssd_alog[j], g_ssd_d[j] = bwd(
                ssd_scan_bwd, f"ssd_scan_bwd_{j}", dy, blk["xh"], blk["bm"], blk["cm"], blk["dt_raw"], blk["sprev"],
                row(ssd_dt_bias[j]), row(ssd_A_log[j]), row(ssd_D[j]))
            d_in, g_ssd_cw[j], g_ssd_cb[j] = bwd(ssd_conv_bwd, f"ssd_conv_bwd_{j}", dxh, dbm, dcm, blk["raw"], dz, ddt,
                                                 CW_ssd[j], row(ssd_conv_b[j]))
            dh, parts[("ssd_in", j)], g_mix_pre[i] = bwd(functools.partial(inproj_bwd, transposed=True),
                                                          f"ssd_bwd2_{j}", blk["x_mix"], row(mix_pre_g[i]), d_in,
                                                          weights[("ssd_in", j)], dh, SSD_IN // N_DEV)
        else:
            dbcv, parts[("sc_out", j)], g_mix_post[i], g_sc_cw[j] = bwd(
                sc_bwd1, f"sc_bwd1_{j}", dh, blk["m"], blk["bcv"], row(mix_post_g[i]), weights[("sc_out", j)], CW_sc[j])
            dh, parts[("sc_in", j)], g_mix_pre[i] = bwd(inproj_bwd, f"sc_bwd2_{j}", blk["x_mix"], row(mix_pre_g[i]),
                                                         dbcv, weights[("sc_in", j)], dh, 3 * D_MODEL // N_DEV)
    grad_x = dh[None]

    st = lambda lst: jnp.concatenate(lst, axis=0)
    small_full = [
        st(g_mix_pre), st(g_mix_post), st(g_ffn_pre), st(g_ffn_post),
        jnp.stack(g_ssd_cw), st(g_ssd_cb), st(g_ssd_dtb), st(g_ssd_alog), st(g_ssd_d), st(g_ssd_nw),
        jnp.stack(g_sc_cw), jnp.stack(g_ffn_cw), st(g_ffn_cb),
    ]
    full_shapes = [a.shape for a in small_full]

    recvd[("ssd_in", 0)], small_gathered = exchange([("a2a", parts[("ssd_in", 0)]), ("ag", _pack(small_full))],
                                                    "a2a_last")

    def finish(n, nl, w, m, v):
        return reduce_adamw([recvd[(n, l)] for l in range(nl)], w, m, v, "adamw_" + n)[0]

    r_up = [tr(a) for a in finish("up", DEPTH, tr(ffn_w_up), tr(m_ffn_w_up), tr(v_ffn_w_up))]
    r_down = finish("down", DEPTH, ffn_w_down, m_ffn_w_down, v_ffn_w_down)
    r_ssd_out = finish("ssd_out", 2, ssd_w_out, m_ssd_w_out, v_ssd_w_out)
    r_sc_in = finish("sc_in", 2, sc_w_in, m_sc_w_in, v_sc_w_in)
    r_sc_out = finish("sc_out", 2, sc_w_out, m_sc_w_out, v_sc_w_out)
    r_ssd_in = [tr(a) for a in finish("ssd_in", 2, tr(ssd_w_in), tr(m_ssd_w_in), tr(v_ssd_w_in))]

    summed = small_reduce(small_gathered, "small_reduce")
    (s_mix_pre, s_mix_post, s_ffn_pre, s_ffn_post, s_ssd_cw, s_ssd_cb, s_ssd_dtb, s_ssd_alog, s_ssd_d, s_ssd_nw,
     s_sc_cw, s_ffn_cw, s_ffn_cb) = _unpack(summed, full_shapes)

    def my_cols(a, width):
        return lax.dynamic_slice_in_dim(a, me * width, width, axis=a.ndim - 1)

    s_ssd_cw = my_cols(s_ssd_cw, SSD_CONV // N_DEV)
    s_sc_cw = my_cols(s_sc_cw, D_MODEL // N_DEV)
    s_ffn_cw = my_cols(s_ffn_cw, FFN_F // N_DEV)

    small_g = [s_mix_pre, s_mix_post, s_ffn_pre, s_ffn_post, s_ssd_cw, s_ssd_cb, s_ssd_dtb, s_ssd_alog, s_ssd_d,
               s_ssd_nw, s_sc_cw, s_ffn_cw, s_ffn_cb]
    small_w = [mix_pre_g, mix_post_g, ffn_pre_g, ffn_post_g, ssd_conv_w, ssd_conv_b, ssd_dt_bias, ssd_A_log, ssd_D,
               ssd_norm_w, sc_conv_w, ffn_conv_w, ffn_conv_b]
    small_m = [m_mix_pre_g, m_mix_post_g, m_ffn_pre_g, m_ffn_post_g, m_ssd_conv_w, m_ssd_conv_b, m_ssd_dt_bias,
               m_ssd_A_log, m_ssd_D, m_ssd_norm_w, m_sc_conv_w, m_ffn_conv_w, m_ffn_conv_b]
    small_v = [v_mix_pre_g, v_mix_post_g, v_ffn_pre_g, v_ffn_post_g, v_ssd_conv_w, v_ssd_conv_b, v_ssd_dt_bias,
               v_ssd_A_log, v_ssd_D, v_ssd_norm_w, v_sc_conv_w, v_ffn_conv_w, v_ffn_conv_b]
    local_shapes = [a.shape for a in small_w]
    pd, pm, pv = small_adamw(_pack(small_g), _pack(small_w), _pack(small_m), _pack(small_v), "small_adamw")
    sd = _unpack(pd, local_shapes)
    sm = _unpack(pm, local_shapes)
    sv = _unpack(pv, local_shapes)

    def ordered(small, big):
        (mix_pre, mix_post, ffn_pre, ffn_post, ssd_cw, ssd_cb, dtb, alog, dsk, nw, sc_cw, ffn_cw, ffn_cb) = small
        (b_ssd_in, b_ssd_out, b_sc_in, b_sc_out, b_up, b_down) = big
        return [mix_pre, mix_post, ffn_pre, ffn_post, b_ssd_in, ssd_cw, ssd_cb, dtb, alog, dsk, nw, b_ssd_out,
                b_sc_in, sc_cw, b_sc_out, b_up, ffn_cw, ffn_cb, b_down]

    bigs = [r_ssd_in, r_ssd_out, r_sc_in, r_sc_out, r_up, r_down]
    grads = ordered(small_g, [r[0] for r in bigs])
    deltas = ordered(sd, [r[1] for r in bigs])
    new_m = ordered(sm, [r[2] for r in bigs])
    new_v = ordered(sv, [r[3] for r in bigs])
    return (loss, grad_x, *grads, *deltas, *new_m, *new_v)
```

```python
import functools

import jax
import jax.numpy as jnp
from jax import lax
from jax.experimental import pallas as pl
from jax.experimental.pallas import tpu as pltpu

F32 = jnp.float32
BF16 = jnp.bfloat16

EPS = 1e-6
D_MODEL = 1024
DEPTH = 4
N_DEV = 8
CHUNK = 64
SSD_DI = 2048
SSD_H = 32
SSD_P = 64
SSD_G = 8
SSD_R = SSD_H // SSD_G
SSD_N = 128
SSD_CONV = SSD_DI + 2 * SSD_G * SSD_N
SSD_IN = SSD_DI + SSD_CONV + SSD_H
LANES = 128
SSD_IN_PAD = -(-SSD_IN // LANES) * LANES
SSD_KW = 4
SC_KW = 3
FFN_F = 2816
FFN_KW = 3
SCAN_CPS = 2
TL = 256
HALO = 8
CONV_RC = 128
CONV_LC = 256
VMEM_LIMIT = 60 * 1024 * 1024

ADAM_LR = 0.001
ADAM_B1 = 0.9
ADAM_B2 = 0.999
ADAM_EPS = 1e-08
ADAM_WD = 0.01
ADAM_STEP = 10

MESH = pl.DeviceIdType.MESH


def _rms(x, g):
    r = lax.rsqrt(jnp.mean(x * x, axis=-1, keepdims=True) + EPS)
    return x * r * g


def _rms_bwd(x, g, dy):
    r = lax.rsqrt(jnp.mean(x * x, axis=-1, keepdims=True) + EPS)
    xh = x * r
    dg = jnp.sum(dy * xh, axis=0, keepdims=True)
    dxh = dy * g
    dx = r * (dxh - xh * jnp.mean(dxh * xh, axis=-1, keepdims=True))
    return dx, dg


def _mm(a, b):
    return jnp.dot(a, b, preferred_element_type=F32)


def _mm_nt(a, b):
    return lax.dot_general(a, b, (((1,), (1,)), ((), ())), preferred_element_type=F32)


def _mm_tn(a, b):
    return lax.dot_general(a, b, (((0,), (0,)), ((), ())), preferred_element_type=F32)


def _silu_parts(x):
    sg = jax.nn.sigmoid(x)
    return x * sg, sg * (1.0 + x * (1.0 - sg))


def _rows_from(a, s, tl):
    if s % HALO == 0:
        return a[s:s + tl]
    return pltpu.roll(a, a.shape[0] - s, 0)[0:tl]


def _taps(ext, kw, tl):
    base = HALO - (kw - 1)
    return [_rows_from(ext, base + j, tl) for j in range(kw)]


def _conv_fwd(taps, w):
    out = taps[0] * w[0:1]
    for j in range(1, len(taps)):
        out = out + taps[j] * w[j:j + 1]
    return out


def _conv_bwd_in(extd, w, kw, tl):
    out = _rows_from(extd, kw - 1, tl) * w[0:1]
    for j in range(1, kw):
        out = out + _rows_from(extd, kw - 1 - j, tl) * w[j:j + 1]
    return out


def _conv_bwd_w(taps, dy):
    return jnp.concatenate([jnp.sum(dy * t, axis=0, keepdims=True) for t in taps], axis=0)


def _emit_row_shards(acc_ref, out_ref, stage_ref):
    rows = out_ref.shape[1]
    for k in range(N_DEV):
        stage_ref[...] = acc_ref[k * rows:(k + 1) * rows, :].astype(BF16)
        pltpu.sync_copy(stage_ref, out_ref.at[k])


def _emit_col_shards(acc_ref, out_ref, stage_ref):
    cols = out_ref.shape[2]
    for k in range(N_DEV):
        stage_ref[...] = acc_ref[:, k * cols:(k + 1) * cols].astype(BF16)
        pltpu.sync_copy(stage_ref, out_ref.at[k])


def _res(shape):
    nd = len(shape)
    return pl.BlockSpec(shape, lambda i: (0,) * nd, pipeline_mode=pl.Buffered(1))


def _small(shape):
    nd = len(shape)
    return pl.BlockSpec(shape, lambda i: (0,) * nd)


def _tile(n):
    return pl.BlockSpec((TL, n), lambda i: (i, 0))


def _rtile(n, nt):
    return pl.BlockSpec((TL, n), lambda i: (nt - 1 - i, 0))


def _halo_before(n, nt, reverse):
    per = TL // HALO
    if reverse:
        return pl.BlockSpec((HALO, n), lambda i: (jnp.maximum((nt - 1 - i) * per - 1, 0), 0))
    return pl.BlockSpec((HALO, n), lambda i: (jnp.maximum(i * per - 1, 0), 0))


_ANY = pl.BlockSpec(memory_space=pl.ANY)


def _sds(shape, dtype=F32):
    return jax.ShapeDtypeStruct(shape, dtype)


def _peer(k):
    x, y, c = lax.axis_index("x"), lax.axis_index("y"), lax.axis_index("c")
    px = x ^ (k >> 2)
    py = y ^ ((k >> 1) & 1)
    pc = c ^ (k & 1)
    return (px, py, pc), 4 * px + 2 * py + pc


def _my_index():
    return 4 * lax.axis_index("x") + 2 * lax.axis_index("y") + lax.axis_index("c")


SIBLING = 1
SAME_CORE_CHIPS = (2, 4, 6)


def _job_copies(kind, src_ref, out_ref, send_sems, recv_sems, local_sems, j):
    me = _my_index()
    sends, recvs = [], []

    def pair(pattern, sem, src, put_slot, get_slot):
        dev, _ = _peer(pattern)
        sems = dict(send_sem=send_sems.at[j, sem], recv_sem=recv_sems.at[j, sem], device_id=dev, device_id_type=MESH)
        sends.append(pltpu.make_async_remote_copy(src_ref=src, dst_ref=out_ref.at[put_slot], **sems))
        recvs.append(pltpu.make_async_remote_copy(src_ref=src, dst_ref=out_ref.at[get_slot], **sems))

    if kind == "agB":
        for k in SAME_CORE_CHIPS:
            _, mine_from_k = _peer(k)
            _, sib_from_k = _peer(k | SIBLING)
            pair(SIBLING, k, out_ref.at[mine_from_k], mine_from_k, sib_from_k)
        return None, sends, recvs
    patterns = (SIBLING,) + SAME_CORE_CHIPS if kind == "agA" else range(1, N_DEV)
    mine = src_ref.at[me] if kind == "a2a" else src_ref
    local = pltpu.make_async_copy(mine, out_ref.at[me], local_sems.at[j])
    for k in patterns:
        _, idx = _peer(k)
        pair(k, k - 1, src_ref.at[idx] if kind == "a2a" else src_ref, me, idx)
    return local, sends, recvs


def _pcall(body, name, grid, in_specs, out_specs, out_shape, scratch_shapes, args, jobs=()):
    n_in, n_out, nj = len(in_specs), len(out_specs), len(jobs)
    last = grid[0] - 1
    kinds = [k for k, _ in jobs]

    def wrapped(*refs):
        ins = refs[:n_in]
        csrc = refs[n_in:n_in + nj]
        outs = refs[n_in + nj:n_in + nj + n_out]
        cout = refs[n_in + nj + n_out:n_in + 2 * nj + n_out]
        rest = refs[n_in + 2 * nj + n_out:]

        def copies(j, kind):
            return _job_copies(kind, csrc[j], cout[j], send_sems, recv_sems, local_sems, j)

        def start(j, kind):
            local, sends, _ = copies(j, kind)
            if local is not None:
                local.start()
            for cp in sends:
                cp.start()

        def finish(j, kind, arrivals_only=False):
            local, sends, recvs = copies(j, kind)
            for cp in recvs:
                cp.wait_recv()
            if not arrivals_only:
                for cp in sends:
                    cp.wait_send()
                if local is not None:
                    local.wait()

        if nj:
            scratch, (send_sems, recv_sems, local_sems) = rest[:-3], rest[-3:]
            i = pl.program_id(0)

            @pl.when(i == 0)
            def _():
                for j in range(nj):
                    start(j, "agA" if kinds[j] == "ag2" else kinds[j])
                for j in range(nj):
                    if kinds[j] == "ag2":
                        finish(j, "agA", arrivals_only=True)
                        start(j, "agB")
        else:
            scratch = rest
        body(*ins, *outs, *scratch)
        if nj:
            @pl.when(i == last)
            def _():
                for j in range(nj):
                    if kinds[j] == "ag2":
                        finish(j, "agB")
                        _, sends, _ = copies(j, "agA")
                        for cp in sends:
                            cp.wait_send()
                        copies(j, "agA")[0].wait()
                    else:
                        finish(j, kinds[j])

    job_shapes = []
    aliases = {}
    for j, (kind, s) in enumerate(jobs):
        shp = (N_DEV,) + tuple(s.shape) if kind in ("ag", "agA", "ag2") else tuple(s.shape)
        job_shapes.append(_sds(shp, s.dtype))
        if kind == "agB":
            aliases[n_in + j] = n_out + j
    sems = [pltpu.SemaphoreType.DMA((nj, N_DEV - 1)), pltpu.SemaphoreType.DMA((nj, N_DEV - 1)),
            pltpu.SemaphoreType.DMA((nj,))] if nj else []
    res = pl.pallas_call(
        wrapped, name=name, grid=grid,
        in_specs=list(in_specs) + [_ANY] * nj,
        out_specs=list(out_specs) + [_ANY] * nj,
        out_shape=list(out_shape) + job_shapes,
        scratch_shapes=list(scratch_shapes) + sems,
        input_output_aliases=aliases,
        compiler_params=pltpu.CompilerParams(dimension_semantics=("arbitrary",), vmem_limit_bytes=VMEM_LIMIT,
                                             has_side_effects=bool(nj)),
    )(*args, *[s for _, s in jobs])
    return list(res[:n_out]), list(res[n_out:])


def exchange(jobs, name):
    def body(o_ref):
        o_ref[...] = jnp.zeros_like(o_ref)

    _, outs = _pcall(body, name, (1,), [], [_small((8, LANES))], [_sds((8, LANES))], [], [], jobs)
    return outs


def ffn_fwd(x, g_pre, w_up, conv_w, conv_b, w_down, g_post, name, jobs=()):
    L = x.shape[0]
    nt = L // TL
    F = FFN_F

    def body(x_ref, gpre_ref, wup_ref, cw_ref, cb_ref, wdn_ref, gpost_ref, up_ref, f_ref, xn_ref, carry_ref):
        i = pl.program_id(0)

        @pl.when(i == 0)
        def _():
            carry_ref[...] = jnp.zeros_like(carry_ref)

        x = x_ref[...]
        h = _rms(x, gpre_ref[...]).astype(BF16)
        up = _mm_nt(h, wup_ref[...])
        up_ref[...] = up
        ug = up[:, :F]
        val = up[:, F:]
        ext = jnp.concatenate([carry_ref[...], ug], axis=0)
        gate = _conv_fwd(_taps(ext, FFN_KW, TL), cw_ref[...]) + cb_ref[...]
        carry_ref[...] = ug[TL - HALO:, :]
        a = (gate * jax.nn.sigmoid(gate) * val).astype(BF16)
        f = _mm(a, wdn_ref[...])
        f_ref[...] = f
        xn_ref[...] = x + _rms(f, gpost_ref[...])

    return _pcall(body, name, (nt,),
                  [_tile(D_MODEL), _small((1, D_MODEL)), _res((2 * F, D_MODEL)), _small((FFN_KW, F)), _small((1, F)),
                   _res((F, D_MODEL)), _small((1, D_MODEL))],
                  [_tile(2 * F), _tile(D_MODEL), _tile(D_MODEL)],
                  [_sds((L, 2 * F)), _sds((L, D_MODEL)), _sds((L, D_MODEL))],
                  [pltpu.VMEM((HALO, F), F32)],
                  [x, g_pre, w_up, conv_w, conv_b, w_down, g_post], jobs)


def ffn_bwd1(dxo, f, up, g_post, w_down, conv_w, conv_b, name, jobs=()):
    L = dxo.shape[0]
    nt = L // TL
    F = FFN_F
    rows = F // N_DEV

    def body(dxo_ref, f_ref, up_ref, halo_ref, gpost_ref, wdn_ref, cw_ref, cb_ref,
             dup_ref, dwdn_ref, dgp_ref, dcw_ref, dcb_ref, acc_ref, carry_ref, stage_ref):
        i = pl.program_id(0)
        t = nt - 1 - i

        @pl.when(i == 0)
        def _():
            acc_ref[...] = jnp.zeros_like(acc_ref)
            carry_ref[...] = jnp.zeros_like(carry_ref)
            dgp_ref[...] = jnp.zeros_like(dgp_ref)
            dcw_ref[...] = jnp.zeros_like(dcw_ref)
            dcb_ref[...] = jnp.zeros_like(dcb_ref)

        df, dgp = _rms_bwd(f_ref[...], gpost_ref[...], dxo_ref[...])
        dgp_ref[...] += dgp
        dfb = df.astype(BF16)
        for c in range(F // CONV_LC):
            cols = slice(c * CONV_LC, (c + 1) * CONV_LC)
            vcols = slice(F + c * CONV_LC, F + (c + 1) * CONV_LC)
            da = _mm_nt(dfb, wdn_ref[cols, :])
            w = cw_ref[:, cols]
            b = cb_ref[:, cols]
            halo = jnp.where(t == 0, 0.0, halo_ref[:, cols])
            carry = carry_ref[:, cols]
            dcb = jnp.zeros((1, CONV_LC), F32)
            dcw = jnp.zeros((FFN_KW, CONV_LC), F32)
            a_rows = [None] * (TL // CONV_RC)
            for r in reversed(range(TL // CONV_RC)):
                r0 = r * CONV_RC
                rows = slice(r0, r0 + CONV_RC)
                if r0 == 0:
                    ext = jnp.concatenate([halo, up_ref[rows, cols]], axis=0)
                else:
                    ext = up_ref[r0 - HALO:r0 + CONV_RC, cols]
                val = up_ref[rows, vcols]
                taps = _taps(ext, FFN_KW, CONV_RC)
                gate = _conv_fwd(taps, w) + b
                s, ds = _silu_parts(gate)
                a_rows[r] = (s * val).astype(BF16)
                da_r = da[rows, :]
                dgate = da_r * val * ds
                dcb = dcb + jnp.sum(dgate, axis=0, keepdims=True)
                dcw = dcw + _conv_bwd_w(taps, dgate)
                extd = jnp.concatenate([dgate, carry], axis=0)
                dup_ref[rows, cols] = _conv_bwd_in(extd, w, FFN_KW, CONV_RC).astype(BF16)
                dup_ref[rows, vcols] = (da_r * s).astype(BF16)
                carry = dgate[:HALO, :]
            carry_ref[:, cols] = carry
            dcb_ref[:, cols] += dcb
            dcw_ref[:, cols] += dcw
            acc_ref[cols, :] += _mm_tn(jnp.concatenate(a_rows, axis=0), dfb)

        @pl.when(i == nt - 1)
        def _():
            _emit_row_shards(acc_ref, dwdn_ref, stage_ref)

    return _pcall(body, name, (nt,),
                  [_rtile(D_MODEL, nt), _rtile(D_MODEL, nt), _rtile(2 * F, nt), _halo_before(F, nt, True),
                   _small((1, D_MODEL)), _res((F, D_MODEL)), _small((FFN_KW, F)), _small((1, F))],
                  [_rtile(2 * F, nt), _ANY, _small((1, D_MODEL)), _small((FFN_KW, F)), _small((1, F))],
                  [_sds((L, 2 * F), BF16), _sds((N_DEV, rows, D_MODEL), BF16), _sds((1, D_MODEL)),
                   _sds((FFN_KW, F)), _sds((1, F))],
                  [pltpu.VMEM((F, D_MODEL), F32), pltpu.VMEM((HALO, F), F32), pltpu.VMEM((rows, D_MODEL), BF16)],
                  [dxo, f, up, up, g_post, w_down, conv_w, conv_b], jobs)


def inproj_bwd(x, g_pre, d, w, dxo, cols, name, jobs=(), transposed=False):
    L = x.shape[0]
    nt = L // TL
    N = d.shape[1]
    w_shape = (N, D_MODEL) if transposed else (D_MODEL, N)
    shard_shape = (cols, D_MODEL) if transposed else (D_MODEL, cols)

    def body(x_ref, g_ref, d_ref, w_ref, dxo_ref, dx_ref, dw_ref, dg_ref, acc_ref, stage_ref):
        i = pl.program_id(0)

        @pl.when(i == 0)
        def _():
            acc_ref[...] = jnp.zeros_like(acc_ref)
            dg_ref[...] = jnp.zeros_like(dg_ref)

        x = x_ref[...]
        g = g_ref[...]
        d = d_ref[...]
        h = _rms(x, g).astype(BF16)
        if transposed:
            dh = _mm(d, w_ref[...])
            acc_ref[...] += _mm_tn(d, h)
        else:
            dh = _mm_nt(d, w_ref[...])
            acc_ref[...] += _mm_tn(h, d)
        dxn, dg = _rms_bwd(x, g, dh)
        dx_ref[...] = dxo_ref[...] + dxn
        dg_ref[...] += dg

        @pl.when(i == nt - 1)
        def _():
            (_emit_row_shards if transposed else _emit_col_shards)(acc_ref, dw_ref, stage_ref)

    return _pcall(body, name, (nt,),
                  [_tile(D_MODEL), _small((1, D_MODEL)), _tile(N), _res(w_shape), _tile(D_MODEL)],
                  [_tile(D_MODEL), _ANY, _small((1, D_MODEL))],
                  [_sds((L, D_MODEL)), _sds((N_DEV,) + shard_shape, BF16), _sds((1, D_MODEL))],
                  [pltpu.VMEM(w_shape, F32), pltpu.VMEM(shard_shape, BF16)],
                  [x, g_pre, d, w, dxo], jobs)


def sc_fwd(x, g_pre, w_in, conv_w, w_out, g_post, name, jobs=()):
    L = x.shape[0]
    nt = L // TL
    W = D_MODEL

    def body(x_ref, gpre_ref, win_ref, cw_ref, wout_ref, gpost_ref, bcv_ref, m_ref, xn_ref, carry_ref):
        i = pl.program_id(0)

        @pl.when(i == 0)
        def _():
            carry_ref[...] = jnp.zeros_like(carry_ref)

        x = x_ref[...]
        h = _rms(x, gpre_ref[...]).astype(BF16)
        bcv = _mm(h, win_ref[...])
        bcv_ref[...] = bcv
        gb = bcv[:, :W]
        p = bcv[:, W:2 * W] * bcv[:, 2 * W:]
        ext = jnp.concatenate([carry_ref[...], p], axis=0)
        u = _conv_fwd(_taps(ext, SC_KW, TL), cw_ref[...])
        carry_ref[...] = p[TL - HALO:, :]
        m = _mm((gb * u).astype(BF16), wout_ref[...])
        m_ref[...] = m
        xn_ref[...] = x + _rms(m, gpost_ref[...])

    return _pcall(body, name, (nt,),
                  [_tile(W), _small((1, W)), _res((W, 3 * W)), _small((SC_KW, W)), _res((W, W)), _small((1, W))],
                  [_tile(3 * W), _tile(W), _tile(W)],
                  [_sds((L, 3 * W)), _sds((L, W)), _sds((L, W))],
                  [pltpu.VMEM((HALO, W), F32)],
                  [x, g_pre, w_in, conv_w, w_out, g_post], jobs)


def sc_bwd1(dxo, m, bcv, g_post, w_out, conv_w, name, jobs=()):
    L = dxo.shape[0]
    nt = L // TL
    W = D_MODEL
    rows = W // N_DEV

    def body(dxo_ref, m_ref, bcv_ref, halo_ref, gpost_ref, wout_ref, cw_ref,
             dbcv_ref, dwout_ref, dgp_ref, dcw_ref, acc_ref, carry_ref, stage_ref):
        i = pl.program_id(0)
        t = nt - 1 - i

        @pl.when(i == 0)
        def _():
            acc_ref[...] = jnp.zeros_like(acc_ref)
            carry_ref[...] = jnp.zeros_like(carry_ref)
            dgp_ref[...] = jnp.zeros_like(dgp_ref)
            dcw_ref[...] = jnp.zeros_like(dcw_ref)

        dm, dgp = _rms_bwd(m_ref[...], gpost_ref[...], dxo_ref[...])
        dgp_ref[...] += dgp
        dmb = dm.astype(BF16)
        dq = _mm_nt(dmb, wout_ref[...])
        bcv = bcv_ref[...]
        gb = bcv[:, :W]
        gc = bcv[:, W:2 * W]
        v = bcv[:, 2 * W:]
        hb = halo_ref[...]
        halo = jnp.where(t == 0, 0.0, hb[:, W:2 * W] * hb[:, 2 * W:])
        ext = jnp.concatenate([halo, gc * v], axis=0)
        w = cw_ref[...]
        taps = _taps(ext, SC_KW, TL)
        u = _conv_fwd(taps, w)
        acc_ref[...] += _mm_tn((gb * u).astype(BF16), dmb)
        dgb = dq * u
        du = dq * gb
        dcw_ref[...] += _conv_bwd_w(taps, du)
        extd = jnp.concatenate([du, carry_ref[...]], axis=0)
        dp = _conv_bwd_in(extd, w, SC_KW, TL)
        carry_ref[...] = du[:HALO, :]
        dbcv_ref[...] = jnp.concatenate([dgb, dp * v, dp * gc], axis=1).astype(BF16)

        @pl.when(i == nt - 1)
        def _():
            _emit_row_shards(acc_ref, dwout_ref, stage_ref)

    return _pcall(body, name, (nt,),
                  [_rtile(W, nt), _rtile(W, nt), _rtile(3 * W, nt), _halo_before(3 * W, nt, True),
                   _small((1, W)), _res((W, W)), _small((SC_KW, W))],
                  [_rtile(3 * W, nt), _ANY, _small((1, W)), _small((SC_KW, W))],
                  [_sds((L, 3 * W), BF16), _sds((N_DEV, rows, W), BF16), _sds((1, W)), _sds((SC_KW, W))],
                  [pltpu.VMEM((W, W), F32), pltpu.VMEM((HALO, W), F32), pltpu.VMEM((rows, W), BF16)],
                  [dxo, m, bcv, bcv, g_post, w_out, conv_w], jobs)


def ssd_inproj(x, g_pre, w_in, conv_w, conv_b, name, jobs=()):
    L = x.shape[0]
    nt = L // TL

    def body(x_ref, gpre_ref, win_ref, cw_ref, cb_ref, z_ref, raw_ref, dt_ref, xh_ref, bm_ref, cm_ref, carry_ref):
        i = pl.program_id(0)

        @pl.when(i == 0)
        def _():
            carry_ref[...] = jnp.zeros_like(carry_ref)

        h = _rms(x_ref[...], gpre_ref[...]).astype(BF16)
        zx = _mm(h, win_ref[...])
        z_ref[...] = zx[:, :SSD_DI]
        raw = zx[:, SSD_DI:SSD_DI + SSD_CONV]
        raw_ref[...] = raw
        dt_ref[...] = zx[:, SSD_DI + SSD_CONV:SSD_IN]
        ext = jnp.concatenate([carry_ref[...], raw], axis=0)
        pre = _conv_fwd(_taps(ext, SSD_KW, TL), cw_ref[...]) + cb_ref[...]
        carry_ref[...] = raw[TL - HALO:, :]
        act = pre * jax.nn.sigmoid(pre)
        for hh in range(SSD_H):
            xh_ref[hh] = act[:, hh * SSD_P:(hh + 1) * SSD_P]
        for g in range(SSD_G):
            bm_ref[g] = act[:, SSD_DI + g * SSD_N:SSD_DI + (g + 1) * SSD_N]
            cm_ref[g] = act[:, SSD_DI + (SSD_G + g) * SSD_N:SSD_DI + (SSD_G + g + 1) * SSD_N]

    return _pcall(body, name, (nt,),
                  [_tile(D_MODEL), _small((1, D_MODEL)), _res((D_MODEL, SSD_IN_PAD)), _small((SSD_KW, SSD_CONV)),
                   _small((1, SSD_CONV))],
                  [_tile(SSD_DI), _tile(SSD_CONV), _tile(SSD_H),
                   pl.BlockSpec((SSD_H, TL, SSD_P), lambda i: (0, i, 0)),
                   pl.BlockSpec((SSD_G, TL, SSD_N), lambda i: (0, i, 0)),
                   pl.BlockSpec((SSD_G, TL, SSD_N), lambda i: (0, i, 0))],
                  [_sds((L, SSD_DI)), _sds((L, SSD_CONV)), _sds((L, SSD_H)), _sds((SSD_H, L, SSD_P)),
                   _sds((SSD_G, L, SSD_N)), _sds((SSD_G, L, SSD_N))],
                  [pltpu.VMEM((HALO, SSD_CONV), F32)],
                  [x, g_pre, w_in, conv_w, conv_b], jobs)


def _per_head(v, heads):
    return jnp.stack([v[:, h:h + 1] for h in heads], axis=0)


def _heads_to_lanes(cols):
    return jnp.concatenate(cols, axis=1)


def _rep_heads(v):
    g, a, b = v.shape
    return jnp.broadcast_to(v[:, None], (g, SSD_R, a, b)).reshape(g * SSD_R, a, b)


def _sum_heads(v):
    h, a, b = v.shape
    return v.reshape(SSD_G, SSD_R, a, b).sum(axis=1)


def _chunk_terms(dtr, bias, a_log):
    T = CHUNK
    dt = jax.nn.softplus(dtr + bias)
    a_head = -jnp.exp(a_log)
    ii = lax.broadcasted_iota(jnp.int32, (T, T), 0)
    jj = lax.broadcasted_iota(jnp.int32, (T, T), 1)
    tri = ii >= jj
    cs = jnp.dot(tri.astype(F32), dt * a_head, precision=lax.Precision.HIGHEST, preferred_element_type=F32)
    return dict(dt=dt, a_head=a_head, tri=tri, cs=cs, cs_t=cs.T)


def _head_terms(ct, heads):
    T = CHUNK
    cs, cs_t, tri = ct["cs"], ct["cs_t"], ct["tri"]
    csc = _per_head(cs, heads)
    csr = jnp.stack([cs_t[h:h + 1, :] for h in heads], axis=0)
    cl = _per_head(cs[T - 1:T, :], heads)
    lmat = jnp.exp(jnp.where(tri[None], csc - csr, -jnp.inf))
    return dict(dtc=_per_head(ct["dt"], heads), lmat=lmat, ecs=jnp.exp(csc), dsc=jnp.exp(cl - csc), cdc=jnp.exp(cl))


def ssd_scan_fwd(xh, bm, cm, dt_raw, dt_bias, a_log, d_skip, name, jobs=()):
    L = xh.shape[1]
    nc = L // CHUNK
    T = CHUNK
    TS = SCAN_CPS * T

    def body(xh_ref, bm_ref, cm_ref, dt_ref, bias_ref, alog_ref, dsk_ref, y_ref, sp_ref, st_ref):
        c = pl.program_id(0)

        @pl.when(c == 0)
        def _():
            st_ref[...] = jnp.zeros_like(st_ref)

        heads = range(SSD_H)
        dh = _per_head(dsk_ref[...], heads)
        s = st_ref[...]
        for k in range(SCAN_CPS):
            rows = slice(k * T, (k + 1) * T)
            ht = _head_terms(_chunk_terms(dt_ref[rows, :], bias_ref[...], alog_ref[...]), heads)
            x = xh_ref[:, rows, :]
            bgb = bm_ref[:, rows, :].astype(BF16)
            cgb = cm_ref[:, rows, :].astype(BF16)
            bh = _rep_heads(bgb)
            ch = _rep_heads(cgb)
            xt = x * ht["dtc"]
            cb = jnp.einsum("gln,gsn->gls", cgb, bgb, preferred_element_type=F32)
            mb = (_rep_heads(cb) * ht["lmat"]).astype(BF16)
            yd = jnp.einsum("hls,hsp->hlp", mb, xt.astype(BF16), preferred_element_type=F32)
            sb = s.astype(BF16)
            yo = jnp.einsum("hln,hpn->hlp", ch, sb, preferred_element_type=F32) * ht["ecs"]
            y_ref[:, rows, :] = yd + yo + x * dh
            sp_ref[k] = sb
            xd = (xt * ht["dsc"]).astype(BF16)
            s = s * ht["cdc"] + jnp.einsum("htp,htn->hpn", xd, bh, preferred_element_type=F32)
        st_ref[...] = s

    hd = pl.BlockSpec((SSD_H, TS, SSD_P), lambda c: (0, c, 0))
    gr = pl.BlockSpec((SSD_G, TS, SSD_N), lambda c: (0, c, 0))
    return _pcall(body, name, (nc // SCAN_CPS,),
                  [hd, gr, gr, pl.BlockSpec((TS, SSD_H), lambda c: (c, 0)),
                   _small((1, SSD_H)), _small((1, SSD_H)), _small((1, SSD_H))],
                  [hd, pl.BlockSpec((SCAN_CPS, SSD_H, SSD_P, SSD_N), lambda c: (c, 0, 0, 0))],
                  [_sds((SSD_H, L, SSD_P)), _sds((nc, SSD_H, SSD_P, SSD_N), BF16)],
                  [pltpu.VMEM((SSD_H, SSD_P, SSD_N), F32)],
                  [xh, bm, cm, dt_raw, dt_bias, a_log, d_skip], jobs)


def ssd_scan_bwd(dy, xh, bm, cm, dt_raw, sprev, dt_bias, a_log, d_skip, name, jobs=()):
    L = xh.shape[1]
    nc = L // CHUNK
    T = CHUNK

    def body(dy_ref, xh_ref, bm_ref, cm_ref, dt_ref, sp_ref, bias_ref, alog_ref, dsk_ref,
             dxh_ref, dbm_ref, dcm_ref, ddt_ref, dbias_ref, dalog_ref, ddsk_ref, g_ref):
        i = pl.program_id(0)

        @pl.when(i == 0)
        def _():
            g_ref[...] = jnp.zeros_like(g_ref)
            dbias_ref[...] = jnp.zeros_like(dbias_ref)
            dalog_ref[...] = jnp.zeros_like(dalog_ref)
            ddsk_ref[...] = jnp.zeros_like(ddsk_ref)

        bias = bias_ref[...]
        heads = range(SSD_H)
        dh = _per_head(dsk_ref[...], heads)
        g = g_ref[...]
        for k in reversed(range(SCAN_CPS)):
            rows = slice(k * T, (k + 1) * T)
            dtr = dt_ref[rows, :]
            ct = _chunk_terms(dtr, bias, alog_ref[...])
            dt, a_head, tri = ct["dt"], ct["a_head"], ct["tri"]
            ht = _head_terms(ct, heads)
            dtc, lmat, ecs, dsc, cdc = ht["dtc"], ht["lmat"], ht["ecs"], ht["dsc"], ht["cdc"]
            x = xh_ref[:, rows, :]
            dyv = dy_ref[:, rows, :]
            dyb = dyv.astype(BF16)
            bgb = bm_ref[:, rows, :].astype(BF16)
            cgb = cm_ref[:, rows, :].astype(BF16)
            bh = _rep_heads(bgb)
            ch = _rep_heads(cgb)
            sb = sp_ref[k]
            gb = g.astype(BF16)
            xt = x * dtc
            xtb = xt.astype(BF16)
            mf = _rep_heads(jnp.einsum("gln,gsn->gls", cgb, bgb, preferred_element_type=F32)) * lmat
            mb = mf.astype(BF16)
            ddsk = jnp.sum(dyv * x, axis=(1, 2), keepdims=True)
            dx = dyv * dh
            yo_raw = jnp.einsum("hln,hpn->hlp", ch, sb, preferred_element_type=F32)
            w1 = dyv * ecs
            w1b = w1.astype(BF16)
            ds_off = jnp.einsum("hlp,hln->hpn", w1b, ch, preferred_element_type=F32)
            dch = jnp.einsum("hlp,hpn->hln", w1b, sb, preferred_element_type=F32)
            dcs_c = jnp.sum(w1 * yo_raw, axis=2, keepdims=True)
            dm = jnp.einsum("hlp,hsp->hls", dyb, xtb, preferred_element_type=F32)
            dxt = jnp.einsum("hls,hlp->hsp", mb, dyb, preferred_element_type=F32)
            dcbb = _sum_heads(dm * lmat).astype(BF16)
            dseg = dm * mf
            dcs_c = dcs_c + jnp.sum(dseg, axis=2, keepdims=True)
            dcs_r = -jnp.sum(dseg, axis=1, keepdims=True)
            dc = jnp.einsum("gls,gsn->gln", dcbb, bgb, preferred_element_type=F32) + _sum_heads(dch)
            db = jnp.einsum("gls,gln->gsn", dcbb, cgb, preferred_element_type=F32)
            xd = xt * dsc
            dxd = jnp.einsum("htn,hpn->htp", bh, gb, preferred_element_type=F32)
            db = db + _sum_heads(jnp.einsum("htp,hpn->htn", xd.astype(BF16), gb, preferred_element_type=F32))
            dxt = dxt + dxd * dsc
            d_ds = jnp.sum(dxd * xt, axis=2, keepdims=True)
            d_cd = jnp.sum(g * sb.astype(F32), axis=(1, 2), keepdims=True)
            g = g * cdc + ds_off
            t1 = d_ds * dsc
            dcs_c = dcs_c - t1
            dcl = jnp.sum(t1, axis=1, keepdims=True) + d_cd * cdc
            ddt_c = jnp.sum(dxt * x, axis=2, keepdims=True)
            dxh_ref[:, rows, :] = dx + dxt * dtc
            dbm_ref[:, rows, :] = db
            dcm_ref[:, rows, :] = dc
            lanes = lambda v: _heads_to_lanes([v[h] for h in heads])
            rows_t = jnp.concatenate([dcs_r[h] for h in heads], axis=0).T
            last = (lax.broadcasted_iota(jnp.int32, (T, 1), 0) == T - 1).astype(F32)
            dcs = lanes(dcs_c) + rows_t + last * lanes(dcl)
            da = lax.dot_general(tri.astype(F32), dcs, (((0,), (0,)), ((), ())),
                                 precision=lax.Precision.HIGHEST, preferred_element_type=F32)
            ddt = da * a_head + lanes(ddt_c)
            dalog_ref[...] += jnp.sum(da * dt, axis=0, keepdims=True)
            ddtr = ddt * jax.nn.sigmoid(dtr + bias)
            ddt_ref[rows, :] = ddtr
            dbias_ref[...] += jnp.sum(ddtr, axis=0, keepdims=True)
            ddsk_ref[...] += lanes(ddsk)
        g_ref[...] = g

        @pl.when(i == nb - 1)
        def _():
            dalog_ref[...] = dalog_ref[...] * (-jnp.exp(alog_ref[...]))

    nb = nc // SCAN_CPS
    TS = SCAN_CPS * T
    hd = pl.BlockSpec((SSD_H, TS, SSD_P), lambda i: (0, nb - 1 - i, 0))
    gr = pl.BlockSpec((SSD_G, TS, SSD_N), lambda i: (0, nb - 1 - i, 0))
    tk = pl.BlockSpec((TS, SSD_H), lambda i: (nb - 1 - i, 0))
    return _pcall(body, name, (nb,),
                  [hd, hd, gr, gr, tk, pl.BlockSpec((SCAN_CPS, SSD_H, SSD_P, SSD_N), lambda i: (nb - 1 - i, 0, 0, 0)),
                   _small((1, SSD_H)), _small((1, SSD_H)), _small((1, SSD_H))],
                  [hd, gr, gr, tk, _small((1, SSD_H)), _small((1, SSD_H)), _small((1, SSD_H))],
                  [_sds((SSD_H, L, SSD_P)), _sds((SSD_G, L, SSD_N)), _sds((SSD_G, L, SSD_N)), _sds((L, SSD_H)),
                   _sds((1, SSD_H)), _sds((1, SSD_H)), _sds((1, SSD_H))],
                  [pltpu.VMEM((SSD_H, SSD_P, SSD_N), F32)],
                  [dy, xh, bm, cm, dt_raw, sprev, dt_bias, a_log, d_skip], jobs)


def _heads_to_tokens(y_ref):
    return jnp.concatenate([y_ref[h] for h in range(SSD_H)], axis=1)


def ssd_out_fwd(x, y, z, norm_w, w_out, g_post, name, jobs=()):
    L = x.shape[0]
    nt = L // TL

    def body(x_ref, y_ref, z_ref, nw_ref, wout_ref, gpost_ref, m_ref, xn_ref):
        z = z_ref[...]
        yg = _heads_to_tokens(y_ref) * (z * jax.nn.sigmoid(z))
        yn = _rms(yg, nw_ref[...]).astype(BF16)
        m = _mm(yn, wout_ref[...])
        m_ref[...] = m
        xn_ref[...] = x_ref[...] + _rms(m, gpost_ref[...])

    return _pcall(body, name, (nt,),
                  [_tile(D_MODEL), pl.BlockSpec((SSD_H, TL, SSD_P), lambda i: (0, i, 0)), _tile(SSD_DI),
                   _small((1, SSD_DI)), _res((SSD_DI, D_MODEL)), _small((1, D_MODEL))],
                  [_tile(D_MODEL), _tile(D_MODEL)],
                  [_sds((L, D_MODEL)), _sds((L, D_MODEL))],
                  [],
                  [x, y, z, norm_w, w_out, g_post], jobs)


def ssd_out_bwd(dxo, m, y, z, norm_w, w_out, g_post, name, jobs=()):
    L = dxo.shape[0]
    nt = L // TL
    rows = SSD_DI // N_DEV

    def body(dxo_ref, m_ref, y_ref, z_ref, nw_ref, wout_ref, gpost_ref,
             dy_ref, dz_ref, dwout_ref, dgp_ref, dnw_ref, acc_ref, stage_ref):
        i = pl.program_id(0)

        @pl.when(i == 0)
        def _():
            acc_ref[...] = jnp.zeros_like(acc_ref)
            dgp_ref[...] = jnp.zeros_like(dgp_ref)
            dnw_ref[...] = jnp.zeros_like(dnw_ref)

        dm, dgp = _rms_bwd(m_ref[...], gpost_ref[...], dxo_ref[...])
        dgp_ref[...] += dgp
        dmb = dm.astype(BF16)
        dyn = _mm_nt(dmb, wout_ref[...])
        z = z_ref[...]
        y = _heads_to_tokens(y_ref)
        sil, dsil = _silu_parts(z)
        yg = y * sil
        nw = nw_ref[...]
        acc_ref[...] += _mm_tn(_rms(yg, nw).astype(BF16), dmb)
        dyg, dnw = _rms_bwd(yg, nw, dyn)
        dnw_ref[...] += dnw
        dyv = dyg * sil
        dz_ref[...] = dyg * y * dsil
        for h in range(SSD_H):
            dy_ref[h] = dyv[:, h * SSD_P:(h + 1) * SSD_P]

        @pl.when(i == nt - 1)
        def _():
            _emit_row_shards(acc_ref, dwout_ref, stage_ref)

    hd = pl.BlockSpec((SSD_H, TL, SSD_P), lambda i: (0, i, 0))
    return _pcall(body, name, (nt,),
                  [_tile(D_MODEL), _tile(D_MODEL), hd, _tile(SSD_DI), _small((1, SSD_DI)), _res((SSD_DI, D_MODEL)),
                   _small((1, D_MODEL))],
                  [hd, _tile(SSD_DI), _ANY, _small((1, D_MODEL)), _small((1, SSD_DI))],
                  [_sds((SSD_H, L, SSD_P)), _sds((L, SSD_DI)), _sds((N_DEV, rows, D_MODEL), BF16),
                   _sds((1, D_MODEL)), _sds((1, SSD_DI))],
                  [pltpu.VMEM((SSD_DI, D_MODEL), F32), pltpu.VMEM((rows, D_MODEL), BF16)],
                  [dxo, m, y, z, norm_w, w_out, g_post], jobs)


def ssd_conv_bwd(dxh, dbm, dcm, xbc_raw, dz, ddt_raw, conv_w, conv_b, name, jobs=()):
    L = xbc_raw.shape[0]
    nt = L // TL

    def body(dxh_ref, dbm_ref, dcm_ref, raw_ref, halo_ref, dz_ref, ddt_ref, cw_ref, cb_ref,
             d_ref, dcw_ref, dcb_ref, carry_ref):
        i = pl.program_id(0)
        t = nt - 1 - i

        @pl.when(i == 0)
        def _():
            carry_ref[...] = jnp.zeros_like(carry_ref)
            dcw_ref[...] = jnp.zeros_like(dcw_ref)
            dcb_ref[...] = jnp.zeros_like(dcb_ref)

        def dact_chunk(rows, c):
            lo = c * CONV_LC
            if lo < SSD_DI:
                per, ref, first = SSD_P, dxh_ref, lo // SSD_P
            elif lo < SSD_DI + SSD_G * SSD_N:
                per, ref, first = SSD_N, dbm_ref, (lo - SSD_DI) // SSD_N
            else:
                per, ref, first = SSD_N, dcm_ref, (lo - SSD_DI - SSD_G * SSD_N) // SSD_N
            return jnp.concatenate([ref[first + q, rows, :] for q in range(CONV_LC // per)], axis=1)

        for c in range(SSD_CONV // CONV_LC):
            cols = slice(c * CONV_LC, (c + 1) * CONV_LC)
            w = cw_ref[:, cols]
            b = cb_ref[:, cols]
            halo = jnp.where(t == 0, 0.0, halo_ref[:, cols])
            carry = carry_ref[:, cols]
            dcb = jnp.zeros((1, CONV_LC), F32)
            dcw = jnp.zeros((SSD_KW, CONV_LC), F32)
            for r in reversed(range(TL // CONV_RC)):
                r0 = r * CONV_RC
                rows = slice(r0, r0 + CONV_RC)
                if r0 == 0:
                    ext = jnp.concatenate([halo, raw_ref[rows, cols]], axis=0)
                else:
                    ext = raw_ref[r0 - HALO:r0 + CONV_RC, cols]
                taps = _taps(ext, SSD_KW, CONV_RC)
                pre = _conv_fwd(taps, w) + b
                _, dsil = _silu_parts(pre)
                dpre = dact_chunk(rows, c) * dsil
                dcb = dcb + jnp.sum(dpre, axis=0, keepdims=True)
                dcw = dcw + _conv_bwd_w(taps, dpre)
                extd = jnp.concatenate([dpre, carry], axis=0)
                draw = _conv_bwd_in(extd, w, SSD_KW, CONV_RC)
                carry = dpre[:HALO, :]
                d_ref[rows, SSD_DI + c * CONV_LC:SSD_DI + (c + 1) * CONV_LC] = draw.astype(BF16)
            carry_ref[:, cols] = carry
            dcb_ref[:, cols] += dcb
            dcw_ref[:, cols] += dcw
        d_ref[:, :SSD_DI] = dz_ref[...].astype(BF16)
        tail = jnp.concatenate([ddt_ref[...], jnp.zeros((TL, SSD_IN_PAD - SSD_IN), F32)], axis=1)
        d_ref[:, SSD_DI + SSD_CONV:] = tail.astype(BF16)

    hd = pl.BlockSpec((SSD_H, TL, SSD_P), lambda i: (0, nt - 1 - i, 0))
    gr = pl.BlockSpec((SSD_G, TL, SSD_N), lambda i: (0, nt - 1 - i, 0))
    return _pcall(body, name, (nt,),
                  [hd, gr, gr, _rtile(SSD_CONV, nt), _halo_before(SSD_CONV, nt, True), _rtile(SSD_DI, nt),
                   _rtile(SSD_H, nt), _small((SSD_KW, SSD_CONV)), _small((1, SSD_CONV))],
                  [_rtile(SSD_IN_PAD, nt), _small((SSD_KW, SSD_CONV)), _small((1, SSD_CONV))],
                  [_sds((L, SSD_IN_PAD), BF16), _sds((SSD_KW, SSD_CONV)), _sds((1, SSD_CONV))],
                  [pltpu.VMEM((HALO, SSD_CONV), F32)],
                  [dxh, dbm, dcm, xbc_raw, xbc_raw, dz, ddt_raw, conv_w, conv_b], jobs)


def loss_fwd_bwd(y, target, name, jobs=()):
    L = y.shape[0]
    nt = L // TL

    def body(y_ref, t_ref, loss_ref, dy_ref):
        i = pl.program_id(0)

        @pl.when(i == 0)
        def _():
            loss_ref[...] = jnp.zeros_like(loss_ref)

        err = y_ref[...] - t_ref[...]
        dy_ref[...] = err * (1.0 / D_MODEL)
        loss_ref[...] += 0.5 * jnp.sum(jnp.mean(err * err, axis=-1, keepdims=True), axis=0, keepdims=True)

    return _pcall(body, name, (nt,),
                  [_tile(D_MODEL), _tile(D_MODEL)],
                  [_small((1, 1)), _tile(D_MODEL)],
                  [_sds((1, 1)), _sds((L, D_MODEL))],
                  [],
                  [y, target], jobs)


def _adamw_math(w, g, m, v):
    m = ADAM_B1 * m + (1.0 - ADAM_B1) * g
    v = ADAM_B2 * v + (1.0 - ADAM_B2) * (g * g)
    m_hat = m / (1.0 - ADAM_B1 ** ADAM_STEP)
    v_hat = v / (1.0 - ADAM_B2 ** ADAM_STEP)
    delta = -ADAM_LR * (m_hat / (jnp.sqrt(v_hat) + ADAM_EPS) + ADAM_WD * w)
    return delta, m, v


def _row_tile(rows):
    for cand in (256, 176, 128, 64, 32, 16, 8):
        if rows % cand == 0:
            return cand
    return rows


def reduce_adamw(recvs, w, m, v, name, jobs=()):
    nl = len(recvs)
    _, R, C = recvs[0].shape
    tr = _row_tile(R)
    nr = R // tr

    def body(*refs):
        r_refs = refs[:nl]
        w_ref, m_ref, v_ref, g_out, d_out, m_out, v_out = refs[nl:]
        layer = pl.program_id(0) // nr
        for ll in range(nl):
            @pl.when(layer == ll)
            def _(ll=ll):
                g = r_refs[ll][0].astype(F32)
                for j in range(1, N_DEV):
                    g = g + r_refs[ll][j].astype(F32)
                delta, mn, vn = _adamw_math(w_ref[0], g, m_ref[0], v_ref[0])
                g_out[0] = g
                d_out[0] = delta
                m_out[0] = mn
                v_out[0] = vn

    def recv_spec(ll):
        return pl.BlockSpec((N_DEV, tr, C), lambda i: (0, jnp.where(i // nr == ll, i % nr, 0), 0))

    blk = pl.BlockSpec((1, tr, C), lambda i: (i // nr, i % nr, 0))
    return _pcall(body, name, (nl * nr,),
                  [recv_spec(ll) for ll in range(nl)] + [blk, blk, blk],
                  [blk] * 4,
                  [_sds((nl, R, C))] * 4,
                  [],
                  [*recvs, w, m, v], jobs)


def small_reduce(gathered, name):
    _, R, C = gathered.shape

    def body(r_ref, o_ref):
        g = r_ref[0]
        for j in range(1, N_DEV):
            g = g + r_ref[j]
        o_ref[...] = g

    return pl.pallas_call(body, name=name, out_shape=_sds((R, C)))(gathered)


def small_adamw(g, w, m, v, name):
    def body(g_ref, w_ref, m_ref, v_ref, d_out, m_out, v_out):
        delta, mn, vn = _adamw_math(w_ref[...], g_ref[...], m_ref[...], v_ref[...])
        d_out[...] = delta
        m_out[...] = mn
        v_out[...] = vn

    return pl.pallas_call(body, name=name, out_shape=[_sds(g.shape)] * 3)(g, w, m, v)


def _pack(arrs):
    flat = jnp.concatenate([a.reshape(-1) for a in arrs])
    n = flat.shape[0]
    rows = -(-n // (8 * LANES)) * 8
    flat = jnp.pad(flat, (0, rows * LANES - n))
    return flat.reshape(rows, LANES)


def _unpack(packed, shapes):
    flat = packed.reshape(-1)
    out = []
    off = 0
    for s in shapes:
        n = 1
        for d in s:
            n *= d
        out.append(flat[off:off + n].reshape(s))
        off += n
    return out


def kernel(x, mix_pre_g, mix_post_g, ffn_pre_g, ffn_post_g, ssd_w_in, ssd_conv_w, ssd_conv_b, ssd_dt_bias, ssd_A_log, ssd_D, ssd_norm_w, ssd_w_out, sc_w_in, sc_conv_w, sc_w_out, ffn_w_up, ffn_conv_w, ffn_conv_b, ffn_w_down, loss_target, m_mix_pre_g, m_mix_post_g, m_ffn_pre_g, m_ffn_post_g, m_ssd_w_in, m_ssd_conv_w, m_ssd_conv_b, m_ssd_dt_bias, m_ssd_A_log, m_ssd_D, m_ssd_norm_w, m_ssd_w_out, m_sc_w_in, m_sc_conv_w, m_sc_w_out, m_ffn_w_up, m_ffn_conv_w, m_ffn_conv_b, m_ffn_w_down, v_mix_pre_g, v_mix_post_g, v_ffn_pre_g, v_ffn_post_g, v_ssd_w_in, v_ssd_conv_w, v_ssd_conv_b, v_ssd_dt_bias, v_ssd_A_log, v_ssd_D, v_ssd_norm_w, v_ssd_w_out, v_sc_w_in, v_sc_conv_w, v_sc_w_out, v_ffn_w_up, v_ffn_conv_w, v_ffn_conv_b, v_ffn_w_down):
    me = _my_index()
    x0 = x[0]
    target = loss_target[0]
    row = lambda a: a.reshape(1, -1)

    tr = lambda a: jnp.transpose(a, (0, 2, 1))
    shards = {"ssd_in": ssd_w_in, "ssd_out": ssd_w_out, "sc_in": sc_w_in, "sc_out": sc_w_out,
              "up": tr(ffn_w_up), "down": ffn_w_down}
    col_sharded = {"ssd_in": SSD_IN_PAD, "sc_in": 3 * D_MODEL}
    weights = {}

    def shard_bf16(key):
        n, l = key
        return shards[n][l].astype(BF16)

    def store_weights(keys, outs):
        for (n, l), g in zip(keys, outs):
            _, R, C = g.shape
            if n in col_sharded:
                full = jnp.transpose(g, (1, 0, 2)).reshape(R, N_DEV * C)
                full = jnp.pad(full, ((0, 0), (0, col_sharded[n] - N_DEV * C)))
            else:
                full = g.reshape(N_DEV * R, C)
            weights[(n, l)] = full

    fwd_first_half = {
        "ssd_inproj_0": [("ssd_out", 0), ("up", 0), ("down", 0)],
        "ssd_scan_fwd_0": [("sc_in", 0), ("sc_out", 0), ("up", 1)],
        "ssd_out_fwd_0": [("down", 1)],
        "ffn_fwd_0": [("ssd_in", 1)],
        "sc_fwd_0": [("ssd_out", 1), ("down", 2)],
        "ffn_fwd_1": [("up", 2)],
        "ssd_inproj_1": [("sc_in", 1), ("sc_out", 1), ("down", 3)],
        "ssd_scan_fwd_1": [("up", 3)],
    }
    bwd_sched = {
        "ffn_bwd2_3": [("down", 3)], "sc_bwd2_1": [("sc_out", 1)], "ffn_bwd1_2": [("up", 3)], "ffn_bwd2_2": [("sc_in", 1)],
        "ssd_out_bwd_1": [("down", 2)], "ssd_scan_bwd_1": [("up", 2), ("ssd_out", 1)], "ffn_bwd1_1": [("ssd_in", 1)],
        "ffn_bwd2_1": [("down", 1)], "sc_bwd2_0": [("sc_out", 0)], "ffn_bwd1_0": [("up", 1)], "ffn_bwd2_0": [("sc_in", 0)],
        "ssd_out_bwd_0": [("down", 0)], "ssd_scan_bwd_0": [("up", 0), ("ssd_out", 0)],
    }

    first = [("ssd_in", 0)]
    outs = exchange([("ag2", shard_bf16(k)) for k in first]
                    + [("ag", ssd_conv_w), ("ag", sc_conv_w), ("ag", ffn_conv_w)], "ag_first")
    store_weights(first, outs[:1])

    def taps(g):
        _, nl, K, C = g.shape
        return jnp.transpose(g, (1, 2, 0, 3)).reshape(nl, K, N_DEV * C)

    CW_ssd, CW_sc, CW_ffn = taps(outs[1]), taps(outs[2]), taps(outs[3])

    half_done = []

    def fwd(fn, name, *args):
        second = list(half_done)
        starting = fwd_first_half.get(name, [])
        jobs = [("agB", buf) for _, buf in second] + [("agA", shard_bf16(k)) for k in starting]
        res, got = fn(*args, name, jobs)
        store_weights([k for k, _ in second], got[:len(second)])
        half_done[:] = list(zip(starting, got[len(second):]))
        return res

    saved = []
    h = x0
    for i in range(DEPTH):
        j = i // 2
        blk = dict(x_mix=h)
        if i % 2 == 0:
            z, raw, dt_raw, xh, bm, cm = fwd(ssd_inproj, f"ssd_inproj_{j}", h, row(mix_pre_g[i]), weights[("ssd_in", j)],
                                             CW_ssd[j], row(ssd_conv_b[j]))
            y, sprev = fwd(ssd_scan_fwd, f"ssd_scan_fwd_{j}", xh, bm, cm, dt_raw, row(ssd_dt_bias[j]),
                           row(ssd_A_log[j]), row(ssd_D[j]))
            m, h = fwd(ssd_out_fwd, f"ssd_out_fwd_{j}", h, y, z, row(ssd_norm_w[j]), weights[("ssd_out", j)],
                       row(mix_post_g[i]))
            blk.update(z=z, raw=raw, dt_raw=dt_raw, xh=xh, bm=bm, cm=cm, y=y, sprev=sprev, m=m)
        else:
            bcv, m, h = fwd(sc_fwd, f"sc_fwd_{j}", h, row(mix_pre_g[i]), weights[("sc_in", j)], CW_sc[j],
                            weights[("sc_out", j)], row(mix_post_g[i]))
            blk.update(bcv=bcv, m=m)
        blk["x_ffn"] = h
        up, f, h = fwd(ffn_fwd, f"ffn_fwd_{i}", h, row(ffn_pre_g[i]), weights[("up", i)], CW_ffn[i],
                       row(ffn_conv_b[i]), weights[("down", i)], row(ffn_post_g[i]))
        blk.update(up=up, f=f)
        saved.append(blk)

    (loss_dev, dh), _ = loss_fwd_bwd(h, target, "loss")
    loss = lax.psum(loss_dev[0, 0], ("x", "y", "c"))

    parts, recvd = {}, {}

    def bwd(fn, name, *args):
        keys = bwd_sched.get(name, [])
        res, got = fn(*args, name, [("a2a", parts[k]) for k in keys])
        for k, g in zip(keys, got):
            recvd[k] = g
        return res

    g_mix_pre, g_mix_post, g_ffn_pre, g_ffn_post = [None] * DEPTH, [None] * DEPTH, [None] * DEPTH, [None] * DEPTH
    g_ffn_cw, g_ffn_cb = [None] * DEPTH, [None] * DEPTH
    g_ssd_cw, g_ssd_cb, g_ssd_dtb, g_ssd_alog, g_ssd_d, g_ssd_nw = ([None] * 2 for _ in range(6))
    g_sc_cw = [None] * 2
    for i in reversed(range(DEPTH)):
        j = i // 2
        blk = saved[i]
        dup, parts[("down", i)], g_ffn_post[i], g_ffn_cw[i], g_ffn_cb[i] = bwd(
            ffn_bwd1, f"ffn_bwd1_{i}", dh, blk["f"], blk["up"], row(ffn_post_g[i]), weights[("down", i)], CW_ffn[i],
            row(ffn_conv_b[i]))
        dh, parts[("up", i)], g_ffn_pre[i] = bwd(functools.partial(inproj_bwd, transposed=True), f"ffn_bwd2_{i}",
                                                  blk["x_ffn"], row(ffn_pre_g[i]), dup, weights[("up", i)], dh,
                                                  2 * FFN_F // N_DEV)
        if i % 2 == 0:
            dy, dz, parts[("ssd_out", j)], g_mix_post[i], g_ssd_nw[j] = bwd(
                ssd_out_bwd, f"ssd_out_bwd_{j}", dh, blk["m"], blk["y"], blk["z"], row(ssd_norm_w[j]),
                weights[("ssd_out", j)], row(mix_post_g[i]))
            dxh, dbm, dcm, ddt, g_ssd_dtb[j], g_ssd_alog[j], g_ssd_d[j] = bwd(
                ssd_scan_bwd, f"ssd_scan_bwd_{j}", dy, blk["xh"], blk["bm"], blk["cm"], blk["dt_raw"], blk["sprev"],
                row(ssd_dt_bias[j]), row(ssd_A_log[j]), row(ssd_D[j]))
            d_in, g_ssd_cw[j], g_ssd_cb[j] = bwd(ssd_conv_bwd, f"ssd_conv_bwd_{j}", dxh, dbm, dcm, blk["raw"], dz, ddt,
                                                 CW_ssd[j], row(ssd_conv_b[j]))
            dh, parts[("ssd_in", j)], g_mix_pre[i] = bwd(inproj_bwd, f"ssd_bwd2_{j}", blk["x_mix"], row(mix_pre_g[i]),
                                                          d_in, weights[("ssd_in", j)], dh, SSD_IN // N_DEV)
        else:
            dbcv, parts[("sc_out", j)], g_mix_post[i], g_sc_cw[j] = bwd(
                sc_bwd1, f"sc_bwd1_{j}", dh, blk["m"], blk["bcv"], row(mix_post_g[i]), weights[("sc_out", j)], CW_sc[j])
            dh, parts[("sc_in", j)], g_mix_pre[i] = bwd(inproj_bwd, f"sc_bwd2_{j}", blk["x_mix"], row(mix_pre_g[i]),
                                                         dbcv, weights[("sc_in", j)], dh, 3 * D_MODEL // N_DEV)
    grad_x = dh[None]

    st = lambda lst: jnp.concatenate(lst, axis=0)
    small_full = [
        st(g_mix_pre), st(g_mix_post), st(g_ffn_pre), st(g_ffn_post),
        jnp.stack(g_ssd_cw), st(g_ssd_cb), st(g_ssd_dtb), st(g_ssd_alog), st(g_ssd_d), st(g_ssd_nw),
        jnp.stack(g_sc_cw), jnp.stack(g_ffn_cw), st(g_ffn_cb),
    ]
    full_shapes = [a.shape for a in small_full]

    recvd[("ssd_in", 0)], small_gathered = exchange([("a2a", parts[("ssd_in", 0)]), ("ag", _pack(small_full))],
                                                    "a2a_last")

    def finish(n, nl, w, m, v):
        return reduce_adamw([recvd[(n, l)] for l in range(nl)], w, m, v, "adamw_" + n)[0]

    r_up = [tr(a) for a in finish("up", DEPTH, tr(ffn_w_up), tr(m_ffn_w_up), tr(v_ffn_w_up))]
    r_down = finish("down", DEPTH, ffn_w_down, m_ffn_w_down, v_ffn_w_down)
    r_ssd_out = finish("ssd_out", 2, ssd_w_out, m_ssd_w_out, v_ssd_w_out)
    r_sc_in = finish("sc_in", 2, sc_w_in, m_sc_w_in, v_sc_w_in)
    r_sc_out = finish("sc_out", 2, sc_w_out, m_sc_w_out, v_sc_w_out)
    r_ssd_in = finish("ssd_in", 2, ssd_w_in, m_ssd_w_in, v_ssd_w_in)

    summed = small_reduce(small_gathered, "small_reduce")
    (s_mix_pre, s_mix_post, s_ffn_pre, s_ffn_post, s_ssd_cw, s_ssd_cb, s_ssd_dtb, s_ssd_alog, s_ssd_d, s_ssd_nw,
     s_sc_cw, s_ffn_cw, s_ffn_cb) = _unpack(summed, full_shapes)

    def my_cols(a, width):
        return lax.dynamic_slice_in_dim(a, me * width, width, axis=a.ndim - 1)

    s_ssd_cw = my_cols(s_ssd_cw, SSD_CONV // N_DEV)
    s_sc_cw = my_cols(s_sc_cw, D_MODEL // N_DEV)
    s_ffn_cw = my_cols(s_ffn_cw, FFN_F // N_DEV)

    small_g = [s_mix_pre, s_mix_post, s_ffn_pre, s_ffn_post, s_ssd_cw, s_ssd_cb, s_ssd_dtb, s_ssd_alog, s_ssd_d,
               s_ssd_nw, s_sc_cw, s_ffn_cw, s_ffn_cb]
    small_w = [mix_pre_g, mix_post_g, ffn_pre_g, ffn_post_g, ssd_conv_w, ssd_conv_b, ssd_dt_bias, ssd_A_log, ssd_D,
               ssd_norm_w, sc_conv_w, ffn_conv_w, ffn_conv_b]
    small_m = [m_mix_pre_g, m_mix_post_g, m_ffn_pre_g, m_ffn_post_g, m_ssd_conv_w, m_ssd_conv_b, m_ssd_dt_bias,
               m_ssd_A_log, m_ssd_D, m_ssd_norm_w, m_sc_conv_w, m_ffn_conv_w, m_ffn_conv_b]
    small_v = [v_mix_pre_g, v_mix_post_g, v_ffn_pre_g, v_ffn_post_g, v_ssd_conv_w, v_ssd_conv_b, v_ssd_dt_bias,
               v_ssd_A_log, v_ssd_D, v_ssd_norm_w, v_sc_conv_w, v_ffn_conv_w, v_ffn_conv_b]
    local_shapes = [a.shape for a in small_w]
    pd, pm, pv = small_adamw(_pack(small_g), _pack(small_w), _pack(small_m), _pack(small_v), "small_adamw")
    sd = _unpack(pd, local_shapes)
    sm = _unpack(pm, local_shapes)
    sv = _unpack(pv, local_shapes)

    def ordered(small, big):
        (mix_pre, mix_post, ffn_pre, ffn_post, ssd_cw, ssd_cb, dtb, alog, dsk, nw, sc_cw, ffn_cw, ffn_cb) = small
        (b_ssd_in, b_ssd_out, b_sc_in, b_sc_out, b_up, b_down) = big
        return [mix_pre, mix_post, ffn_pre, ffn_post, b_ssd_in, ssd_cw, ssd_cb, dtb, alog, dsk, nw, b_ssd_out,
                b_sc_in, sc_cw, b_sc_out, b_up, ffn_cw, ffn_cb, b_down]

    bigs = [r_ssd_in, r_ssd_out, r_sc_in, r_sc_out, r_up, r_down]
    grads = ordered(small_g, [r[0] for r in bigs])
    deltas = ordered(sd, [r[1] for r in bigs])
    new_m = ordered(sm, [r[2] for r in bigs])
    new_v = ordered(sv, [r[3] for r in bigs])
    return (loss, grad_x, *grads, *deltas, *new_m, *new_v)
```

```python
import functools

import jax
import jax.numpy as jnp
from jax import lax
from jax.experimental import pallas as pl
from jax.experimental.pallas import tpu as pltpu

F32 = jnp.float32
BF16 = jnp.bfloat16

EPS = 1e-6
D_MODEL = 1024
DEPTH = 4
N_DEV = 8
CHUNK = 64
SSD_DI = 2048
SSD_H = 32
SSD_P = 64
SSD_G = 8
SSD_R = SSD_H // SSD_G
SSD_N = 128
SSD_CONV = SSD_DI + 2 * SSD_G * SSD_N
SSD_IN = SSD_DI + SSD_CONV + SSD_H
LANES = 128
SSD_IN_PAD = -(-SSD_IN // LANES) * LANES
SSD_KW = 4
SC_KW = 3
FFN_F = 2816
FFN_KW = 3
SCAN_CPS = 2
TL = 256
HALO = 8
CONV_RC = 128
CONV_LC = 256
VMEM_LIMIT = 60 * 1024 * 1024

ADAM_LR = 0.001
ADAM_B1 = 0.9
ADAM_B2 = 0.999
ADAM_EPS = 1e-08
ADAM_WD = 0.01
ADAM_STEP = 10

MESH = pl.DeviceIdType.MESH


def _rms(x, g):
    r = lax.rsqrt(jnp.mean(x * x, axis=-1, keepdims=True) + EPS)
    return x * r * g


def _rms_bwd(x, g, dy):
    r = lax.rsqrt(jnp.mean(x * x, axis=-1, keepdims=True) + EPS)
    xh = x * r
    dg = jnp.sum(dy * xh, axis=0, keepdims=True)
    dxh = dy * g
    dx = r * (dxh - xh * jnp.mean(dxh * xh, axis=-1, keepdims=True))
    return dx, dg


def _mm(a, b):
    return jnp.dot(a, b, preferred_element_type=F32)


def _mm_nt(a, b):
    return lax.dot_general(a, b, (((1,), (1,)), ((), ())), preferred_element_type=F32)


def _mm_tn(a, b):
    return lax.dot_general(a, b, (((0,), (0,)), ((), ())), preferred_element_type=F32)


def _silu_parts(x):
    sg = jax.nn.sigmoid(x)
    return x * sg, sg * (1.0 + x * (1.0 - sg))


def _rows_from(a, s, tl):
    if s % HALO == 0:
        return a[s:s + tl]
    return pltpu.roll(a, a.shape[0] - s, 0)[0:tl]


def _taps(ext, kw, tl):
    base = HALO - (kw - 1)
    return [_rows_from(ext, base + j, tl) for j in range(kw)]


def _conv_fwd(taps, w):
    out = taps[0] * w[0:1]
    for j in range(1, len(taps)):
        out = out + taps[j] * w[j:j + 1]
    return out


def _conv_bwd_in(extd, w, kw, tl):
    out = _rows_from(extd, kw - 1, tl) * w[0:1]
    for j in range(1, kw):
        out = out + _rows_from(extd, kw - 1 - j, tl) * w[j:j + 1]
    return out


def _conv_bwd_w(taps, dy):
    return jnp.concatenate([jnp.sum(dy * t, axis=0, keepdims=True) for t in taps], axis=0)


def _emit_row_shards(acc_ref, out_ref, stage_ref):
    rows = out_ref.shape[1]
    for k in range(N_DEV):
        stage_ref[...] = acc_ref[k * rows:(k + 1) * rows, :].astype(BF16)
        pltpu.sync_copy(stage_ref, out_ref.at[k])


def _emit_col_shards(acc_ref, out_ref, stage_ref):
    cols = out_ref.shape[2]
    for k in range(N_DEV):
        stage_ref[...] = acc_ref[:, k * cols:(k + 1) * cols].astype(BF16)
        pltpu.sync_copy(stage_ref, out_ref.at[k])


def _res(shape):
    nd = len(shape)
    return pl.BlockSpec(shape, lambda i: (0,) * nd, pipeline_mode=pl.Buffered(1))


def _small(shape):
    nd = len(shape)
    return pl.BlockSpec(shape, lambda i: (0,) * nd)


def _tile(n):
    return pl.BlockSpec((TL, n), lambda i: (i, 0))


def _rtile(n, nt):
    return pl.BlockSpec((TL, n), lambda i: (nt - 1 - i, 0))


def _halo_before(n, nt, reverse):
    per = TL // HALO
    if reverse:
        return pl.BlockSpec((HALO, n), lambda i: (jnp.maximum((nt - 1 - i) * per - 1, 0), 0))
    return pl.BlockSpec((HALO, n), lambda i: (jnp.maximum(i * per - 1, 0), 0))


_ANY = pl.BlockSpec(memory_space=pl.ANY)


def _sds(shape, dtype=F32):
    return jax.ShapeDtypeStruct(shape, dtype)


def _peer(k):
    x, y, c = lax.axis_index("x"), lax.axis_index("y"), lax.axis_index("c")
    px = x ^ (k >> 2)
    py = y ^ ((k >> 1) & 1)
    pc = c ^ (k & 1)
    return (px, py, pc), 4 * px + 2 * py + pc


def _my_index():
    return 4 * lax.axis_index("x") + 2 * lax.axis_index("y") + lax.axis_index("c")


SIBLING = 1
SAME_CORE_CHIPS = (2, 4, 6)


def _job_copies(kind, src_ref, out_ref, send_sems, recv_sems, local_sems, j):
    me = _my_index()
    sends, recvs = [], []

    def pair(pattern, sem, src, put_slot, get_slot):
        dev, _ = _peer(pattern)
        sems = dict(send_sem=send_sems.at[j, sem], recv_sem=recv_sems.at[j, sem], device_id=dev, device_id_type=MESH)
        sends.append(pltpu.make_async_remote_copy(src_ref=src, dst_ref=out_ref.at[put_slot], **sems))
        recvs.append(pltpu.make_async_remote_copy(src_ref=src, dst_ref=out_ref.at[get_slot], **sems))

    if kind == "agB":
        for k in SAME_CORE_CHIPS:
            _, mine_from_k = _peer(k)
            _, sib_from_k = _peer(k | SIBLING)
            pair(SIBLING, k, out_ref.at[mine_from_k], mine_from_k, sib_from_k)
        return None, sends, recvs
    patterns = (SIBLING,) + SAME_CORE_CHIPS if kind == "agA" else range(1, N_DEV)
    mine = src_ref.at[me] if kind == "a2a" else src_ref
    local = pltpu.make_async_copy(mine, out_ref.at[me], local_sems.at[j])
    for k in patterns:
        _, idx = _peer(k)
        pair(k, k - 1, src_ref.at[idx] if kind == "a2a" else src_ref, me, idx)
    return local, sends, recvs


def _pcall(body, name, grid, in_specs, out_specs, out_shape, scratch_shapes, args, jobs=()):
    n_in, n_out, nj = len(in_specs), len(out_specs), len(jobs)
    last = grid[0] - 1
    kinds = [k for k, _ in jobs]

    def wrapped(*refs):
        ins = refs[:n_in]
        csrc = refs[n_in:n_in + nj]
        outs = refs[n_in + nj:n_in + nj + n_out]
        cout = refs[n_in + nj + n_out:n_in + 2 * nj + n_out]
        rest = refs[n_in + 2 * nj + n_out:]

        def copies(j, kind):
            return _job_copies(kind, csrc[j], cout[j], send_sems, recv_sems, local_sems, j)

        def start(j, kind):
            local, sends, _ = copies(j, kind)
            if local is not None:
                local.start()
            for cp in sends:
                cp.start()

        def finish(j, kind, arrivals_only=False):
            local, sends, recvs = copies(j, kind)
            for cp in recvs:
                cp.wait_recv()
            if not arrivals_only:
                for cp in sends:
                    cp.wait_send()
                if local is not None:
                    local.wait()

        if nj:
            scratch, (send_sems, recv_sems, local_sems) = rest[:-3], rest[-3:]
            i = pl.program_id(0)

            @pl.when(i == 0)
            def _():
                for j in range(nj):
                    start(j, "agA" if kinds[j] == "ag2" else kinds[j])
                for j in range(nj):
                    if kinds[j] == "ag2":
                        finish(j, "agA", arrivals_only=True)
                        start(j, "agB")
        else:
            scratch = rest
        body(*ins, *outs, *scratch)
        if nj:
            @pl.when(i == last)
            def _():
                for j in range(nj):
                    if kinds[j] == "ag2":
                        finish(j, "agB")
                        _, sends, _ = copies(j, "agA")
                        for cp in sends:
                            cp.wait_send()
                        copies(j, "agA")[0].wait()
                    else:
                        finish(j, kinds[j])

    job_shapes = []
    aliases = {}
    for j, (kind, s) in enumerate(jobs):
        shp = (N_DEV,) + tuple(s.shape) if kind in ("ag", "agA", "ag2") else tuple(s.shape)
        job_shapes.append(_sds(shp, s.dtype))
        if kind == "agB":
            aliases[n_in + j] = n_out + j
    sems = [pltpu.SemaphoreType.DMA((nj, N_DEV - 1)), pltpu.SemaphoreType.DMA((nj, N_DEV - 1)),
            pltpu.SemaphoreType.DMA((nj,))] if nj else []
    res = pl.pallas_call(
        wrapped, name=name, grid=grid,
        in_specs=list(in_specs) + [_ANY] * nj,
        out_specs=list(out_specs) + [_ANY] * nj,
        out_shape=list(out_shape) + job_shapes,
        scratch_shapes=list(scratch_shapes) + sems,
        input_output_aliases=aliases,
        compiler_params=pltpu.CompilerParams(dimension_semantics=("arbitrary",), vmem_limit_bytes=VMEM_LIMIT,
                                             has_side_effects=bool(nj)),
    )(*args, *[s for _, s in jobs])
    return list(res[:n_out]), list(res[n_out:])


def exchange(jobs, name):
    def body(o_ref):
        o_ref[...] = jnp.zeros_like(o_ref)

    _, outs = _pcall(body, name, (1,), [], [_small((8, LANES))], [_sds((8, LANES))], [], [], jobs)
    return outs


def ffn_fwd(x, g_pre, w_up, conv_w, conv_b, w_down, g_post, name, jobs=()):
    L = x.shape[0]
    nt = L // TL
    F = FFN_F

    def body(x_ref, gpre_ref, wup_ref, cw_ref, cb_ref, wdn_ref, gpost_ref, up_ref, f_ref, xn_ref, carry_ref):
        i = pl.program_id(0)

        @pl.when(i == 0)
        def _():
            carry_ref[...] = jnp.zeros_like(carry_ref)

        x = x_ref[...]
        h = _rms(x, gpre_ref[...]).astype(BF16)
        up = _mm_nt(h, wup_ref[...])
        up_ref[...] = up
        ug = up[:, :F]
        val = up[:, F:]
        ext = jnp.concatenate([carry_ref[...], ug], axis=0)
        gate = _conv_fwd(_taps(ext, FFN_KW, TL), cw_ref[...]) + cb_ref[...]
        carry_ref[...] = ug[TL - HALO:, :]
        a = (gate * jax.nn.sigmoid(gate) * val).astype(BF16)
        f = _mm(a, wdn_ref[...])
        f_ref[...] = f
        xn_ref[...] = x + _rms(f, gpost_ref[...])

    return _pcall(body, name, (nt,),
                  [_tile(D_MODEL), _small((1, D_MODEL)), _res((2 * F, D_MODEL)), _small((FFN_KW, F)), _small((1, F)),
                   _res((F, D_MODEL)), _small((1, D_MODEL))],
                  [_tile(2 * F), _tile(D_MODEL), _tile(D_MODEL)],
                  [_sds((L, 2 * F)), _sds((L, D_MODEL)), _sds((L, D_MODEL))],
                  [pltpu.VMEM((HALO, F), F32)],
                  [x, g_pre, w_up, conv_w, conv_b, w_down, g_post], jobs)


def ffn_bwd1(dxo, f, up, g_post, w_down, conv_w, conv_b, name, jobs=()):
    L = dxo.shape[0]
    nt = L // TL
    F = FFN_F
    rows = F // N_DEV

    def body(dxo_ref, f_ref, up_ref, halo_ref, gpost_ref, wdn_ref, cw_ref, cb_ref,
             dup_ref, dwdn_ref, dgp_ref, dcw_ref, dcb_ref, acc_ref, carry_ref, stage_ref):
        i = pl.program_id(0)
        t = nt - 1 - i

        @pl.when(i == 0)
        def _():
            acc_ref[...] = jnp.zeros_like(acc_ref)
            carry_ref[...] = jnp.zeros_like(carry_ref)
            dgp_ref[...] = jnp.zeros_like(dgp_ref)
            dcw_ref[...] = jnp.zeros_like(dcw_ref)
            dcb_ref[...] = jnp.zeros_like(dcb_ref)

        df, dgp = _rms_bwd(f_ref[...], gpost_ref[...], dxo_ref[...])
        dgp_ref[...] += dgp
        dfb = df.astype(BF16)
        for c in range(F // CONV_LC):
            cols = slice(c * CONV_LC, (c + 1) * CONV_LC)
            vcols = slice(F + c * CONV_LC, F + (c + 1) * CONV_LC)
            da = _mm_nt(dfb, wdn_ref[cols, :])
            w = cw_ref[:, cols]
            b = cb_ref[:, cols]
            halo = jnp.where(t == 0, 0.0, halo_ref[:, cols])
            carry = carry_ref[:, cols]
            dcb = jnp.zeros((1, CONV_LC), F32)
            dcw = jnp.zeros((FFN_KW, CONV_LC), F32)
            a_rows = [None] * (TL // CONV_RC)
            for r in reversed(range(TL // CONV_RC)):
                r0 = r * CONV_RC
                rows = slice(r0, r0 + CONV_RC)
                if r0 == 0:
                    ext = jnp.concatenate([halo, up_ref[rows, cols]], axis=0)
                else:
                    ext = up_ref[r0 - HALO:r0 + CONV_RC, cols]
                val = up_ref[rows, vcols]
                taps = _taps(ext, FFN_KW, CONV_RC)
                gate = _conv_fwd(taps, w) + b
                s, ds = _silu_parts(gate)
                a_rows[r] = (s * val).astype(BF16)
                da_r = da[rows, :]
                dgate = da_r * val * ds
                dcb = dcb + jnp.sum(dgate, axis=0, keepdims=True)
                dcw = dcw + _conv_bwd_w(taps, dgate)
                extd = jnp.concatenate([dgate, carry], axis=0)
                dup_ref[rows, cols] = _conv_bwd_in(extd, w, FFN_KW, CONV_RC).astype(BF16)
                dup_ref[rows, vcols] = (da_r * s).astype(BF16)
                carry = dgate[:HALO, :]
            carry_ref[:, cols] = carry
            dcb_ref[:, cols] += dcb
            dcw_ref[:, cols] += dcw
            acc_ref[cols, :] += _mm_tn(jnp.concatenate(a_rows, axis=0), dfb)

        @pl.when(i == nt - 1)
        def _():
            _emit_row_shards(acc_ref, dwdn_ref, stage_ref)

    return _pcall(body, name, (nt,),
                  [_rtile(D_MODEL, nt), _rtile(D_MODEL, nt), _rtile(2 * F, nt), _halo_before(F, nt, True),
                   _small((1, D_MODEL)), _res((F, D_MODEL)), _small((FFN_KW, F)), _small((1, F))],
                  [_rtile(2 * F, nt), _ANY, _small((1, D_MODEL)), _small((FFN_KW, F)), _small((1, F))],
                  [_sds((L, 2 * F), BF16), _sds((N_DEV, rows, D_MODEL), BF16), _sds((1, D_MODEL)),
                   _sds((FFN_KW, F)), _sds((1, F))],
                  [pltpu.VMEM((F, D_MODEL), F32), pltpu.VMEM((HALO, F), F32), pltpu.VMEM((rows, D_MODEL), BF16)],
                  [dxo, f, up, up, g_post, w_down, conv_w, conv_b], jobs)


def inproj_bwd(x, g_pre, d, w, dxo, cols, name, jobs=(), transposed=False):
    L = x.shape[0]
    nt = L // TL
    N = d.shape[1]
    w_shape = (N, D_MODEL) if transposed else (D_MODEL, N)
    shard_shape = (cols, D_MODEL) if transposed else (D_MODEL, cols)

    def body(x_ref, g_ref, d_ref, w_ref, dxo_ref, dx_ref, dw_ref, dg_ref, acc_ref, stage_ref):
        i = pl.program_id(0)

        @pl.when(i == 0)
        def _():
            acc_ref[...] = jnp.zeros_like(acc_ref)
            dg_ref[...] = jnp.zeros_like(dg_ref)

        x = x_ref[...]
        g = g_ref[...]
        d = d_ref[...]
        h = _rms(x, g).astype(BF16)
        if transposed:
            dh = _mm(d, w_ref[...])
            acc_ref[...] += _mm_tn(d, h)
        else:
            dh = _mm_nt(d, w_ref[...])
            acc_ref[...] += _mm_tn(h, d)
        dxn, dg = _rms_bwd(x, g, dh)
        dx_ref[...] = dxo_ref[...] + dxn
        dg_ref[...] += dg

        @pl.when(i == nt - 1)
        def _():
            (_emit_row_shards if transposed else _emit_col_shards)(acc_ref, dw_ref, stage_ref)

    return _pcall(body, name, (nt,),
                  [_tile(D_MODEL), _small((1, D_MODEL)), _tile(N), _res(w_shape), _tile(D_MODEL)],
                  [_tile(D_MODEL), _ANY, _small((1, D_MODEL))],
                  [_sds((L, D_MODEL)), _sds((N_DEV,) + shard_shape, BF16), _sds((1, D_MODEL))],
                  [pltpu.VMEM(w_shape, F32), pltpu.VMEM(shard_shape, BF16)],
                  [x, g_pre, d, w, dxo], jobs)


def sc_fwd(x, g_pre, w_in, conv_w, w_out, g_post, name, jobs=()):
    L = x.shape[0]
    nt = L // TL
    W = D_MODEL

    def body(x_ref, gpre_ref, win_ref, cw_ref, wout_ref, gpost_ref, bcv_ref, m_ref, xn_ref, carry_ref):
        i = pl.program_id(0)

        @pl.when(i == 0)
        def _():
            carry_ref[...] = jnp.zeros_like(carry_ref)

        x = x_ref[...]
        h = _rms(x, gpre_ref[...]).astype(BF16)
        bcv = _mm(h, win_ref[...])
        bcv_ref[...] = bcv
        gb = bcv[:, :W]
        p = bcv[:, W:2 * W] * bcv[:, 2 * W:]
        ext = jnp.concatenate([carry_ref[...], p], axis=0)
        u = _conv_fwd(_taps(ext, SC_KW, TL), cw_ref[...])
        carry_ref[...] = p[TL - HALO:, :]
        m = _mm((gb * u).astype(BF16), wout_ref[...])
        m_ref[...] = m
        xn_ref[...] = x + _rms(m, gpost_ref[...])

    return _pcall(body, name, (nt,),
                  [_tile(W), _small((1, W)), _res((W, 3 * W)), _small((SC_KW, W)), _res((W, W)), _small((1, W))],
                  [_tile(3 * W), _tile(W), _tile(W)],
                  [_sds((L, 3 * W)), _sds((L, W)), _sds((L, W))],
                  [pltpu.VMEM((HALO, W), F32)],
                  [x, g_pre, w_in, conv_w, w_out, g_post], jobs)


def sc_bwd1(dxo, m, bcv, g_post, w_out, conv_w, name, jobs=()):
    L = dxo.shape[0]
    nt = L // TL
    W = D_MODEL
    rows = W // N_DEV

    def body(dxo_ref, m_ref, bcv_ref, halo_ref, gpost_ref, wout_ref, cw_ref,
             dbcv_ref, dwout_ref, dgp_ref, dcw_ref, acc_ref, carry_ref, stage_ref):
        i = pl.program_id(0)
        t = nt - 1 - i

        @pl.when(i == 0)
        def _():
            acc_ref[...] = jnp.zeros_like(acc_ref)
            carry_ref[...] = jnp.zeros_like(carry_ref)
            dgp_ref[...] = jnp.zeros_like(dgp_ref)
            dcw_ref[...] = jnp.zeros_like(dcw_ref)

        dm, dgp = _rms_bwd(m_ref[...], gpost_ref[...], dxo_ref[...])
        dgp_ref[...] += dgp
        dmb = dm.astype(BF16)
        dq = _mm_nt(dmb, wout_ref[...])
        bcv = bcv_ref[...]
        gb = bcv[:, :W]
        gc = bcv[:, W:2 * W]
        v = bcv[:, 2 * W:]
        hb = halo_ref[...]
        halo = jnp.where(t == 0, 0.0, hb[:, W:2 * W] * hb[:, 2 * W:])
        ext = jnp.concatenate([halo, gc * v], axis=0)
        w = cw_ref[...]
        taps = _taps(ext, SC_KW, TL)
        u = _conv_fwd(taps, w)
        acc_ref[...] += _mm_tn((gb * u).astype(BF16), dmb)
        dgb = dq * u
        du = dq * gb
        dcw_ref[...] += _conv_bwd_w(taps, du)
        extd = jnp.concatenate([du, carry_ref[...]], axis=0)
        dp = _conv_bwd_in(extd, w, SC_KW, TL)
        carry_ref[...] = du[:HALO, :]
        dbcv_ref[...] = jnp.concatenate([dgb, dp * v, dp * gc], axis=1).astype(BF16)

        @pl.when(i == nt - 1)
        def _():
            _emit_row_shards(acc_ref, dwout_ref, stage_ref)

    return _pcall(body, name, (nt,),
                  [_rtile(W, nt), _rtile(W, nt), _rtile(3 * W, nt), _halo_before(3 * W, nt, True),
                   _small((1, W)), _res((W, W)), _small((SC_KW, W))],
                  [_rtile(3 * W, nt), _ANY, _small((1, W)), _small((SC_KW, W))],
                  [_sds((L, 3 * W), BF16), _sds((N_DEV, rows, W), BF16), _sds((1, W)), _sds((SC_KW, W))],
                  [pltpu.VMEM((W, W), F32), pltpu.VMEM((HALO, W), F32), pltpu.VMEM((rows, W), BF16)],
                  [dxo, m, bcv, bcv, g_post, w_out, conv_w], jobs)


def ssd_inproj(x, g_pre, w_in, conv_w, conv_b, name, jobs=()):
    L = x.shape[0]
    nt = L // TL

    def body(x_ref, gpre_ref, win_ref, cw_ref, cb_ref, z_ref, raw_ref, dt_ref, xh_ref, bm_ref, cm_ref, carry_ref):
        i = pl.program_id(0)

        @pl.when(i == 0)
        def _():
            carry_ref[...] = jnp.zeros_like(carry_ref)

        h = _rms(x_ref[...], gpre_ref[...]).astype(BF16)
        zx = _mm_nt(h, win_ref[...])
        z_ref[...] = zx[:, :SSD_DI]
        raw = zx[:, SSD_DI:SSD_DI + SSD_CONV]
        raw_ref[...] = raw
        dt_ref[...] = zx[:, SSD_DI + SSD_CONV:SSD_IN]
        ext = jnp.concatenate([carry_ref[...], raw], axis=0)
        pre = _conv_fwd(_taps(ext, SSD_KW, TL), cw_ref[...]) + cb_ref[...]
        carry_ref[...] = raw[TL - HALO:, :]
        act = pre * jax.nn.sigmoid(pre)
        for p in range(SSD_H // 2):
            xh_ref[p] = act[:, p * 2 * SSD_P:(p + 1) * 2 * SSD_P]
        for g in range(SSD_G):
            bm_ref[g] = act[:, SSD_DI + g * SSD_N:SSD_DI + (g + 1) * SSD_N]
            cm_ref[g] = act[:, SSD_DI + (SSD_G + g) * SSD_N:SSD_DI + (SSD_G + g + 1) * SSD_N]

    return _pcall(body, name, (nt,),
                  [_tile(D_MODEL), _small((1, D_MODEL)), _res((SSD_IN_PAD, D_MODEL)), _small((SSD_KW, SSD_CONV)),
                   _small((1, SSD_CONV))],
                  [_tile(SSD_DI), _tile(SSD_CONV), _tile(SSD_H),
                   pl.BlockSpec((SSD_H // 2, TL, 2 * SSD_P), lambda i: (0, i, 0)),
                   pl.BlockSpec((SSD_G, TL, SSD_N), lambda i: (0, i, 0)),
                   pl.BlockSpec((SSD_G, TL, SSD_N), lambda i: (0, i, 0))],
                  [_sds((L, SSD_DI)), _sds((L, SSD_CONV)), _sds((L, SSD_H)), _sds((SSD_H // 2, L, 2 * SSD_P)),
                   _sds((SSD_G, L, SSD_N)), _sds((SSD_G, L, SSD_N))],
                  [pltpu.VMEM((HALO, SSD_CONV), F32)],
                  [x, g_pre, w_in, conv_w, conv_b], jobs)


def _per_head(v, heads):
    return jnp.stack([v[:, h:h + 1] for h in heads], axis=0)


def _heads_to_lanes(cols):
    return jnp.concatenate(cols, axis=1)


def _rep_heads(v):
    g, a, b = v.shape
    return jnp.broadcast_to(v[:, None], (g, SSD_R, a, b)).reshape(g * SSD_R, a, b)


def _sum_heads(v):
    h, a, b = v.shape
    return v.reshape(SSD_G, SSD_R, a, b).sum(axis=1)


def _pack_pairs(v):
    h, t, p = v.shape
    v = v.reshape(h // 2, 2, t, p)
    return jnp.concatenate([v[:, 0], v[:, 1]], axis=-1)


def _unpack_pairs(v):
    h2, t, p2 = v.shape
    return jnp.stack([v[:, :, :p2 // 2], v[:, :, p2 // 2:]], axis=1).reshape(2 * h2, t, p2 // 2)


def _chunk_terms(dtr, bias, a_log):
    T = CHUNK
    dt = jax.nn.softplus(dtr + bias)
    a_head = -jnp.exp(a_log)
    ii = lax.broadcasted_iota(jnp.int32, (T, T), 0)
    jj = lax.broadcasted_iota(jnp.int32, (T, T), 1)
    tri = ii >= jj
    cs = jnp.dot(tri.astype(F32), dt * a_head, precision=lax.Precision.HIGHEST, preferred_element_type=F32)
    return dict(dt=dt, a_head=a_head, tri=tri, cs=cs, cs_t=cs.T)


def _head_terms(ct, heads):
    T = CHUNK
    cs, cs_t, tri = ct["cs"], ct["cs_t"], ct["tri"]
    csc = _per_head(cs, heads)
    csr = jnp.stack([cs_t[h:h + 1, :] for h in heads], axis=0)
    cl = _per_head(cs[T - 1:T, :], heads)
    lmat = jnp.exp(jnp.where(tri[None], csc - csr, -jnp.inf))
    return dict(dtc=_per_head(ct["dt"], heads), lmat=lmat, ecs=jnp.exp(csc), dsc=jnp.exp(cl - csc), cdc=jnp.exp(cl))


def ssd_scan_fwd(xh, bm, cm, dt_raw, dt_bias, a_log, d_skip, name, jobs=()):
    L = xh.shape[1]
    nc = L // CHUNK
    T = CHUNK
    TS = SCAN_CPS * T

    def body(xh_ref, bm_ref, cm_ref, dt_ref, bias_ref, alog_ref, dsk_ref, y_ref, sp_ref, st_ref):
        c = pl.program_id(0)

        @pl.when(c == 0)
        def _():
            st_ref[...] = jnp.zeros_like(st_ref)

        heads = range(SSD_H)
        dh = _per_head(dsk_ref[...], heads)
        s = st_ref[...]
        for k in range(SCAN_CPS):
            rows = slice(k * T, (k + 1) * T)
            ht = _head_terms(_chunk_terms(dt_ref[rows, :], bias_ref[...], alog_ref[...]), heads)
            x = _unpack_pairs(xh_ref[:, rows, :])
            bgb = bm_ref[:, rows, :].astype(BF16)
            cgb = cm_ref[:, rows, :].astype(BF16)
            bh = _rep_heads(bgb)
            ch = _rep_heads(cgb)
            xt = x * ht["dtc"]
            cb = jnp.einsum("gln,gsn->gls", cgb, bgb, preferred_element_type=F32)
            mb = (_rep_heads(cb) * ht["lmat"]).astype(BF16)
            yd = jnp.einsum("hls,hsp->hlp", mb, xt.astype(BF16), preferred_element_type=F32)
            sb = s.astype(BF16)
            yo = jnp.einsum("hln,hpn->hlp", ch, sb, preferred_element_type=F32) * ht["ecs"]
            y_ref[:, rows, :] = _pack_pairs(yd + yo + x * dh)
            sp_ref[k] = sb
            xd = (xt * ht["dsc"]).astype(BF16)
            s = s * ht["cdc"] + jnp.einsum("htp,htn->hpn", xd, bh, preferred_element_type=F32)
        st_ref[...] = s

    hd = pl.BlockSpec((SSD_H // 2, TS, 2 * SSD_P), lambda c: (0, c, 0))
    gr = pl.BlockSpec((SSD_G, TS, SSD_N), lambda c: (0, c, 0))
    return _pcall(body, name, (nc // SCAN_CPS,),
                  [hd, gr, gr, pl.BlockSpec((TS, SSD_H), lambda c: (c, 0)),
                   _small((1, SSD_H)), _small((1, SSD_H)), _small((1, SSD_H))],
                  [hd, pl.BlockSpec((SCAN_CPS, SSD_H, SSD_P, SSD_N), lambda c: (c, 0, 0, 0))],
                  [_sds((SSD_H // 2, L, 2 * SSD_P)), _sds((nc, SSD_H, SSD_P, SSD_N), BF16)],
                  [pltpu.VMEM((SSD_H, SSD_P, SSD_N), F32)],
                  [xh, bm, cm, dt_raw, dt_bias, a_log, d_skip], jobs)


def ssd_scan_bwd(dy, xh, bm, cm, dt_raw, sprev, dt_bias, a_log, d_skip, name, jobs=()):
    L = xh.shape[1]
    nc = L // CHUNK
    T = CHUNK

    def body(dy_ref, xh_ref, bm_ref, cm_ref, dt_ref, sp_ref, bias_ref, alog_ref, dsk_ref,
             dxh_ref, dbm_ref, dcm_ref, ddt_ref, dbias_ref, dalog_ref, ddsk_ref, g_ref):
        i = pl.program_id(0)

        @pl.when(i == 0)
        def _():
            g_ref[...] = jnp.zeros_like(g_ref)
            dbias_ref[...] = jnp.zeros_like(dbias_ref)
            dalog_ref[...] = jnp.zeros_like(dalog_ref)
            ddsk_ref[...] = jnp.zeros_like(ddsk_ref)

        bias = bias_ref[...]
        heads = range(SSD_H)
        dh = _per_head(dsk_ref[...], heads)
        g = g_ref[...]
        for k in reversed(range(SCAN_CPS)):
            rows = slice(k * T, (k + 1) * T)
            dtr = dt_ref[rows, :]
            ct = _chunk_terms(dtr, bias, alog_ref[...])
            dt, a_head, tri = ct["dt"], ct["a_head"], ct["tri"]
            ht = _head_terms(ct, heads)
            dtc, lmat, ecs, dsc, cdc = ht["dtc"], ht["lmat"], ht["ecs"], ht["dsc"], ht["cdc"]
            x = _unpack_pairs(xh_ref[:, rows, :])
            dyv = _unpack_pairs(dy_ref[:, rows, :])
            dyb = dyv.astype(BF16)
            bgb = bm_ref[:, rows, :].astype(BF16)
            cgb = cm_ref[:, rows, :].astype(BF16)
            bh = _rep_heads(bgb)
            ch = _rep_heads(cgb)
            sb = sp_ref[k]
            gb = g.astype(BF16)
            xt = x * dtc
            xtb = xt.astype(BF16)
            mf = _rep_heads(jnp.einsum("gln,gsn->gls", cgb, bgb, preferred_element_type=F32)) * lmat
            mb = mf.astype(BF16)
            ddsk = jnp.sum(dyv * x, axis=(1, 2), keepdims=True)
            dx = dyv * dh
            yo_raw = jnp.einsum("hln,hpn->hlp", ch, sb, preferred_element_type=F32)
            w1 = dyv * ecs
            w1b = w1.astype(BF16)
            ds_off = jnp.einsum("hlp,hln->hpn", w1b, ch, preferred_element_type=F32)
            dch = jnp.einsum("hlp,hpn->hln", w1b, sb, preferred_element_type=F32)
            dcs_c = jnp.sum(w1 * yo_raw, axis=2, keepdims=True)
            dm = jnp.einsum("hlp,hsp->hls", dyb, xtb, preferred_element_type=F32)
            dxt = jnp.einsum("hls,hlp->hsp", mb, dyb, preferred_element_type=F32)
            dcbb = _sum_heads(dm * lmat).astype(BF16)
            dseg = dm * mf
            dcs_c = dcs_c + jnp.sum(dseg, axis=2, keepdims=True)
            dcs_r = -jnp.sum(dseg, axis=1, keepdims=True)
            dc = jnp.einsum("gls,gsn->gln", dcbb, bgb, preferred_element_type=F32) + _sum_heads(dch)
            db = jnp.einsum("gls,gln->gsn", dcbb, cgb, preferred_element_type=F32)
            xd = xt * dsc
            dxd = jnp.einsum("htn,hpn->htp", bh, gb, preferred_element_type=F32)
            db = db + _sum_heads(jnp.einsum("htp,hpn->htn", xd.astype(BF16), gb, preferred_element_type=F32))
            dxt = dxt + dxd * dsc
            d_ds = jnp.sum(dxd * xt, axis=2, keepdims=True)
            d_cd = jnp.sum(g * sb.astype(F32), axis=(1, 2), keepdims=True)
            g = g * cdc + ds_off
            t1 = d_ds * dsc
            dcs_c = dcs_c - t1
            dcl = jnp.sum(t1, axis=1, keepdims=True) + d_cd * cdc
            ddt_c = jnp.sum(dxt * x, axis=2, keepdims=True)
            dxh_ref[:, rows, :] = _pack_pairs(dx + dxt * dtc)
            dbm_ref[:, rows, :] = db
            dcm_ref[:, rows, :] = dc
            lanes = lambda v: _heads_to_lanes([v[h] for h in heads])
            rows_t = jnp.concatenate([dcs_r[h] for h in heads], axis=0).T
            last = (lax.broadcasted_iota(jnp.int32, (T, 1), 0) == T - 1).astype(F32)
            dcs = lanes(dcs_c) + rows_t + last * lanes(dcl)
            da = lax.dot_general(tri.astype(F32), dcs, (((0,), (0,)), ((), ())),
                                 precision=lax.Precision.HIGHEST, preferred_element_type=F32)
            ddt = da * a_head + lanes(ddt_c)
            dalog_ref[...] += jnp.sum(da * dt, axis=0, keepdims=True)
            ddtr = ddt * jax.nn.sigmoid(dtr + bias)
            ddt_ref[rows, :] = ddtr
            dbias_ref[...] += jnp.sum(ddtr, axis=0, keepdims=True)
            ddsk_ref[...] += lanes(ddsk)
        g_ref[...] = g

        @pl.when(i == nb - 1)
        def _():
            dalog_ref[...] = dalog_ref[...] * (-jnp.exp(alog_ref[...]))

    nb = nc // SCAN_CPS
    TS = SCAN_CPS * T
    hd = pl.BlockSpec((SSD_H // 2, TS, 2 * SSD_P), lambda i: (0, nb - 1 - i, 0))
    gr = pl.BlockSpec((SSD_G, TS, SSD_N), lambda i: (0, nb - 1 - i, 0))
    tk = pl.BlockSpec((TS, SSD_H), lambda i: (nb - 1 - i, 0))
    return _pcall(body, name, (nb,),
                  [hd, hd, gr, gr, tk, pl.BlockSpec((SCAN_CPS, SSD_H, SSD_P, SSD_N), lambda i: (nb - 1 - i, 0, 0, 0)),
                   _small((1, SSD_H)), _small((1, SSD_H)), _small((1, SSD_H))],
                  [hd, gr, gr, tk, _small((1, SSD_H)), _small((1, SSD_H)), _small((1, SSD_H))],
                  [_sds((SSD_H // 2, L, 2 * SSD_P)), _sds((SSD_G, L, SSD_N)), _sds((SSD_G, L, SSD_N)), _sds((L, SSD_H)),
                   _sds((1, SSD_H)), _sds((1, SSD_H)), _sds((1, SSD_H))],
                  [pltpu.VMEM((SSD_H, SSD_P, SSD_N), F32)],
                  [dy, xh, bm, cm, dt_raw, sprev, dt_bias, a_log, d_skip], jobs)


def _heads_to_tokens(y_ref):
    return jnp.concatenate([y_ref[p] for p in range(SSD_H // 2)], axis=1)


def ssd_out_fwd(x, y, z, norm_w, w_out, g_post, name, jobs=()):
    L = x.shape[0]
    nt = L // TL

    def body(x_ref, y_ref, z_ref, nw_ref, wout_ref, gpost_ref, m_ref, xn_ref):
        z = z_ref[...]
        yg = _heads_to_tokens(y_ref) * (z * jax.nn.sigmoid(z))
        yn = _rms(yg, nw_ref[...]).astype(BF16)
        m = _mm(yn, wout_ref[...])
        m_ref[...] = m
        xn_ref[...] = x_ref[...] + _rms(m, gpost_ref[...])

    return _pcall(body, name, (nt,),
                  [_tile(D_MODEL), pl.BlockSpec((SSD_H // 2, TL, 2 * SSD_P), lambda i: (0, i, 0)), _tile(SSD_DI),
                   _small((1, SSD_DI)), _res((SSD_DI, D_MODEL)), _small((1, D_MODEL))],
                  [_tile(D_MODEL), _tile(D_MODEL)],
                  [_sds((L, D_MODEL)), _sds((L, D_MODEL))],
                  [],
                  [x, y, z, norm_w, w_out, g_post], jobs)


def ssd_out_bwd(dxo, m, y, z, norm_w, w_out, g_post, name, jobs=()):
    L = dxo.shape[0]
    nt = L // TL
    rows = SSD_DI // N_DEV

    def body(dxo_ref, m_ref, y_ref, z_ref, nw_ref, wout_ref, gpost_ref,
             dy_ref, dz_ref, dwout_ref, dgp_ref, dnw_ref, acc_ref, stage_ref):
        i = pl.program_id(0)

        @pl.when(i == 0)
        def _():
            acc_ref[...] = jnp.zeros_like(acc_ref)
            dgp_ref[...] = jnp.zeros_like(dgp_ref)
            dnw_ref[...] = jnp.zeros_like(dnw_ref)

        dm, dgp = _rms_bwd(m_ref[...], gpost_ref[...], dxo_ref[...])
        dgp_ref[...] += dgp
        dmb = dm.astype(BF16)
        dyn = _mm_nt(dmb, wout_ref[...])
        z = z_ref[...]
        y = _heads_to_tokens(y_ref)
        sil, dsil = _silu_parts(z)
        yg = y * sil
        nw = nw_ref[...]
        acc_ref[...] += _mm_tn(_rms(yg, nw).astype(BF16), dmb)
        dyg, dnw = _rms_bwd(yg, nw, dyn)
        dnw_ref[...] += dnw
        dyv = dyg * sil
        dz_ref[...] = dyg * y * dsil
        for p in range(SSD_H // 2):
            dy_ref[p] = dyv[:, p * 2 * SSD_P:(p + 1) * 2 * SSD_P]

        @pl.when(i == nt - 1)
        def _():
            _emit_row_shards(acc_ref, dwout_ref, stage_ref)

    hd = pl.BlockSpec((SSD_H // 2, TL, 2 * SSD_P), lambda i: (0, i, 0))
    return _pcall(body, name, (nt,),
                  [_tile(D_MODEL), _tile(D_MODEL), hd, _tile(SSD_DI), _small((1, SSD_DI)), _res((SSD_DI, D_MODEL)),
                   _small((1, D_MODEL))],
                  [hd, _tile(SSD_DI), _ANY, _small((1, D_MODEL)), _small((1, SSD_DI))],
                  [_sds((SSD_H // 2, L, 2 * SSD_P)), _sds((L, SSD_DI)), _sds((N_DEV, rows, D_MODEL), BF16),
                   _sds((1, D_MODEL)), _sds((1, SSD_DI))],
                  [pltpu.VMEM((SSD_DI, D_MODEL), F32), pltpu.VMEM((rows, D_MODEL), BF16)],
                  [dxo, m, y, z, norm_w, w_out, g_post], jobs)


def ssd_conv_bwd(dxh, dbm, dcm, xbc_raw, dz, ddt_raw, conv_w, conv_b, name, jobs=()):
    L = xbc_raw.shape[0]
    nt = L // TL

    def body(dxh_ref, dbm_ref, dcm_ref, raw_ref, halo_ref, dz_ref, ddt_ref, cw_ref, cb_ref,
             d_ref, dcw_ref, dcb_ref, carry_ref):
        i = pl.program_id(0)
        t = nt - 1 - i

        @pl.when(i == 0)
        def _():
            carry_ref[...] = jnp.zeros_like(carry_ref)
            dcw_ref[...] = jnp.zeros_like(dcw_ref)
            dcb_ref[...] = jnp.zeros_like(dcb_ref)

        def dact_chunk(rows, c):
            lo = c * CONV_LC
            if lo < SSD_DI:
                per, ref, first = 2 * SSD_P, dxh_ref, lo // (2 * SSD_P)
            elif lo < SSD_DI + SSD_G * SSD_N:
                per, ref, first = SSD_N, dbm_ref, (lo - SSD_DI) // SSD_N
            else:
                per, ref, first = SSD_N, dcm_ref, (lo - SSD_DI - SSD_G * SSD_N) // SSD_N
            return jnp.concatenate([ref[first + q, rows, :] for q in range(CONV_LC // per)], axis=1)

        for c in range(SSD_CONV // CONV_LC):
            cols = slice(c * CONV_LC, (c + 1) * CONV_LC)
            w = cw_ref[:, cols]
            b = cb_ref[:, cols]
            halo = jnp.where(t == 0, 0.0, halo_ref[:, cols])
            carry = carry_ref[:, cols]
            dcb = jnp.zeros((1, CONV_LC), F32)
            dcw = jnp.zeros((SSD_KW, CONV_LC), F32)
            for r in reversed(range(TL // CONV_RC)):
                r0 = r * CONV_RC
                rows = slice(r0, r0 + CONV_RC)
                if r0 == 0:
                    ext = jnp.concatenate([halo, raw_ref[rows, cols]], axis=0)
                else:
                    ext = raw_ref[r0 - HALO:r0 + CONV_RC, cols]
                taps = _taps(ext, SSD_KW, CONV_RC)
                pre = _conv_fwd(taps, w) + b
                _, dsil = _silu_parts(pre)
                dpre = dact_chunk(rows, c) * dsil
                dcb = dcb + jnp.sum(dpre, axis=0, keepdims=True)
                dcw = dcw + _conv_bwd_w(taps, dpre)
                extd = jnp.concatenate([dpre, carry], axis=0)
                draw = _conv_bwd_in(extd, w, SSD_KW, CONV_RC)
                carry = dpre[:HALO, :]
                d_ref[rows, SSD_DI + c * CONV_LC:SSD_DI + (c + 1) * CONV_LC] = draw.astype(BF16)
            carry_ref[:, cols] = carry
            dcb_ref[:, cols] += dcb
            dcw_ref[:, cols] += dcw
        d_ref[:, :SSD_DI] = dz_ref[...].astype(BF16)
        tail = jnp.concatenate([ddt_ref[...], jnp.zeros((TL, SSD_IN_PAD - SSD_IN), F32)], axis=1)
        d_ref[:, SSD_DI + SSD_CONV:] = tail.astype(BF16)

    hd = pl.BlockSpec((SSD_H // 2, TL, 2 * SSD_P), lambda i: (0, nt - 1 - i, 0))
    gr = pl.BlockSpec((SSD_G, TL, SSD_N), lambda i: (0, nt - 1 - i, 0))
    return _pcall(body, name, (nt,),
                  [hd, gr, gr, _rtile(SSD_CONV, nt), _halo_before(SSD_CONV, nt, True), _rtile(SSD_DI, nt),
                   _rtile(SSD_H, nt), _small((SSD_KW, SSD_CONV)), _small((1, SSD_CONV))],
                  [_rtile(SSD_IN_PAD, nt), _small((SSD_KW, SSD_CONV)), _small((1, SSD_CONV))],
                  [_sds((L, SSD_IN_PAD), BF16), _sds((SSD_KW, SSD_CONV)), _sds((1, SSD_CONV))],
                  [pltpu.VMEM((HALO, SSD_CONV), F32)],
                  [dxh, dbm, dcm, xbc_raw, xbc_raw, dz, ddt_raw, conv_w, conv_b], jobs)


def loss_fwd_bwd(y, target, name, jobs=()):
    L = y.shape[0]
    nt = L // TL

    def body(y_ref, t_ref, loss_ref, dy_ref):
        i = pl.program_id(0)

        @pl.when(i == 0)
        def _():
            loss_ref[...] = jnp.zeros_like(loss_ref)

        err = y_ref[...] - t_ref[...]
        dy_ref[...] = err * (1.0 / D_MODEL)
        loss_ref[...] += 0.5 * jnp.sum(jnp.mean(err * err, axis=-1, keepdims=True), axis=0, keepdims=True)

    return _pcall(body, name, (nt,),
                  [_tile(D_MODEL), _tile(D_MODEL)],
                  [_small((1, 1)), _tile(D_MODEL)],
                  [_sds((1, 1)), _sds((L, D_MODEL))],
                  [],
                  [y, target], jobs)


def _adamw_math(w, g, m, v):
    m = ADAM_B1 * m + (1.0 - ADAM_B1) * g
    v = ADAM_B2 * v + (1.0 - ADAM_B2) * (g * g)
    m_hat = m / (1.0 - ADAM_B1 ** ADAM_STEP)
    v_hat = v / (1.0 - ADAM_B2 ** ADAM_STEP)
    delta = -ADAM_LR * (m_hat / (jnp.sqrt(v_hat) + ADAM_EPS) + ADAM_WD * w)
    return delta, m, v


def _row_tile(rows):
    for cand in (256, 176, 128, 64, 32, 16, 8):
        if rows % cand == 0:
            return cand
    return rows


def reduce_adamw(recvs, w, m, v, name, jobs=()):
    nl = len(recvs)
    _, R, C = recvs[0].shape
    if R % 16 == 0:
        tr, tc = _row_tile(R), C
    else:
        tr, tc = R, 2 * LANES
    nr = (R // tr) * (C // tc)

    def body(*refs):
        r_refs = refs[:nl]
        w_ref, m_ref, v_ref, g_out, d_out, m_out, v_out = refs[nl:]
        layer = pl.program_id(0) // nr
        for ll in range(nl):
            @pl.when(layer == ll)
            def _(ll=ll):
                g = r_refs[ll][0].astype(F32)
                for j in range(1, N_DEV):
                    g = g + r_refs[ll][j].astype(F32)
                delta, mn, vn = _adamw_math(w_ref[0], g, m_ref[0], v_ref[0])
                g_out[0] = g
                d_out[0] = delta
                m_out[0] = mn
                v_out[0] = vn

    def recv_spec(ll):
        def index(i):
            t = jnp.where(i // nr == ll, i % nr, 0)
            return (0, t, 0) if tc == C else (0, 0, t)
        return pl.BlockSpec((N_DEV, tr, tc), index)

    blk = pl.BlockSpec((1, tr, tc), lambda i: (i // nr, i % nr, 0) if tc == C else (i // nr, 0, i % nr))
    return _pcall(body, name, (nl * nr,),
                  [recv_spec(ll) for ll in range(nl)] + [blk, blk, blk],
                  [blk] * 4,
                  [_sds((nl, R, C))] * 4,
                  [],
                  [*recvs, w, m, v], jobs)


def small_reduce(gathered, name):
    _, R, C = gathered.shape

    def body(r_ref, o_ref):
        g = r_ref[0]
        for j in range(1, N_DEV):
            g = g + r_ref[j]
        o_ref[...] = g

    return pl.pallas_call(body, name=name, out_shape=_sds((R, C)))(gathered)


def small_adamw(g, w, m, v, name):
    def body(g_ref, w_ref, m_ref, v_ref, d_out, m_out, v_out):
        delta, mn, vn = _adamw_math(w_ref[...], g_ref[...], m_ref[...], v_ref[...])
        d_out[...] = delta
        m_out[...] = mn
        v_out[...] = vn

    return pl.pallas_call(body, name=name, out_shape=[_sds(g.shape)] * 3)(g, w, m, v)


def _pack(arrs):
    flat = jnp.concatenate([a.reshape(-1) for a in arrs])
    n = flat.shape[0]
    rows = -(-n // (8 * LANES)) * 8
    flat = jnp.pad(flat, (0, rows * LANES - n))
    return flat.reshape(rows, LANES)


def _unpack(packed, shapes):
    flat = packed.reshape(-1)
    out = []
    off = 0
    for s in shapes:
        n = 1
        for d in s:
            n *= d
        out.append(flat[off:off + n].reshape(s))
        off += n
    return out


def kernel(x, mix_pre_g, mix_post_g, ffn_pre_g, ffn_post_g, ssd_w_in, ssd_conv_w, ssd_conv_b, ssd_dt_bias, ssd_A_log, ssd_D, ssd_norm_w, ssd_w_out, sc_w_in, sc_conv_w, sc_w_out, ffn_w_up, ffn_conv_w, ffn_conv_b, ffn_w_down, loss_target, m_mix_pre_g, m_mix_post_g, m_ffn_pre_g, m_ffn_post_g, m_ssd_w_in, m_ssd_conv_w, m_ssd_conv_b, m_ssd_dt_bias, m_ssd_A_log, m_ssd_D, m_ssd_norm_w, m_ssd_w_out, m_sc_w_in, m_sc_conv_w, m_sc_w_out, m_ffn_w_up, m_ffn_conv_w, m_ffn_conv_b, m_ffn_w_down, v_mix_pre_g, v_mix_post_g, v_ffn_pre_g, v_ffn_post_g, v_ssd_w_in, v_ssd_conv_w, v_ssd_conv_b, v_ssd_dt_bias, v_ssd_A_log, v_ssd_D, v_ssd_norm_w, v_ssd_w_out, v_sc_w_in, v_sc_conv_w, v_sc_w_out, v_ffn_w_up, v_ffn_conv_w, v_ffn_conv_b, v_ffn_w_down):
    me = _my_index()
    x0 = x[0]
    target = loss_target[0]
    row = lambda a: a.reshape(1, -1)

    tr = lambda a: jnp.transpose(a, (0, 2, 1))
    shards = {"ssd_in": tr(ssd_w_in), "ssd_out": ssd_w_out, "sc_in": sc_w_in, "sc_out": sc_w_out,
              "up": tr(ffn_w_up), "down": ffn_w_down}
    col_sharded = ("sc_in",)
    padded_rows = {"ssd_in": SSD_IN_PAD}
    weights = {}

    def shard_bf16(key):
        n, l = key
        return shards[n][l].astype(BF16)

    def store_weights(keys, outs):
        for (n, l), g in zip(keys, outs):
            _, R, C = g.shape
            if n in col_sharded:
                full = jnp.transpose(g, (1, 0, 2)).reshape(R, N_DEV * C)
            else:
                full = g.reshape(N_DEV * R, C)
                if n in padded_rows:
                    full = jnp.pad(full, ((0, padded_rows[n] - N_DEV * R), (0, 0)))
            weights[(n, l)] = full

    fwd_first_half = {
        "ssd_inproj_0": [("ssd_out", 0), ("up", 0), ("down", 0)],
        "ssd_scan_fwd_0": [("sc_in", 0), ("sc_out", 0), ("up", 1)],
        "ssd_out_fwd_0": [("down", 1)],
        "ffn_fwd_0": [("ssd_in", 1)],
        "sc_fwd_0": [("ssd_out", 1), ("down", 2)],
        "ffn_fwd_1": [("up", 2)],
        "ssd_inproj_1": [("sc_in", 1), ("sc_out", 1), ("down", 3)],
        "ssd_scan_fwd_1": [("up", 3)],
    }
    bwd_sched = {
        "ffn_bwd2_3": [("down", 3)], "sc_bwd2_1": [("sc_out", 1)], "ffn_bwd1_2": [("up", 3)], "ffn_bwd2_2": [("sc_in", 1)],
        "ssd_out_bwd_1": [("down", 2)], "ssd_scan_bwd_1": [("up", 2), ("ssd_out", 1)], "ffn_bwd1_1": [("ssd_in", 1)],
        "ffn_bwd2_1": [("down", 1)], "sc_bwd2_0": [("sc_out", 0)], "ffn_bwd1_0": [("up", 1)], "ffn_bwd2_0": [("sc_in", 0)],
        "ssd_out_bwd_0": [("down", 0)], "ssd_scan_bwd_0": [("up", 0), ("ssd_out", 0)],
    }

    first = [("ssd_in", 0)]
    outs = exchange([("ag2", shard_bf16(k)) for k in first]
                    + [("ag", ssd_conv_w), ("ag", sc_conv_w), ("ag", ffn_conv_w)], "ag_first")
    store_weights(first, outs[:1])

    def taps(g):
        _, nl, K, C = g.shape
        return jnp.transpose(g, (1, 2, 0, 3)).reshape(nl, K, N_DEV * C)

    CW_ssd, CW_sc, CW_ffn = taps(outs[1]), taps(outs[2]), taps(outs[3])

    half_done = []

    def fwd(fn, name, *args):
        second = list(half_done)
        starting = fwd_first_half.get(name, [])
        jobs = [("agB", buf) for _, buf in second] + [("agA", shard_bf16(k)) for k in starting]
        res, got = fn(*args, name, jobs)
        store_weights([k for k, _ in second], got[:len(second)])
        half_done[:] = list(zip(starting, got[len(second):]))
        return res

    saved = []
    h = x0
    for i in range(DEPTH):
        j = i // 2
        blk = dict(x_mix=h)
        if i % 2 == 0:
            z, raw, dt_raw, xh, bm, cm = fwd(ssd_inproj, f"ssd_inproj_{j}", h, row(mix_pre_g[i]), weights[("ssd_in", j)],
                                             CW_ssd[j], row(ssd_conv_b[j]))
            y, sprev = fwd(ssd_scan_fwd, f"ssd_scan_fwd_{j}", xh, bm, cm, dt_raw, row(ssd_dt_bias[j]),
                           row(ssd_A_log[j]), row(ssd_D[j]))
            m, h = fwd(ssd_out_fwd, f"ssd_out_fwd_{j}", h, y, z, row(ssd_norm_w[j]), weights[("ssd_out", j)],
                       row(mix_post_g[i]))
            blk.update(z=z, raw=raw, dt_raw=dt_raw, xh=xh, bm=bm, cm=cm, y=y, sprev=sprev, m=m)
        else:
            bcv, m, h = fwd(sc_fwd, f"sc_fwd_{j}", h, row(mix_pre_g[i]), weights[("sc_in", j)], CW_sc[j],
                            weights[("sc_out", j)], row(mix_post_g[i]))
            blk.update(bcv=bcv, m=m)
        blk["x_ffn"] = h
        up, f, h = fwd(ffn_fwd, f"ffn_fwd_{i}", h, row(ffn_pre_g[i]), weights[("up", i)], CW_ffn[i],
                       row(ffn_conv_b[i]), weights[("down", i)], row(ffn_post_g[i]))
        blk.update(up=up, f=f)
        saved.append(blk)

    (loss_dev, dh), _ = loss_fwd_bwd(h, target, "loss")
    loss = lax.psum(loss_dev[0, 0], ("x", "y", "c"))

    parts, recvd = {}, {}

    def bwd(fn, name, *args):
        keys = bwd_sched.get(name, [])
        res, got = fn(*args, name, [("a2a", parts[k]) for k in keys])
        for k, g in zip(keys, got):
            recvd[k] = g
        return res

    g_mix_pre, g_mix_post, g_ffn_pre, g_ffn_post = [None] * DEPTH, [None] * DEPTH, [None] * DEPTH, [None] * DEPTH
    g_ffn_cw, g_ffn_cb = [None] * DEPTH, [None] * DEPTH
    g_ssd_cw, g_ssd_cb, g_ssd_dtb, g_ssd_alog, g_ssd_d, g_ssd_nw = ([None] * 2 for _ in range(6))
    g_sc_cw = [None] * 2
    for i in reversed(range(DEPTH)):
        j = i // 2
        blk = saved[i]
        dup, parts[("down", i)], g_ffn_post[i], g_ffn_cw[i], g_ffn_cb[i] = bwd(
            ffn_bwd1, f"ffn_bwd1_{i}", dh, blk["f"], blk["up"], row(ffn_post_g[i]), weights[("down", i)], CW_ffn[i],
            row(ffn_conv_b[i]))
        dh, parts[("up", i)], g_ffn_pre[i] = bwd(functools.partial(inproj_bwd, transposed=True), f"ffn_bwd2_{i}",
                                                  blk["x_ffn"], row(ffn_pre_g[i]), dup, weights[("up", i)], dh,
                                                  2 * FFN_F // N_DEV)
        if i % 2 == 0:
            dy, dz, parts[("ssd_out", j)], g_mix_post[i], g_ssd_nw[j] = bwd(
                ssd_out_bwd, f"ssd_out_bwd_{j}", dh, blk["m"], blk["y"], blk["z"], row(ssd_norm_w[j]),
                weights[("ssd_out", j)], row(mix_post_g[i]))
            dxh, dbm, dcm, ddt, g_ssd_dtb[j], g_ssd_alog[j], g_ssd_d[j] = bwd(
                ssd_scan_bwd, f"ssd_scan_bwd_{j}", dy, blk["xh"], blk["bm"], blk["cm"], blk["dt_raw"], blk["sprev"],
                row(ssd_dt_bias[j]), row(ssd_A_log[j]), row(ssd_D[j]))
            d_in, g_ssd_cw[j], g_ssd_cb[j] = bwd(ssd_conv_bwd, f"ssd_conv_bwd_{j}", dxh, dbm, dcm, blk["raw"], dz, ddt,
                                                 CW_ssd[j], row(ssd_conv_b[j]))
            dh, parts[("ssd_in", j)], g_mix_pre[i] = bwd(functools.partial(inproj_bwd, transposed=True),
                                                          f"ssd_bwd2_{j}", blk["x_mix"], row(mix_pre_g[i]), d_in,
                                                          weights[("ssd_in", j)], dh, SSD_IN // N_DEV)
        else:
            dbcv, parts[("sc_out", j)], g_mix_post[i], g_sc_cw[j] = bwd(
                sc_bwd1, f"sc_bwd1_{j}", dh, blk["m"], blk["bcv"], row(mix_post_g[i]), weights[("sc_out", j)], CW_sc[j])
            dh, parts[("sc_in", j)], g_mix_pre[i] = bwd(inproj_bwd, f"sc_bwd2_{j}", blk["x_mix"], row(mix_pre_g[i]),
                                                         dbcv, weights[("sc_in", j)], dh, 3 * D_MODEL // N_DEV)
    grad_x = dh[None]

    st = lambda lst: jnp.concatenate(lst, axis=0)
    small_full = [
        st(g_mix_pre), st(g_mix_post), st(g_ffn_pre), st(g_ffn_post),
        jnp.stack(g_ssd_cw), st(g_ssd_cb), st(g_ssd_dtb), st(g_ssd_alog), st(g_ssd_d), st(g_ssd_nw),
        jnp.stack(g_sc_cw), jnp.stack(g_ffn_cw), st(g_ffn_cb),
    ]
    full_shapes = [a.shape for a in small_full]

    recvd[("ssd_in", 0)], small_gathered = exchange([("a2a", parts[("ssd_in", 0)]), ("ag", _pack(small_full))],
                                                    "a2a_last")

    def finish(n, nl, w, m, v):
        return reduce_adamw([recvd[(n, l)] for l in range(nl)], w, m, v, "adamw_" + n)[0]

    r_up = [tr(a) for a in finish("up", DEPTH, tr(ffn_w_up), tr(m_ffn_w_up), tr(v_ffn_w_up))]
    r_down = finish("down", DEPTH, ffn_w_down, m_ffn_w_down, v_ffn_w_down)
    r_ssd_out = finish("ssd_out", 2, ssd_w_out, m_ssd_w_out, v_ssd_w_out)
    r_sc_in = finish("sc_in", 2, sc_w_in, m_sc_w_in, v_sc_w_in)
    r_sc_out = finish("sc_out", 2, sc_w_out, m_sc_w_out, v_sc_w_out)
    r_ssd_in = [tr(a) for a in finish("ssd_in", 2, tr(ssd_w_in), tr(m_ssd_w_in), tr(v_ssd_w_in))]

    summed = small_reduce(small_gathered, "small_reduce")
    (s_mix_pre, s_mix_post, s_ffn_pre, s_ffn_post, s_ssd_cw, s_ssd_cb, s_ssd_dtb, s_ssd_alog, s_ssd_d, s_ssd_nw,
     s_sc_cw, s_ffn_cw, s_ffn_cb) = _unpack(summed, full_shapes)

    def my_cols(a, width):
        return lax.dynamic_slice_in_dim(a, me * width, width, axis=a.ndim - 1)

    s_ssd_cw = my_cols(s_ssd_cw, SSD_CONV // N_DEV)
    s_sc_cw = my_cols(s_sc_cw, D_MODEL // N_DEV)
    s_ffn_cw = my_cols(s_ffn_cw, FFN_F // N_DEV)

    small_g = [s_mix_pre, s_mix_post, s_ffn_pre, s_ffn_post, s_ssd_cw, s_ssd_cb, s_ssd_dtb, s_ssd_alog, s_ssd_d,
               s_ssd_nw, s_sc_cw, s_ffn_cw, s_ffn_cb]
    small_w = [mix_pre_g, mix_post_g, ffn_pre_g, ffn_post_g, ssd_conv_w, ssd_conv_b, ssd_dt_bias, ssd_A_log, ssd_D,
               ssd_norm_w, sc_conv_w, ffn_conv_w, ffn_conv_b]
    small_m = [m_mix_pre_g, m_mix_post_g, m_ffn_pre_g, m_ffn_post_g, m_ssd_conv_w, m_ssd_conv_b, m_ssd_dt_bias,
               m_ssd_A_log, m_ssd_D, m_ssd_norm_w, m_sc_conv_w, m_ffn_conv_w, m_ffn_conv_b]
    small_v = [v_mix_pre_g, v_mix_post_g, v_ffn_pre_g, v_ffn_post_g, v_ssd_conv_w, v_ssd_conv_b, v_ssd_dt_bias,
               v_ssd_A_log, v_ssd_D, v_ssd_norm_w, v_sc_conv_w, v_ffn_conv_w, v_ffn_conv_b]
    local_shapes = [a.shape for a in small_w]
    pd, pm, pv = small_adamw(_pack(small_g), _pack(small_w), _pack(small_m), _pack(small_v), "small_adamw")
    sd = _unpack(pd, local_shapes)
    sm = _unpack(pm, local_shapes)
    sv = _unpack(pv, local_shapes)

    def ordered(small, big):
        (mix_pre, mix_post, ffn_pre, ffn_post, ssd_cw, ssd_cb, dtb, alog, dsk, nw, sc_cw, ffn_cw, ffn_cb) = small
        (b_ssd_in, b_ssd_out, b_sc_in, b_sc_out, b_up, b_down) = big
        return [mix_pre, mix_post, ffn_pre, ffn_post, b_ssd_in, ssd_cw, ssd_cb, dtb, alog, dsk, nw, b_ssd_out,
                b_sc_in, sc_cw, b_sc_out, b_up, ffn_cw, ffn_cb, b_down]

    bigs = [r_ssd_in, r_ssd_out, r_sc_in, r_sc_out, r_up, r_down]
    grads = ordered(small_g, [r[0] for r in bigs])
    deltas = ordered(sd, [r[1] for r in bigs])
    new_m = ordered(sm, [r[2] for r in bigs])
    new_v = ordered(sv, [r[3] for r in bigs])
    return (loss, grad_x, *grads, *deltas, *new_m, *new_v)
```

```python
import functools

import jax
import jax.numpy as jnp
from jax import lax
from jax.experimental import pallas as pl
from jax.experimental.pallas import tpu as pltpu

F32 = jnp.float32
BF16 = jnp.bfloat16

EPS = 1e-6
D_MODEL = 1024
DEPTH = 4
N_DEV = 8
CHUNK = 64
SSD_DI = 2048
SSD_H = 32
SSD_P = 64
SSD_G = 8
SSD_R = SSD_H // SSD_G
SSD_N = 128
SSD_CONV = SSD_DI + 2 * SSD_G * SSD_N
SSD_IN = SSD_DI + SSD_CONV + SSD_H
LANES = 128
SSD_IN_PAD = -(-SSD_IN // LANES) * LANES
SSD_KW = 4
SC_KW = 3
FFN_F = 2816
FFN_KW = 3
SCAN_CPS = 4
TL = 256
HALO = 8
CONV_RC = 128
CONV_LC = 256
VMEM_LIMIT = 60 * 1024 * 1024

ADAM_LR = 0.001
ADAM_B1 = 0.9
ADAM_B2 = 0.999
ADAM_EPS = 1e-08
ADAM_WD = 0.01
ADAM_STEP = 10

MESH = pl.DeviceIdType.MESH


def _rms(x, g):
    r = lax.rsqrt(jnp.mean(x * x, axis=-1, keepdims=True) + EPS)
    return x * r * g


def _rms_bwd(x, g, dy):
    r = lax.rsqrt(jnp.mean(x * x, axis=-1, keepdims=True) + EPS)
    xh = x * r
    dg = jnp.sum(dy * xh, axis=0, keepdims=True)
    dxh = dy * g
    dx = r * (dxh - xh * jnp.mean(dxh * xh, axis=-1, keepdims=True))
    return dx, dg


def _mm(a, b):
    return jnp.dot(a, b, preferred_element_type=F32)


def _mm_nt(a, b):
    return lax.dot_general(a, b, (((1,), (1,)), ((), ())), preferred_element_type=F32)


def _mm_tn(a, b):
    return lax.dot_general(a, b, (((0,), (0,)), ((), ())), preferred_element_type=F32)


def _silu_parts(x):
    sg = jax.nn.sigmoid(x)
    return x * sg, sg * (1.0 + x * (1.0 - sg))


def _rows_from(a, s, tl):
    if s % HALO == 0:
        return a[s:s + tl]
    return pltpu.roll(a, a.shape[0] - s, 0)[0:tl]


def _taps(ext, kw, tl):
    base = HALO - (kw - 1)
    return [_rows_from(ext, base + j, tl) for j in range(kw)]


def _conv_fwd(taps, w):
    out = taps[0] * w[0:1]
    for j in range(1, len(taps)):
        out = out + taps[j] * w[j:j + 1]
    return out


def _conv_bwd_in(extd, w, kw, tl):
    out = _rows_from(extd, kw - 1, tl) * w[0:1]
    for j in range(1, kw):
        out = out + _rows_from(extd, kw - 1 - j, tl) * w[j:j + 1]
    return out


def _conv_bwd_w(taps, dy):
    return jnp.concatenate([jnp.sum(dy * t, axis=0, keepdims=True) for t in taps], axis=0)


def _emit_row_shards(acc_ref, out_ref, stage_ref):
    rows = out_ref.shape[1]
    for k in range(N_DEV):
        stage_ref[...] = acc_ref[k * rows:(k + 1) * rows, :].astype(BF16)
        pltpu.sync_copy(stage_ref, out_ref.at[k])


def _emit_col_shards(acc_ref, out_ref, stage_ref):
    cols = out_ref.shape[2]
    for k in range(N_DEV):
        stage_ref[...] = acc_ref[:, k * cols:(k + 1) * cols].astype(BF16)
        pltpu.sync_copy(stage_ref, out_ref.at[k])


def _res(shape):
    nd = len(shape)
    return pl.BlockSpec(shape, lambda i: (0,) * nd, pipeline_mode=pl.Buffered(1))


def _small(shape):
    nd = len(shape)
    return pl.BlockSpec(shape, lambda i: (0,) * nd)


def _tile(n):
    return pl.BlockSpec((TL, n), lambda i: (i, 0))


def _rtile(n, nt):
    return pl.BlockSpec((TL, n), lambda i: (nt - 1 - i, 0))


def _halo_before(n, nt, reverse):
    per = TL // HALO
    if reverse:
        return pl.BlockSpec((HALO, n), lambda i: (jnp.maximum((nt - 1 - i) * per - 1, 0), 0))
    return pl.BlockSpec((HALO, n), lambda i: (jnp.maximum(i * per - 1, 0), 0))


_ANY = pl.BlockSpec(memory_space=pl.ANY)


def _sds(shape, dtype=F32):
    return jax.ShapeDtypeStruct(shape, dtype)


def _peer(k):
    x, y, c = lax.axis_index("x"), lax.axis_index("y"), lax.axis_index("c")
    px = x ^ (k >> 2)
    py = y ^ ((k >> 1) & 1)
    pc = c ^ (k & 1)
    return (px, py, pc), 4 * px + 2 * py + pc


def _my_index():
    return 4 * lax.axis_index("x") + 2 * lax.axis_index("y") + lax.axis_index("c")


SIBLING = 1
SAME_CORE_CHIPS = (2, 4, 6)


def _job_copies(kind, src_ref, out_ref, send_sems, recv_sems, local_sems, j):
    me = _my_index()
    sends, recvs = [], []

    def pair(pattern, sem, src, put_slot, get_slot):
        dev, _ = _peer(pattern)
        sems = dict(send_sem=send_sems.at[j, sem], recv_sem=recv_sems.at[j, sem], device_id=dev, device_id_type=MESH)
        sends.append(pltpu.make_async_remote_copy(src_ref=src, dst_ref=out_ref.at[put_slot], **sems))
        recvs.append(pltpu.make_async_remote_copy(src_ref=src, dst_ref=out_ref.at[get_slot], **sems))

    if kind == "agB":
        for k in SAME_CORE_CHIPS:
            _, mine_from_k = _peer(k)
            _, sib_from_k = _peer(k | SIBLING)
            pair(SIBLING, k, out_ref.at[mine_from_k], mine_from_k, sib_from_k)
        return None, sends, recvs
    patterns = (SIBLING,) + SAME_CORE_CHIPS if kind == "agA" else range(1, N_DEV)
    mine = src_ref.at[me] if kind == "a2a" else src_ref
    local = pltpu.make_async_copy(mine, out_ref.at[me], local_sems.at[j])
    for k in patterns:
        _, idx = _peer(k)
        pair(k, k - 1, src_ref.at[idx] if kind == "a2a" else src_ref, me, idx)
    return local, sends, recvs


def _pcall(body, name, grid, in_specs, out_specs, out_shape, scratch_shapes, args, jobs=()):
    n_in, n_out, nj = len(in_specs), len(out_specs), len(jobs)
    last = grid[0] - 1
    kinds = [k for k, _ in jobs]

    def wrapped(*refs):
        ins = refs[:n_in]
        csrc = refs[n_in:n_in + nj]
        outs = refs[n_in + nj:n_in + nj + n_out]
        cout = refs[n_in + nj + n_out:n_in + 2 * nj + n_out]
        rest = refs[n_in + 2 * nj + n_out:]

        def copies(j, kind):
            return _job_copies(kind, csrc[j], cout[j], send_sems, recv_sems, local_sems, j)

        def start(j, kind):
            local, sends, _ = copies(j, kind)
            if local is not None:
                local.start()
            for cp in sends:
                cp.start()

        def finish(j, kind, arrivals_only=False):
            local, sends, recvs = copies(j, kind)
            for cp in recvs:
                cp.wait_recv()
            if not arrivals_only:
                for cp in sends:
                    cp.wait_send()
                if local is not None:
                    local.wait()

        if nj:
            scratch, (send_sems, recv_sems, local_sems) = rest[:-3], rest[-3:]
            i = pl.program_id(0)

            @pl.when(i == 0)
            def _():
                for j in range(nj):
                    start(j, "agA" if kinds[j] == "ag2" else kinds[j])
                for j in range(nj):
                    if kinds[j] == "ag2":
                        finish(j, "agA", arrivals_only=True)
                        start(j, "agB")
        else:
            scratch = rest
        body(*ins, *outs, *scratch)
        if nj:
            @pl.when(i == last)
            def _():
                for j in range(nj):
                    if kinds[j] == "ag2":
                        finish(j, "agB")
                        _, sends, _ = copies(j, "agA")
                        for cp in sends:
                            cp.wait_send()
                        copies(j, "agA")[0].wait()
                    else:
                        finish(j, kinds[j])

    job_shapes = []
    aliases = {}
    for j, (kind, s) in enumerate(jobs):
        shp = (N_DEV,) + tuple(s.shape) if kind in ("ag", "agA", "ag2") else tuple(s.shape)
        job_shapes.append(_sds(shp, s.dtype))
        if kind == "agB":
            aliases[n_in + j] = n_out + j
    sems = [pltpu.SemaphoreType.DMA((nj, N_DEV - 1)), pltpu.SemaphoreType.DMA((nj, N_DEV - 1)),
            pltpu.SemaphoreType.DMA((nj,))] if nj else []
    res = pl.pallas_call(
        wrapped, name=name, grid=grid,
        in_specs=list(in_specs) + [_ANY] * nj,
        out_specs=list(out_specs) + [_ANY] * nj,
        out_shape=list(out_shape) + job_shapes,
        scratch_shapes=list(scratch_shapes) + sems,
        input_output_aliases=aliases,
        compiler_params=pltpu.CompilerParams(dimension_semantics=("arbitrary",), vmem_limit_bytes=VMEM_LIMIT,
                                             has_side_effects=bool(nj)),
    )(*args, *[s for _, s in jobs])
    return list(res[:n_out]), list(res[n_out:])


def exchange(jobs, name):
    def body(o_ref):
        o_ref[...] = jnp.zeros_like(o_ref)

    _, outs = _pcall(body, name, (1,), [], [_small((8, LANES))], [_sds((8, LANES))], [], [], jobs)
    return outs


def ffn_fwd(x, g_pre, w_up, conv_w, conv_b, w_down, g_post, name, jobs=()):
    L = x.shape[0]
    nt = L // TL
    F = FFN_F

    def body(x_ref, gpre_ref, wup_ref, cw_ref, cb_ref, wdn_ref, gpost_ref, up_ref, f_ref, xn_ref, carry_ref):
        i = pl.program_id(0)

        @pl.when(i == 0)
        def _():
            carry_ref[...] = jnp.zeros_like(carry_ref)

        x = x_ref[...]
        h = _rms(x, gpre_ref[...]).astype(BF16)
        up = _mm_nt(h, wup_ref[...])
        up_ref[...] = up
        ug = up[:, :F]
        val = up[:, F:]
        ext = jnp.concatenate([carry_ref[...], ug], axis=0)
        gate = _conv_fwd(_taps(ext, FFN_KW, TL), cw_ref[...]) + cb_ref[...]
        carry_ref[...] = ug[TL - HALO:, :]
        a = (gate * jax.nn.sigmoid(gate) * val).astype(BF16)
        f = _mm(a, wdn_ref[...])
        f_ref[...] = f
        xn_ref[...] = x + _rms(f, gpost_ref[...])

    return _pcall(body, name, (nt,),
                  [_tile(D_MODEL), _small((1, D_MODEL)), _res((2 * F, D_MODEL)), _small((FFN_KW, F)), _small((1, F)),
                   _res((F, D_MODEL)), _small((1, D_MODEL))],
                  [_tile(2 * F), _tile(D_MODEL), _tile(D_MODEL)],
                  [_sds((L, 2 * F)), _sds((L, D_MODEL)), _sds((L, D_MODEL))],
                  [pltpu.VMEM((HALO, F), F32)],
                  [x, g_pre, w_up, conv_w, conv_b, w_down, g_post], jobs)


def ffn_bwd1(dxo, f, up, g_post, w_down, conv_w, conv_b, name, jobs=()):
    L = dxo.shape[0]
    nt = L // TL
    F = FFN_F
    rows = F // N_DEV

    def body(dxo_ref, f_ref, up_ref, halo_ref, gpost_ref, wdn_ref, cw_ref, cb_ref,
             dup_ref, dwdn_ref, dgp_ref, dcw_ref, dcb_ref, acc_ref, carry_ref, stage_ref):
        i = pl.program_id(0)
        t = nt - 1 - i

        @pl.when(i == 0)
        def _():
            acc_ref[...] = jnp.zeros_like(acc_ref)
            carry_ref[...] = jnp.zeros_like(carry_ref)
            dgp_ref[...] = jnp.zeros_like(dgp_ref)
            dcw_ref[...] = jnp.zeros_like(dcw_ref)
            dcb_ref[...] = jnp.zeros_like(dcb_ref)

        df, dgp = _rms_bwd(f_ref[...], gpost_ref[...], dxo_ref[...])
        dgp_ref[...] += dgp
        dfb = df.astype(BF16)
        for c in range(F // CONV_LC):
            cols = slice(c * CONV_LC, (c + 1) * CONV_LC)
            vcols = slice(F + c * CONV_LC, F + (c + 1) * CONV_LC)
            da = _mm_nt(dfb, wdn_ref[cols, :])
            w = cw_ref[:, cols]
            b = cb_ref[:, cols]
            halo = jnp.where(t == 0, 0.0, halo_ref[:, cols])
            carry = carry_ref[:, cols]
            dcb = jnp.zeros((1, CONV_LC), F32)
            dcw = jnp.zeros((FFN_KW, CONV_LC), F32)
            a_rows = [None] * (TL // CONV_RC)
            for r in reversed(range(TL // CONV_RC)):
                r0 = r * CONV_RC
                rows = slice(r0, r0 + CONV_RC)
                if r0 == 0:
                    ext = jnp.concatenate([halo, up_ref[rows, cols]], axis=0)
                else:
                    ext = up_ref[r0 - HALO:r0 + CONV_RC, cols]
                val = up_ref[rows, vcols]
                taps = _taps(ext, FFN_KW, CONV_RC)
                gate = _conv_fwd(taps, w) + b
                s, ds = _silu_parts(gate)
                a_rows[r] = (s * val).astype(BF16)
                da_r = da[rows, :]
                dgate = da_r * val * ds
                dcb = dcb + jnp.sum(dgate, axis=0, keepdims=True)
                dcw = dcw + _conv_bwd_w(taps, dgate)
                extd = jnp.concatenate([dgate, carry], axis=0)
                dup_ref[rows, cols] = _conv_bwd_in(extd, w, FFN_KW, CONV_RC).astype(BF16)
                dup_ref[rows, vcols] = (da_r * s).astype(BF16)
                carry = dgate[:HALO, :]
            carry_ref[:, cols] = carry
            dcb_ref[:, cols] += dcb
            dcw_ref[:, cols] += dcw
            acc_ref[cols, :] += _mm_tn(jnp.concatenate(a_rows, axis=0), dfb)

        @pl.when(i == nt - 1)
        def _():
            _emit_row_shards(acc_ref, dwdn_ref, stage_ref)

    return _pcall(body, name, (nt,),
                  [_rtile(D_MODEL, nt), _rtile(D_MODEL, nt), _rtile(2 * F, nt), _halo_before(F, nt, True),
                   _small((1, D_MODEL)), _res((F, D_MODEL)), _small((FFN_KW, F)), _small((1, F))],
                  [_rtile(2 * F, nt), _ANY, _small((1, D_MODEL)), _small((FFN_KW, F)), _small((1, F))],
                  [_sds((L, 2 * F), BF16), _sds((N_DEV, rows, D_MODEL), BF16), _sds((1, D_MODEL)),
                   _sds((FFN_KW, F)), _sds((1, F))],
                  [pltpu.VMEM((F, D_MODEL), F32), pltpu.VMEM((HALO, F), F32), pltpu.VMEM((rows, D_MODEL), BF16)],
                  [dxo, f, up, up, g_post, w_down, conv_w, conv_b], jobs)


def inproj_bwd(x, g_pre, d, w, dxo, cols, name, jobs=(), transposed=False):
    L = x.shape[0]
    nt = L // TL
    N = d.shape[1]
    w_shape = (N, D_MODEL) if transposed else (D_MODEL, N)
    shard_shape = (cols, D_MODEL) if transposed else (D_MODEL, cols)

    def body(x_ref, g_ref, d_ref, w_ref, dxo_ref, dx_ref, dw_ref, dg_ref, acc_ref, stage_ref):
        i = pl.program_id(0)

        @pl.when(i == 0)
        def _():
            acc_ref[...] = jnp.zeros_like(acc_ref)
            dg_ref[...] = jnp.zeros_like(dg_ref)

        x = x_ref[...]
        g = g_ref[...]
        d = d_ref[...]
        h = _rms(x, g).astype(BF16)
        if transposed:
            dh = _mm(d, w_ref[...])
            acc_ref[...] += _mm_tn(d, h)
        else:
            dh = _mm_nt(d, w_ref[...])
            acc_ref[...] += _mm_tn(h, d)
        dxn, dg = _rms_bwd(x, g, dh)
        dx_ref[...] = dxo_ref[...] + dxn
        dg_ref[...] += dg

        @pl.when(i == nt - 1)
        def _():
            (_emit_row_shards if transposed else _emit_col_shards)(acc_ref, dw_ref, stage_ref)

    return _pcall(body, name, (nt,),
                  [_tile(D_MODEL), _small((1, D_MODEL)), _tile(N), _res(w_shape), _tile(D_MODEL)],
                  [_tile(D_MODEL), _ANY, _small((1, D_MODEL))],
                  [_sds((L, D_MODEL)), _sds((N_DEV,) + shard_shape, BF16), _sds((1, D_MODEL))],
                  [pltpu.VMEM(w_shape, F32), pltpu.VMEM(shard_shape, BF16)],
                  [x, g_pre, d, w, dxo], jobs)


def sc_fwd(x, g_pre, w_in, conv_w, w_out, g_post, name, jobs=()):
    L = x.shape[0]
    nt = L // TL
    W = D_MODEL

    def body(x_ref, gpre_ref, win_ref, cw_ref, wout_ref, gpost_ref, bcv_ref, m_ref, xn_ref, carry_ref):
        i = pl.program_id(0)

        @pl.when(i == 0)
        def _():
            carry_ref[...] = jnp.zeros_like(carry_ref)

        x = x_ref[...]
        h = _rms(x, gpre_ref[...]).astype(BF16)
        bcv = _mm(h, win_ref[...])
        bcv_ref[...] = bcv
        gb = bcv[:, :W]
        p = bcv[:, W:2 * W] * bcv[:, 2 * W:]
        ext = jnp.concatenate([carry_ref[...], p], axis=0)
        u = _conv_fwd(_taps(ext, SC_KW, TL), cw_ref[...])
        carry_ref[...] = p[TL - HALO:, :]
        m = _mm((gb * u).astype(BF16), wout_ref[...])
        m_ref[...] = m
        xn_ref[...] = x + _rms(m, gpost_ref[...])

    return _pcall(body, name, (nt,),
                  [_tile(W), _small((1, W)), _res((W, 3 * W)), _small((SC_KW, W)), _res((W, W)), _small((1, W))],
                  [_tile(3 * W), _tile(W), _tile(W)],
                  [_sds((L, 3 * W)), _sds((L, W)), _sds((L, W))],
                  [pltpu.VMEM((HALO, W), F32)],
                  [x, g_pre, w_in, conv_w, w_out, g_post], jobs)


def sc_bwd1(dxo, m, bcv, g_post, w_out, conv_w, name, jobs=()):
    L = dxo.shape[0]
    nt = L // TL
    W = D_MODEL
    rows = W // N_DEV

    def body(dxo_ref, m_ref, bcv_ref, halo_ref, gpost_ref, wout_ref, cw_ref,
             dbcv_ref, dwout_ref, dgp_ref, dcw_ref, acc_ref, carry_ref, stage_ref):
        i = pl.program_id(0)
        t = nt - 1 - i

        @pl.when(i == 0)
        def _():
            acc_ref[...] = jnp.zeros_like(acc_ref)
            carry_ref[...] = jnp.zeros_like(carry_ref)
            dgp_ref[...] = jnp.zeros_like(dgp_ref)
            dcw_ref[...] = jnp.zeros_like(dcw_ref)

        dm, dgp = _rms_bwd(m_ref[...], gpost_ref[...], dxo_ref[...])
        dgp_ref[...] += dgp
        dmb = dm.astype(BF16)
        dq = _mm_nt(dmb, wout_ref[...])
        bcv = bcv_ref[...]
        gb = bcv[:, :W]
        gc = bcv[:, W:2 * W]
        v = bcv[:, 2 * W:]
        hb = halo_ref[...]
        halo = jnp.where(t == 0, 0.0, hb[:, W:2 * W] * hb[:, 2 * W:])
        ext = jnp.concatenate([halo, gc * v], axis=0)
        w = cw_ref[...]
        taps = _taps(ext, SC_KW, TL)
        u = _conv_fwd(taps, w)
        acc_ref[...] += _mm_tn((gb * u).astype(BF16), dmb)
        dgb = dq * u
        du = dq * gb
        dcw_ref[...] += _conv_bwd_w(taps, du)
        extd = jnp.concatenate([du, carry_ref[...]], axis=0)
        dp = _conv_bwd_in(extd, w, SC_KW, TL)
        carry_ref[...] = du[:HALO, :]
        dbcv_ref[...] = jnp.concatenate([dgb, dp * v, dp * gc], axis=1).astype(BF16)

        @pl.when(i == nt - 1)
        def _():
            _emit_row_shards(acc_ref, dwout_ref, stage_ref)

    return _pcall(body, name, (nt,),
                  [_rtile(W, nt), _rtile(W, nt), _rtile(3 * W, nt), _halo_before(3 * W, nt, True),
                   _small((1, W)), _res((W, W)), _small((SC_KW, W))],
                  [_rtile(3 * W, nt), _ANY, _small((1, W)), _small((SC_KW, W))],
                  [_sds((L, 3 * W), BF16), _sds((N_DEV, rows, W), BF16), _sds((1, W)), _sds((SC_KW, W))],
                  [pltpu.VMEM((W, W), F32), pltpu.VMEM((HALO, W), F32), pltpu.VMEM((rows, W), BF16)],
                  [dxo, m, bcv, bcv, g_post, w_out, conv_w], jobs)


def ssd_inproj(x, g_pre, w_in, conv_w, conv_b, name, jobs=()):
    L = x.shape[0]
    nt = L // TL

    def body(x_ref, gpre_ref, win_ref, cw_ref, cb_ref, z_ref, raw_ref, dt_ref, xh_ref, bm_ref, cm_ref, carry_ref):
        i = pl.program_id(0)

        @pl.when(i == 0)
        def _():
            carry_ref[...] = jnp.zeros_like(carry_ref)

        h = _rms(x_ref[...], gpre_ref[...]).astype(BF16)
        zx = _mm_nt(h, win_ref[...])
        z_ref[...] = zx[:, :SSD_DI]
        raw = zx[:, SSD_DI:SSD_DI + SSD_CONV]
        raw_ref[...] = raw
        dt_ref[...] = zx[:, SSD_DI + SSD_CONV:SSD_IN]
        ext = jnp.concatenate([carry_ref[...], raw], axis=0)
        pre = _conv_fwd(_taps(ext, SSD_KW, TL), cw_ref[...]) + cb_ref[...]
        carry_ref[...] = raw[TL - HALO:, :]
        act = pre * jax.nn.sigmoid(pre)
        for hh in range(SSD_H):
            xh_ref[hh] = act[:, hh * SSD_P:(hh + 1) * SSD_P]
        for g in range(SSD_G):
            bm_ref[g] = act[:, SSD_DI + g * SSD_N:SSD_DI + (g + 1) * SSD_N]
            cm_ref[g] = act[:, SSD_DI + (SSD_G + g) * SSD_N:SSD_DI + (SSD_G + g + 1) * SSD_N]

    return _pcall(body, name, (nt,),
                  [_tile(D_MODEL), _small((1, D_MODEL)), _res((SSD_IN_PAD, D_MODEL)), _small((SSD_KW, SSD_CONV)),
                   _small((1, SSD_CONV))],
                  [_tile(SSD_DI), _tile(SSD_CONV), _tile(SSD_H),
                   pl.BlockSpec((SSD_H, TL, SSD_P), lambda i: (0, i, 0)),
                   pl.BlockSpec((SSD_G, TL, SSD_N), lambda i: (0, i, 0)),
                   pl.BlockSpec((SSD_G, TL, SSD_N), lambda i: (0, i, 0))],
                  [_sds((L, SSD_DI)), _sds((L, SSD_CONV)), _sds((L, SSD_H)), _sds((SSD_H, L, SSD_P)),
                   _sds((SSD_G, L, SSD_N)), _sds((SSD_G, L, SSD_N))],
                  [pltpu.VMEM((HALO, SSD_CONV), F32)],
                  [x, g_pre, w_in, conv_w, conv_b], jobs)


def _per_head(v, heads):
    return jnp.stack([v[:, h:h + 1] for h in heads], axis=0)


def _heads_to_lanes(cols):
    return jnp.concatenate(cols, axis=1)


def _rep_heads(v):
    g, a, b = v.shape
    return jnp.broadcast_to(v[:, None], (g, SSD_R, a, b)).reshape(g * SSD_R, a, b)


def _sum_heads(v):
    h, a, b = v.shape
    return v.reshape(SSD_G, SSD_R, a, b).sum(axis=1)


def _chunk_terms(dtr, bias, a_log):
    T = CHUNK
    dt = jax.nn.softplus(dtr + bias)
    a_head = -jnp.exp(a_log)
    ii = lax.broadcasted_iota(jnp.int32, (T, T), 0)
    jj = lax.broadcasted_iota(jnp.int32, (T, T), 1)
    tri = ii >= jj
    cs = jnp.dot(tri.astype(F32), dt * a_head, precision=lax.Precision.HIGHEST, preferred_element_type=F32)
    return dict(dt=dt, a_head=a_head, tri=tri, cs=cs, cs_t=cs.T)


def _head_terms(ct, heads):
    T = CHUNK
    cs, cs_t, tri = ct["cs"], ct["cs_t"], ct["tri"]
    csc = _per_head(cs, heads)
    csr = jnp.stack([cs_t[h:h + 1, :] for h in heads], axis=0)
    cl = _per_head(cs[T - 1:T, :], heads)
    lmat = jnp.exp(jnp.where(tri[None], csc - csr, -jnp.inf))
    return dict(dtc=_per_head(ct["dt"], heads), lmat=lmat, ecs=jnp.exp(csc), dsc=jnp.exp(cl - csc), cdc=jnp.exp(cl))


def ssd_scan_fwd(xh, bm, cm, dt_raw, dt_bias, a_log, d_skip, name, jobs=()):
    L = xh.shape[1]
    nc = L // CHUNK
    T = CHUNK
    TS = SCAN_CPS * T

    def body(xh_ref, bm_ref, cm_ref, dt_ref, bias_ref, alog_ref, dsk_ref, y_ref, sp_ref, st_ref):
        c = pl.program_id(0)

        @pl.when(c == 0)
        def _():
            st_ref[...] = jnp.zeros_like(st_ref)

        heads = range(SSD_H)
        dh = _per_head(dsk_ref[...], heads)
        s = st_ref[...]
        for k in range(SCAN_CPS):
            rows = slice(k * T, (k + 1) * T)
            ht = _head_terms(_chunk_terms(dt_ref[rows, :], bias_ref[...], alog_ref[...]), heads)
            x = xh_ref[:, rows, :]
            bgb = bm_ref[:, rows, :].astype(BF16)
            cgb = cm_ref[:, rows, :].astype(BF16)
            bh = _rep_heads(bgb)
            ch = _rep_heads(cgb)
            xt = x * ht["dtc"]
            cb = jnp.einsum("gln,gsn->gls", cgb, bgb, preferred_element_type=F32)
            mb = (_rep_heads(cb) * ht["lmat"]).astype(BF16)
            yd = jnp.einsum("hls,hsp->hlp", mb, xt.astype(BF16), preferred_element_type=F32)
            sb = s.astype(BF16)
            yo = jnp.einsum("hln,hpn->hlp", ch, sb, preferred_element_type=F32) * ht["ecs"]
            y_ref[:, rows, :] = yd + yo + x * dh
            sp_ref[k] = sb
            xd = (xt * ht["dsc"]).astype(BF16)
            s = s * ht["cdc"] + jnp.einsum("htp,htn->hpn", xd, bh, preferred_element_type=F32)
        st_ref[...] = s

    hd = pl.BlockSpec((SSD_H, TS, SSD_P), lambda c: (0, c, 0))
    gr = pl.BlockSpec((SSD_G, TS, SSD_N), lambda c: (0, c, 0))
    return _pcall(body, name, (nc // SCAN_CPS,),
                  [hd, gr, gr, pl.BlockSpec((TS, SSD_H), lambda c: (c, 0)),
                   _small((1, SSD_H)), _small((1, SSD_H)), _small((1, SSD_H))],
                  [hd, pl.BlockSpec((SCAN_CPS, SSD_H, SSD_P, SSD_N), lambda c: (c, 0, 0, 0))],
                  [_sds((SSD_H, L, SSD_P)), _sds((nc, SSD_H, SSD_P, SSD_N), BF16)],
                  [pltpu.VMEM((SSD_H, SSD_P, SSD_N), F32)],
                  [xh, bm, cm, dt_raw, dt_bias, a_log, d_skip], jobs)


def ssd_scan_bwd(dy, xh, bm, cm, dt_raw, sprev, dt_bias, a_log, d_skip, name, jobs=()):
    L = xh.shape[1]
    nc = L // CHUNK
    T = CHUNK

    def body(dy_ref, xh_ref, bm_ref, cm_ref, dt_ref, sp_ref, bias_ref, alog_ref, dsk_ref,
             dxh_ref, dbm_ref, dcm_ref, ddt_ref, dbias_ref, dalog_ref, ddsk_ref, g_ref):
        i = pl.program_id(0)

        @pl.when(i == 0)
        def _():
            g_ref[...] = jnp.zeros_like(g_ref)
            dbias_ref[...] = jnp.zeros_like(dbias_ref)
            dalog_ref[...] = jnp.zeros_like(dalog_ref)
            ddsk_ref[...] = jnp.zeros_like(ddsk_ref)

        bias = bias_ref[...]
        heads = range(SSD_H)
        dh = _per_head(dsk_ref[...], heads)
        g = g_ref[...]
        for k in reversed(range(SCAN_CPS)):
            rows = slice(k * T, (k + 1) * T)
            dtr = dt_ref[rows, :]
            ct = _chunk_terms(dtr, bias, alog_ref[...])
            dt, a_head, tri = ct["dt"], ct["a_head"], ct["tri"]
            ht = _head_terms(ct, heads)
            dtc, lmat, ecs, dsc, cdc = ht["dtc"], ht["lmat"], ht["ecs"], ht["dsc"], ht["cdc"]
            x = xh_ref[:, rows, :]
            dyv = dy_ref[:, rows, :]
            dyb = dyv.astype(BF16)
            bgb = bm_ref[:, rows, :].astype(BF16)
            cgb = cm_ref[:, rows, :].astype(BF16)
            bh = _rep_heads(bgb)
            ch = _rep_heads(cgb)
            sb = sp_ref[k]
            gb = g.astype(BF16)
            xt = x * dtc
            xtb = xt.astype(BF16)
            mf = _rep_heads(jnp.einsum("gln,gsn->gls", cgb, bgb, preferred_element_type=F32)) * lmat
            mb = mf.astype(BF16)
            ddsk = jnp.sum(dyv * x, axis=(1, 2), keepdims=True)
            dx = dyv * dh
            yo_raw = jnp.einsum("hln,hpn->hlp", ch, sb, preferred_element_type=F32)
            w1 = dyv * ecs
            w1b = w1.astype(BF16)
            ds_off = jnp.einsum("hlp,hln->hpn", w1b, ch, preferred_element_type=F32)
            dch = jnp.einsum("hlp,hpn->hln", w1b, sb, preferred_element_type=F32)
            dcs_c = jnp.sum(w1 * yo_raw, axis=2, keepdims=True)
            dm = jnp.einsum("hlp,hsp->hls", dyb, xtb, preferred_element_type=F32)
            dxt = jnp.einsum("hls,hlp->hsp", mb, dyb, preferred_element_type=F32)
            dcbb = _sum_heads(dm * lmat).astype(BF16)
            dseg = dm * mf
            dcs_c = dcs_c + jnp.sum(dseg, axis=2, keepdims=True)
            dcs_r = -jnp.sum(dseg, axis=1, keepdims=True)
            dc = jnp.einsum("gls,gsn->gln", dcbb, bgb, preferred_element_type=F32) + _sum_heads(dch)
            db = jnp.einsum("gls,gln->gsn", dcbb, cgb, preferred_element_type=F32)
            xd = xt * dsc
            dxd = jnp.einsum("htn,hpn->htp", bh, gb, preferred_element_type=F32)
            db = db + _sum_heads(jnp.einsum("htp,hpn->htn", xd.astype(BF16), gb, preferred_element_type=F32))
            dxt = dxt + dxd * dsc
            d_ds = jnp.sum(dxd * xt, axis=2, keepdims=True)
            d_cd = jnp.sum(g * sb.astype(F32), axis=(1, 2), keepdims=True)
            g = g * cdc + ds_off
            t1 = d_ds * dsc
            dcs_c = dcs_c - t1
            dcl = jnp.sum(t1, axis=1, keepdims=True) + d_cd * cdc
            ddt_c = jnp.sum(dxt * x, axis=2, keepdims=True)
            dxh_ref[:, rows, :] = dx + dxt * dtc
            dbm_ref[:, rows, :] = db
            dcm_ref[:, rows, :] = dc
            lanes = lambda v: _heads_to_lanes([v[h] for h in heads])
            rows_t = jnp.concatenate([dcs_r[h] for h in heads], axis=0).T
            last = (lax.broadcasted_iota(jnp.int32, (T, 1), 0) == T - 1).astype(F32)
            dcs = lanes(dcs_c) + rows_t + last * lanes(dcl)
            da = lax.dot_general(tri.astype(F32), dcs, (((0,), (0,)), ((), ())),
                                 precision=lax.Precision.HIGHEST, preferred_element_type=F32)
            ddt = da * a_head + lanes(ddt_c)
            dalog_ref[...] += jnp.sum(da * dt, axis=0, keepdims=True)
            ddtr = ddt * jax.nn.sigmoid(dtr + bias)
            ddt_ref[rows, :] = ddtr
            dbias_ref[...] += jnp.sum(ddtr, axis=0, keepdims=True)
            ddsk_ref[...] += lanes(ddsk)
        g_ref[...] = g

        @pl.when(i == nb - 1)
        def _():
            dalog_ref[...] = dalog_ref[...] * (-jnp.exp(alog_ref[...]))

    nb = nc // SCAN_CPS
    TS = SCAN_CPS * T
    hd = pl.BlockSpec((SSD_H, TS, SSD_P), lambda i: (0, nb - 1 - i, 0))
    gr = pl.BlockSpec((SSD_G, TS, SSD_N), lambda i: (0, nb - 1 - i, 0))
    tk = pl.BlockSpec((TS, SSD_H), lambda i: (nb - 1 - i, 0))
    return _pcall(body, name, (nb,),
                  [hd, hd, gr, gr, tk, pl.BlockSpec((SCAN_CPS, SSD_H, SSD_P, SSD_N), lambda i: (nb - 1 - i, 0, 0, 0)),
                   _small((1, SSD_H)), _small((1, SSD_H)), _small((1, SSD_H))],
                  [hd, gr, gr, tk, _small((1, SSD_H)), _small((1, SSD_H)), _small((1, SSD_H))],
                  [_sds((SSD_H, L, SSD_P)), _sds((SSD_G, L, SSD_N)), _sds((SSD_G, L, SSD_N)), _sds((L, SSD_H)),
                   _sds((1, SSD_H)), _sds((1, SSD_H)), _sds((1, SSD_H))],
                  [pltpu.VMEM((SSD_H, SSD_P, SSD_N), F32)],
                  [dy, xh, bm, cm, dt_raw, sprev, dt_bias, a_log, d_skip], jobs)


def _heads_to_tokens(y_ref):
    return jnp.concatenate([y_ref[h] for h in range(SSD_H)], axis=1)


def ssd_out_fwd(x, y, z, norm_w, w_out, g_post, name, jobs=()):
    L = x.shape[0]
    nt = L // TL

    def body(x_ref, y_ref, z_ref, nw_ref, wout_ref, gpost_ref, m_ref, xn_ref):
        z = z_ref[...]
        yg = _heads_to_tokens(y_ref) * (z * jax.nn.sigmoid(z))
        yn = _rms(yg, nw_ref[...]).astype(BF16)
        m = _mm(yn, wout_ref[...])
        m_ref[...] = m
        xn_ref[...] = x_ref[...] + _rms(m, gpost_ref[...])

    return _pcall(body, name, (nt,),
                  [_tile(D_MODEL), pl.BlockSpec((SSD_H, TL, SSD_P), lambda i: (0, i, 0)), _tile(SSD_DI),
                   _small((1, SSD_DI)), _res((SSD_DI, D_MODEL)), _small((1, D_MODEL))],
                  [_tile(D_MODEL), _tile(D_MODEL)],
                  [_sds((L, D_MODEL)), _sds((L, D_MODEL))],
                  [],
                  [x, y, z, norm_w, w_out, g_post], jobs)


def ssd_out_bwd(dxo, m, y, z, norm_w, w_out, g_post, name, jobs=()):
    L = dxo.shape[0]
    nt = L // TL
    rows = SSD_DI // N_DEV

    def body(dxo_ref, m_ref, y_ref, z_ref, nw_ref, wout_ref, gpost_ref,
             dy_ref, dz_ref, dwout_ref, dgp_ref, dnw_ref, acc_ref, stage_ref):
        i = pl.program_id(0)

        @pl.when(i == 0)
        def _():
            acc_ref[...] = jnp.zeros_like(acc_ref)
            dgp_ref[...] = jnp.zeros_like(dgp_ref)
            dnw_ref[...] = jnp.zeros_like(dnw_ref)

        dm, dgp = _rms_bwd(m_ref[...], gpost_ref[...], dxo_ref[...])
        dgp_ref[...] += dgp
        dmb = dm.astype(BF16)
        dyn = _mm_nt(dmb, wout_ref[...])
        z = z_ref[...]
        y = _heads_to_tokens(y_ref)
        sil, dsil = _silu_parts(z)
        yg = y * sil
        nw = nw_ref[...]
        acc_ref[...] += _mm_tn(_rms(yg, nw).astype(BF16), dmb)
        dyg, dnw = _rms_bwd(yg, nw, dyn)
        dnw_ref[...] += dnw
        dyv = dyg * sil
        dz_ref[...] = dyg * y * dsil
        for h in range(SSD_H):
            dy_ref[h] = dyv[:, h * SSD_P:(h + 1) * SSD_P]

        @pl.when(i == nt - 1)
        def _():
            _emit_row_shards(acc_ref, dwout_ref, stage_ref)

    hd = pl.BlockSpec((SSD_H, TL, SSD_P), lambda i: (0, i, 0))
    return _pcall(body, name, (nt,),
                  [_tile(D_MODEL), _tile(D_MODEL), hd, _tile(SSD_DI), _small((1, SSD_DI)), _res((SSD_DI, D_MODEL)),
                   _small((1, D_MODEL))],
                  [hd, _tile(SSD_DI), _ANY, _small((1, D_MODEL)), _small((1, SSD_DI))],
                  [_sds((SSD_H, L, SSD_P)), _sds((L, SSD_DI)), _sds((N_DEV, rows, D_MODEL), BF16),
                   _sds((1, D_MODEL)), _sds((1, SSD_DI))],
                  [pltpu.VMEM((SSD_DI, D_MODEL), F32), pltpu.VMEM((rows, D_MODEL), BF16)],
                  [dxo, m, y, z, norm_w, w_out, g_post], jobs)


def ssd_conv_bwd(dxh, dbm, dcm, xbc_raw, dz, ddt_raw, conv_w, conv_b, name, jobs=()):
    L = xbc_raw.shape[0]
    nt = L // TL

    def body(dxh_ref, dbm_ref, dcm_ref, raw_ref, halo_ref, dz_ref, ddt_ref, cw_ref, cb_ref,
             d_ref, dcw_ref, dcb_ref, carry_ref):
        i = pl.program_id(0)
        t = nt - 1 - i

        @pl.when(i == 0)
        def _():
            carry_ref[...] = jnp.zeros_like(carry_ref)
            dcw_ref[...] = jnp.zeros_like(dcw_ref)
            dcb_ref[...] = jnp.zeros_like(dcb_ref)

        def dact_chunk(rows, c):
            lo = c * CONV_LC
            if lo < SSD_DI:
                per, ref, first = SSD_P, dxh_ref, lo // SSD_P
            elif lo < SSD_DI + SSD_G * SSD_N:
                per, ref, first = SSD_N, dbm_ref, (lo - SSD_DI) // SSD_N
            else:
                per, ref, first = SSD_N, dcm_ref, (lo - SSD_DI - SSD_G * SSD_N) // SSD_N
            return jnp.concatenate([ref[first + q, rows, :] for q in range(CONV_LC // per)], axis=1)

        for c in range(SSD_CONV // CONV_LC):
            cols = slice(c * CONV_LC, (c + 1) * CONV_LC)
            w = cw_ref[:, cols]
            b = cb_ref[:, cols]
            halo = jnp.where(t == 0, 0.0, halo_ref[:, cols])
            carry = carry_ref[:, cols]
            dcb = jnp.zeros((1, CONV_LC), F32)
            dcw = jnp.zeros((SSD_KW, CONV_LC), F32)
            for r in reversed(range(TL // CONV_RC)):
                r0 = r * CONV_RC
                rows = slice(r0, r0 + CONV_RC)
                if r0 == 0:
                    ext = jnp.concatenate([halo, raw_ref[rows, cols]], axis=0)
                else:
                    ext = raw_ref[r0 - HALO:r0 + CONV_RC, cols]
                taps = _taps(ext, SSD_KW, CONV_RC)
                pre = _conv_fwd(taps, w) + b
                _, dsil = _silu_parts(pre)
                dpre = dact_chunk(rows, c) * dsil
                dcb = dcb + jnp.sum(dpre, axis=0, keepdims=True)
                dcw = dcw + _conv_bwd_w(taps, dpre)
                extd = jnp.concatenate([dpre, carry], axis=0)
                draw = _conv_bwd_in(extd, w, SSD_KW, CONV_RC)
                carry = dpre[:HALO, :]
                d_ref[rows, SSD_DI + c * CONV_LC:SSD_DI + (c + 1) * CONV_LC] = draw.astype(BF16)
            carry_ref[:, cols] = carry
            dcb_ref[:, cols] += dcb
            dcw_ref[:, cols] += dcw
        d_ref[:, :SSD_DI] = dz_ref[...].astype(BF16)
        tail = jnp.concatenate([ddt_ref[...], jnp.zeros((TL, SSD_IN_PAD - SSD_IN), F32)], axis=1)
        d_ref[:, SSD_DI + SSD_CONV:] = tail.astype(BF16)

    hd = pl.BlockSpec((SSD_H, TL, SSD_P), lambda i: (0, nt - 1 - i, 0))
    gr = pl.BlockSpec((SSD_G, TL, SSD_N), lambda i: (0, nt - 1 - i, 0))
    return _pcall(body, name, (nt,),
                  [hd, gr, gr, _rtile(SSD_CONV, nt), _halo_before(SSD_CONV, nt, True), _rtile(SSD_DI, nt),
                   _rtile(SSD_H, nt), _small((SSD_KW, SSD_CONV)), _small((1, SSD_CONV))],
                  [_rtile(SSD_IN_PAD, nt), _small((SSD_KW, SSD_CONV)), _small((1, SSD_CONV))],
                  [_sds((L, SSD_IN_PAD), BF16), _sds((SSD_KW, SSD_CONV)), _sds((1, SSD_CONV))],
                  [pltpu.VMEM((HALO, SSD_CONV), F32)],
                  [dxh, dbm, dcm, xbc_raw, xbc_raw, dz, ddt_raw, conv_w, conv_b], jobs)


def loss_fwd_bwd(y, target, name, jobs=()):
    L = y.shape[0]
    nt = L // TL

    def body(y_ref, t_ref, loss_ref, dy_ref):
        i = pl.program_id(0)

        @pl.when(i == 0)
        def _():
            loss_ref[...] = jnp.zeros_like(loss_ref)

        err = y_ref[...] - t_ref[...]
        dy_ref[...] = err * (1.0 / D_MODEL)
        loss_ref[...] += 0.5 * jnp.sum(jnp.mean(err * err, axis=-1, keepdims=True), axis=0, keepdims=True)

    return _pcall(body, name, (nt,),
                  [_tile(D_MODEL), _tile(D_MODEL)],
                  [_small((1, 1)), _tile(D_MODEL)],
                  [_sds((1, 1)), _sds((L, D_MODEL))],
                  [],
                  [y, target], jobs)


def _adamw_math(w, g, m, v):
    m = ADAM_B1 * m + (1.0 - ADAM_B1) * g
    v = ADAM_B2 * v + (1.0 - ADAM_B2) * (g * g)
    m_hat = m / (1.0 - ADAM_B1 ** ADAM_STEP)
    v_hat = v / (1.0 - ADAM_B2 ** ADAM_STEP)
    delta = -ADAM_LR * (m_hat / (jnp.sqrt(v_hat) + ADAM_EPS) + ADAM_WD * w)
    return delta, m, v


def _row_tile(rows):
    for cand in (256, 176, 128, 64, 32, 16, 8):
        if rows % cand == 0:
            return cand
    return rows


def reduce_adamw(recvs, w, m, v, name, jobs=()):
    nl = len(recvs)
    _, R, C = recvs[0].shape
    if R % 16 == 0:
        tr, tc = _row_tile(R), C
    else:
        tr, tc = R, 2 * LANES
    nr = (R // tr) * (C // tc)

    def body(*refs):
        r_refs = refs[:nl]
        w_ref, m_ref, v_ref, g_out, d_out, m_out, v_out = refs[nl:]
        layer = pl.program_id(0) // nr
        for ll in range(nl):
            @pl.when(layer == ll)
            def _(ll=ll):
                g = r_refs[ll][0].astype(F32)
                for j in range(1, N_DEV):
                    g = g + r_refs[ll][j].astype(F32)
                delta, mn, vn = _adamw_math(w_ref[0], g, m_ref[0], v_ref[0])
                g_out[0] = g
                d_out[0] = delta
                m_out[0] = mn
                v_out[0] = vn

    def recv_spec(ll):
        def index(i):
            t = jnp.where(i // nr == ll, i % nr, 0)
            return (0, t, 0) if tc == C else (0, 0, t)
        return pl.BlockSpec((N_DEV, tr, tc), index)

    blk = pl.BlockSpec((1, tr, tc), lambda i: (i // nr, i % nr, 0) if tc == C else (i // nr, 0, i % nr))
    return _pcall(body, name, (nl * nr,),
                  [recv_spec(ll) for ll in range(nl)] + [blk, blk, blk],
                  [blk] * 4,
                  [_sds((nl, R, C))] * 4,
                  [],
                  [*recvs, w, m, v], jobs)


def small_reduce(gathered, name):
    _, R, C = gathered.shape

    def body(r_ref, o_ref):
        g = r_ref[0]
        for j in range(1, N_DEV):
            g = g + r_ref[j]
        o_ref[...] = g

    return pl.pallas_call(body, name=name, out_shape=_sds((R, C)))(gathered)


def small_adamw(g, w, m, v, name):
    def body(g_ref, w_ref, m_ref, v_ref, d_out, m_out, v_out):
        delta, mn, vn = _adamw_math(w_ref[...], g_ref[...], m_ref[...], v_ref[...])
        d_out[...] = delta
        m_out[...] = mn
        v_out[...] = vn

    return pl.pallas_call(body, name=name, out_shape=[_sds(g.shape)] * 3)(g, w, m, v)


def _pack(arrs):
    flat = jnp.concatenate([a.reshape(-1) for a in arrs])
    n = flat.shape[0]
    rows = -(-n // (8 * LANES)) * 8
    flat = jnp.pad(flat, (0, rows * LANES - n))
    return flat.reshape(rows, LANES)


def _unpack(packed, shapes):
    flat = packed.reshape(-1)
    out = []
    off = 0
    for s in shapes:
        n = 1
        for d in s:
            n *= d
        out.append(flat[off:off + n].reshape(s))
        off += n
    return out


def kernel(x, mix_pre_g, mix_post_g, ffn_pre_g, ffn_post_g, ssd_w_in, ssd_conv_w, ssd_conv_b, ssd_dt_bias, ssd_A_log, ssd_D, ssd_norm_w, ssd_w_out, sc_w_in, sc_conv_w, sc_w_out, ffn_w_up, ffn_conv_w, ffn_conv_b, ffn_w_down, loss_target, m_mix_pre_g, m_mix_post_g, m_ffn_pre_g, m_ffn_post_g, m_ssd_w_in, m_ssd_conv_w, m_ssd_conv_b, m_ssd_dt_bias, m_ssd_A_log, m_ssd_D, m_ssd_norm_w, m_ssd_w_out, m_sc_w_in, m_sc_conv_w, m_sc_w_out, m_ffn_w_up, m_ffn_conv_w, m_ffn_conv_b, m_ffn_w_down, v_mix_pre_g, v_mix_post_g, v_ffn_pre_g, v_ffn_post_g, v_ssd_w_in, v_ssd_conv_w, v_ssd_conv_b, v_ssd_dt_bias, v_ssd_A_log, v_ssd_D, v_ssd_norm_w, v_ssd_w_out, v_sc_w_in, v_sc_conv_w, v_sc_w_out, v_ffn_w_up, v_ffn_conv_w, v_ffn_conv_b, v_ffn_w_down):
    me = _my_index()
    x0 = x[0]
    target = loss_target[0]
    row = lambda a: a.reshape(1, -1)

    tr = lambda a: jnp.transpose(a, (0, 2, 1))
    shards = {"ssd_in": tr(ssd_w_in), "ssd_out": ssd_w_out, "sc_in": sc_w_in, "sc_out": sc_w_out,
              "up": tr(ffn_w_up), "down": ffn_w_down}
    col_sharded = ("sc_in",)
    padded_rows = {"ssd_in": SSD_IN_PAD}
    weights = {}

    def shard_bf16(key):
        n, l = key
        return shards[n][l].astype(BF16)

    def store_weights(keys, outs):
        for (n, l), g in zip(keys, outs):
            _, R, C = g.shape
            if n in col_sharded:
                full = jnp.transpose(g, (1, 0, 2)).reshape(R, N_DEV * C)
            else:
                full = g.reshape(N_DEV * R, C)
                if n in padded_rows:
                    full = jnp.pad(full, ((0, padded_rows[n] - N_DEV * R), (0, 0)))
            weights[(n, l)] = full

    fwd_first_half = {
        "ssd_inproj_0": [("ssd_out", 0), ("up", 0), ("down", 0)],
        "ssd_scan_fwd_0": [("sc_in", 0), ("sc_out", 0), ("up", 1)],
        "ssd_out_fwd_0": [("down", 1)],
        "ffn_fwd_0": [("ssd_in", 1)],
        "sc_fwd_0": [("ssd_out", 1), ("down", 2)],
        "ffn_fwd_1": [("up", 2)],
        "ssd_inproj_1": [("sc_in", 1), ("sc_out", 1), ("down", 3)],
        "ssd_scan_fwd_1": [("up", 3)],
    }
    bwd_sched = {
        "ffn_bwd2_3": [("down", 3)], "sc_bwd2_1": [("sc_out", 1)], "ffn_bwd1_2": [("up", 3)], "ffn_bwd2_2": [("sc_in", 1)],
        "ssd_out_bwd_1": [("down", 2)], "ssd_scan_bwd_1": [("up", 2), ("ssd_out", 1)], "ffn_bwd1_1": [("ssd_in", 1)],
        "ffn_bwd2_1": [("down", 1)], "sc_bwd2_0": [("sc_out", 0)], "ffn_bwd1_0": [("up", 1)], "ffn_bwd2_0": [("sc_in", 0)],
        "ssd_out_bwd_0": [("down", 0)], "ssd_scan_bwd_0": [("up", 0), ("ssd_out", 0)],
    }

    first = [("ssd_in", 0)]
    outs = exchange([("ag2", shard_bf16(k)) for k in first]
                    + [("ag", ssd_conv_w), ("ag", sc_conv_w), ("ag", ffn_conv_w)], "ag_first")
    store_weights(first, outs[:1])

    def taps(g):
        _, nl, K, C = g.shape
        return jnp.transpose(g, (1, 2, 0, 3)).reshape(nl, K, N_DEV * C)

    CW_ssd, CW_sc, CW_ffn = taps(outs[1]), taps(outs[2]), taps(outs[3])

    half_done = []

    def fwd(fn, name, *args):
        second = list(half_done)
        starting = fwd_first_half.get(name, [])
        jobs = [("agB", buf) for _, buf in second] + [("agA", shard_bf16(k)) for k in starting]
        res, got = fn(*args, name, jobs)
        store_weights([k for k, _ in second], got[:len(second)])
        half_done[:] = list(zip(starting, got[len(second):]))
        return res

    saved = []
    h = x0
    for i in range(DEPTH):
        j = i // 2
        blk = dict(x_mix=h)
        if i % 2 == 0:
            z, raw, dt_raw, xh, bm, cm = fwd(ssd_inproj, f"ssd_inproj_{j}", h, row(mix_pre_g[i]), weights[("ssd_in", j)],
                                             CW_ssd[j], row(ssd_conv_b[j]))
            y, sprev = fwd(ssd_scan_fwd, f"ssd_scan_fwd_{j}", xh, bm, cm, dt_raw, row(ssd_dt_bias[j]),
                           row(ssd_A_log[j]), row(ssd_D[j]))
            m, h = fwd(ssd_out_fwd, f"ssd_out_fwd_{j}", h, y, z, row(ssd_norm_w[j]), weights[("ssd_out", j)],
                       row(mix_post_g[i]))
            blk.update(z=z, raw=raw, dt_raw=dt_raw, xh=xh, bm=bm, cm=cm, y=y, sprev=sprev, m=m)
        else:
            bcv, m, h = fwd(sc_fwd, f"sc_fwd_{j}", h, row(mix_pre_g[i]), weights[("sc_in", j)], CW_sc[j],
                            weights[("sc_out", j)], row(mix_post_g[i]))
            blk.update(bcv=bcv, m=m)
        blk["x_ffn"] = h
        up, f, h = fwd(ffn_fwd, f"ffn_fwd_{i}", h, row(ffn_pre_g[i]), weights[("up", i)], CW_ffn[i],
                       row(ffn_conv_b[i]), weights[("down", i)], row(ffn_post_g[i]))
        blk.update(up=up, f=f)
        saved.append(blk)

    (loss_dev, dh), _ = loss_fwd_bwd(h, target, "loss")
    loss = lax.psum(loss_dev[0, 0], ("x", "y", "c"))

    parts, recvd = {}, {}

    def bwd(fn, name, *args):
        keys = bwd_sched.get(name, [])
        res, got = fn(*args, name, [("a2a", parts[k]) for k in keys])
        for k, g in zip(keys, got):
            recvd[k] = g
        return res

    g_mix_pre, g_mix_post, g_ffn_pre, g_ffn_post = [None] * DEPTH, [None] * DEPTH, [None] * DEPTH, [None] * DEPTH
    g_ffn_cw, g_ffn_cb = [None] * DEPTH, [None] * DEPTH
    g_ssd_cw, g_ssd_cb, g_ssd_dtb, g_ssd_alog, g_ssd_d, g_ssd_nw = ([None] * 2 for _ in range(6))
    g_sc_cw = [None] * 2
    for i in reversed(range(DEPTH)):
        j = i // 2
        blk = saved[i]
        dup, parts[("down", i)], g_ffn_post[i], g_ffn_cw[i], g_ffn_cb[i] = bwd(
            ffn_bwd1, f"ffn_bwd1_{i}", dh, blk["f"], blk["up"], row(ffn_post_g[i]), weights[("down", i)], CW_ffn[i],
            row(ffn_conv_b[i]))
        dh, parts[("up", i)], g_ffn_pre[i] = bwd(functools.partial(inproj_bwd, transposed=True), f"ffn_bwd2_{i}",
                                                  blk["x_ffn"], row(ffn_pre_g[i]), dup, weights[("up", i)], dh,
                                                  2 * FFN_F // N_DEV)
        if i % 2 == 0:
            dy, dz, parts[("ssd_out", j)], g_mix_post[i], g_ssd_nw[j] = bwd(
                ssd_out_bwd, f"ssd_out_bwd_{j}", dh, blk["m"], blk["y"], blk["z"], row(ssd_norm_w[j]),
                weights[("ssd_out", j)], row(mix_post_g[i]))
            dxh, dbm, dcm, ddt, g_ssd_dtb[j], g_ssd_alog[j], g_ssd_d[j] = bwd(
                ssd_scan_bwd, f"ssd_scan_bwd_{j}", dy, blk["xh"], blk["bm"], blk["cm"], blk["dt_raw"], blk["sprev"],
                row(ssd_dt_bias[j]), row(ssd_A_log[j]), row(ssd_D[j]))
            d_in, g_ssd_cw[j], g_ssd_cb[j] = bwd(ssd_conv_bwd, f"ssd_conv_bwd_{j}", dxh, dbm, dcm, blk["raw"], dz, ddt,
                                                 CW_ssd[j], row(ssd_conv_b[j]))
            dh, parts[("ssd_in", j)], g_mix_pre[i] = bwd(functools.partial(inproj_bwd, transposed=True),
                                                          f"ssd_bwd2_{j}", blk["x_mix"], row(mix_pre_g[i]), d_in,
                                                          weights[("ssd_in", j)], dh, SSD_IN // N_DEV)
        else:
            dbcv, parts[("sc_out", j)], g_mix_post[i], g_sc_cw[j] = bwd(
                sc_bwd1, f"sc_bwd1_{j}", dh, blk["m"], blk["bcv"], row(mix_post_g[i]), weights[("sc_out", j)], CW_sc[j])
            dh, parts[("sc_in", j)], g_mix_pre[i] = bwd(inproj_bwd, f"sc_bwd2_{j}", blk["x_mix"], row(mix_pre_g[i]),
                                                         dbcv, weights[("sc_in", j)], dh, 3 * D_MODEL // N_DEV)
    grad_x = dh[None]

    st = lambda lst: jnp.concatenate(lst, axis=0)
    small_full = [
        st(g_mix_pre), st(g_mix_post), st(g_ffn_pre), st(g_ffn_post),
        jnp.stack(g_ssd_cw), st(g_ssd_cb), st(g_ssd_dtb), st(g_ssd_alog), st(g_ssd_d), st(g_ssd_nw),
        jnp.stack(g_sc_cw), jnp.stack(g_ffn_cw), st(g_ffn_cb),
    ]
    full_shapes = [a.shape for a in small_full]

    recvd[("ssd_in", 0)], small_gathered = exchange([("a2a", parts[("ssd_in", 0)]), ("ag", _pack(small_full))],
                                                    "a2a_last")

    def finish(n, nl, w, m, v):
        return reduce_adamw([recvd[(n, l)] for l in range(nl)], w, m, v, "adamw_" + n)[0]

    r_up = [tr(a) for a in finish("up", DEPTH, tr(ffn_w_up), tr(m_ffn_w_up), tr(v_ffn_w_up))]
    r_down = finish("down", DEPTH, ffn_w_down, m_ffn_w_down, v_ffn_w_down)
    r_ssd_out = finish("ssd_out", 2, ssd_w_out, m_ssd_w_out, v_ssd_w_out)
    r_sc_in = finish("sc_in", 2, sc_w_in, m_sc_w_in, v_sc_w_in)
    r_sc_out = finish("sc_out", 2, sc_w_out, m_sc_w_out, v_sc_w_out)
    r_ssd_in = [tr(a) for a in finish("ssd_in", 2, tr(ssd_w_in), tr(m_ssd_w_in), tr(v_ssd_w_in))]

    summed = small_reduce(small_gathered, "small_reduce")
    (s_mix_pre, s_mix_post, s_ffn_pre, s_ffn_post, s_ssd_cw, s_ssd_cb, s_ssd_dtb, s_ssd_alog, s_ssd_d, s_ssd_nw,
     s_sc_cw, s_ffn_cw, s_ffn_cb) = _unpack(summed, full_shapes)

    def my_cols(a, width):
        return lax.dynamic_slice_in_dim(a, me * width, width, axis=a.ndim - 1)

    s_ssd_cw = my_cols(s_ssd_cw, SSD_CONV // N_DEV)
    s_sc_cw = my_cols(s_sc_cw, D_MODEL // N_DEV)
    s_ffn_cw = my_cols(s_ffn_cw, FFN_F // N_DEV)

    small_g = [s_mix_pre, s_mix_post, s_ffn_pre, s_ffn_post, s_ssd_cw, s_ssd_cb, s_ssd_dtb, s_ssd_alog, s_ssd_d,
               s_ssd_nw, s_sc_cw, s_ffn_cw, s_ffn_cb]
    small_w = [mix_pre_g, mix_post_g, ffn_pre_g, ffn_post_g, ssd_conv_w, ssd_conv_b, ssd_dt_bias, ssd_A_log, ssd_D,
               ssd_norm_w, sc_conv_w, ffn_conv_w, ffn_conv_b]
    small_m = [m_mix_pre_g, m_mix_post_g, m_ffn_pre_g, m_ffn_post_g, m_ssd_conv_w, m_ssd_conv_b, m_ssd_dt_bias,
               m_ssd_A_log, m_ssd_D, m_ssd_norm_w, m_sc_conv_w, m_ffn_conv_w, m_ffn_conv_b]
    small_v = [v_mix_pre_g, v_mix_post_g, v_ffn_pre_g, v_ffn_post_g, v_ssd_conv_w, v_ssd_conv_b, v_ssd_dt_bias,
               v_ssd_A_log, v_ssd_D, v_ssd_norm_w, v_sc_conv_w, v_ffn_conv_w, v_ffn_conv_b]
    local_shapes = [a.shape for a in small_w]
    pd, pm, pv = small_adamw(_pack(small_g), _pack(small_w), _pack(small_m), _pack(small_v), "small_adamw")
    sd = _unpack(pd, local_shapes)
    sm = _unpack(pm, local_shapes)
    sv = _unpack(pv, local_shapes)

    def ordered(small, big):
        (mix_pre, mix_post, ffn_pre, ffn_post, ssd_cw, ssd_cb, dtb, alog, dsk, nw, sc_cw, ffn_cw, ffn_cb) = small
        (b_ssd_in, b_ssd_out, b_sc_in, b_sc_out, b_up, b_down) = big
        return [mix_pre, mix_post, ffn_pre, ffn_post, b_ssd_in, ssd_cw, ssd_cb, dtb, alog, dsk, nw, b_ssd_out,
                b_sc_in, sc_cw, b_sc_out, b_up, ffn_cw, ffn_cb, b_down]

    bigs = [r_ssd_in, r_ssd_out, r_sc_in, r_sc_out, r_up, r_down]
    grads = ordered(small_g, [r[0] for r in bigs])
    deltas = ordered(sd, [r[1] for r in bigs])
    new_m = ordered(sm, [r[2] for r in bigs])
    new_v = ordered(sv, [r[3] for r in bigs])
    return (loss, grad_x, *grads, *deltas, *new_m, *new_v)
```
